```python
import math
import jax, jax.numpy as jnp
from jax import lax
import numpy as np

D_MODEL = 2048
BATCH = 4
SEQ = 4096
DEPTH = 4

N_MIXERS = 4
GROUP_WIDTH = D_MODEL // N_MIXERS
D_MIX = N_MIXERS * GROUP_WIDTH
NORM_EPS = 1e-6

ATTN_HEAD_DIM = 64
ATTN_Q_HEADS = GROUP_WIDTH // ATTN_HEAD_DIM
ATTN_KV_HEADS = 2
ATTN_GROUP = ATTN_Q_HEADS // ATTN_KV_HEADS
ATTN_WINDOW = 128
ATTN_BLOCK = 128
ATTN_Q_WIDTH = ATTN_Q_HEADS * ATTN_HEAD_DIM
ATTN_KV_WIDTH = ATTN_KV_HEADS * ATTN_HEAD_DIM
ATTN_IN = ATTN_Q_WIDTH + 2 * ATTN_KV_WIDTH

RWKV_HEAD_DIM = 64
RWKV_HEADS = GROUP_WIDTH // RWKV_HEAD_DIM
RWKV_DECAY_RANK = 64
RWKV_ICLR_RANK = 64
RWKV_GATE_RANK = 128
RWKV_IN = 3 * GROUP_WIDTH + RWKV_DECAY_RANK + RWKV_ICLR_RANK + RWKV_GATE_RANK
RWKV_LN_EPS = 64e-5

SSD_HEAD_DIM = 64
SSD_HEADS = GROUP_WIDTH // SSD_HEAD_DIM
SSD_GROUPS = 2
SSD_HEADS_PER_GROUP = SSD_HEADS // SSD_GROUPS
SSD_STATE = 128
SSD_CONV = 4
SSD_CHUNK = 128
SSD_CONV_CH = GROUP_WIDTH + 2 * SSD_GROUPS * SSD_STATE
SSD_IN = GROUP_WIDTH + SSD_CONV_CH + SSD_HEADS
SSD_NORM_EPS = 1e-5

S5_GROUP_CH = 16
S5_GROUPS = GROUP_WIDTH // S5_GROUP_CH
S5_STATE = 64
S5_IN = GROUP_WIDTH

IN_WIDTH = ATTN_IN + RWKV_IN + SSD_IN + S5_IN

D_FF = 11 * D_MODEL // 4
N_EXPERTS = 8
TOP_K = 2
D_EXPERT = D_FF // 2
MOE_BLOCK = 512
N_DENSE = (DEPTH + 1) // 2
N_MOE = DEPTH // 2

kernel_name = 'hybrid_parallel_mixer_trunk'


def rms_norm(x, w, eps=NORM_EPS):
    xf = x.astype(jnp.float32)
    y = xf * lax.rsqrt(jnp.mean(xf * xf, axis=-1, keepdims=True) + eps)
    return (y * w.astype(jnp.float32)).astype(x.dtype)


def ada_modulation(c_act, w, b):
    mod = (c_act @ w + b)[:, None, :]
    return jnp.split(mod, 3, axis=-1)


def alibi_slopes(n):
    return jnp.exp2(-8.0 * jnp.arange(1, n + 1, dtype=jnp.float32) / n)


def sliding_window_attention(q, k, v, sinks):
    b, l, _, hd = q.shape
    nb = l // ATTN_BLOCK
    qb = q.reshape(b, nb, ATTN_BLOCK, ATTN_KV_HEADS, ATTN_GROUP, hd)
    pad = jnp.zeros((b, ATTN_BLOCK, ATTN_KV_HEADS, hd), k.dtype)

    def band(t):
        tp = jnp.concatenate([pad, t], axis=1).reshape(b, nb + 1, ATTN_BLOCK, ATTN_KV_HEADS, hd)
        return jnp.concatenate([tp[:, :-1], tp[:, 1:]], axis=2)

    kb, vb = band(k), band(v)
    s = jnp.einsum('bnqkgd,bnskd->bnkgqs', qb, kb,
                   preferred_element_type=jnp.float32) * (hd ** -0.5)
    qi = jnp.arange(ATTN_BLOCK)[:, None] + ATTN_BLOCK
    kj = jnp.arange(2 * ATTN_BLOCK)[None, :]
    dist = qi - kj
    key_pos = jnp.arange(nb)[:, None, None] * ATTN_BLOCK - ATTN_BLOCK + kj[None]
    valid = ((dist >= 0) & (dist < ATTN_WINDOW))[None] & (key_pos >= 0)
    slopes = alibi_slopes(ATTN_Q_HEADS).reshape(ATTN_KV_HEADS, ATTN_GROUP)
    s = s - slopes[:, :, None, None] * dist.astype(jnp.float32)
    s = jnp.where(valid[None, :, None, None], s, -jnp.inf)
    sink = sinks.astype(jnp.float32).reshape(ATTN_KV_HEADS, ATTN_GROUP)[:, :, None, None]
    m = jnp.maximum(jnp.max(s, axis=-1, keepdims=True), sink)
    p = jnp.exp(s - m)
    denom = jnp.sum(p, axis=-1, keepdims=True) + jnp.exp(sink - m)
    o = jnp.einsum('bnkgqs,bnskd->bnqkgd', p / denom, vb.astype(jnp.float32))
    return o.reshape(b, l, ATTN_Q_HEADS * hd)


def token_shift(p):
    return jnp.pad(p, ((0, 0), (1, 0), (0, 0)))[:, :-1]


def rwkv7_time_mix(p, mu, w0, w_up, a0, a_up, g_up, k_k, k_a, r_k, ln_w, ln_b):
    b, l, _ = p.shape
    f32 = jnp.float32
    p = p + mu * (token_shift(p) - p)
    gw = GROUP_WIDTH
    r, k, v, wd, ad, gd = jnp.split(
        p, [gw, 2 * gw, 3 * gw, 3 * gw + RWKV_DECAY_RANK,
            3 * gw + RWKV_DECAY_RANK + RWKV_ICLR_RANK], axis=-1)
    w = -jax.nn.softplus(-(w0 + jnp.tanh(wd) @ w_up)) - 0.5
    decay = jnp.exp(-jnp.exp(w.astype(f32)))
    a = jax.nn.sigmoid(a0 + ad @ a_up)
    g = jax.nn.sigmoid(gd) @ g_up

    def heads(t):
        return t.reshape(b, l, RWKV_HEADS, RWKV_HEAD_DIM).astype(f32)

    kk = heads(k * k_k)
    kk = kk / jnp.maximum(jnp.linalg.norm(kk, axis=-1, keepdims=True), 1e-12)
    k = k * (1 + (a - 1) * k_a)
    r_h, k_h, v_h, w_h, a_h = heads(r), heads(k), heads(v), heads(decay), heads(a)

    def step(state, inp):
        r_t, w_t, k_t, v_t, a_t, b_t = inp
        state = (state * w_t[:, :, None, :]
                 + jnp.einsum('bhvk,bhk->bhv', state, a_t)[..., None] * b_t[:, :, None, :]
                 + v_t[..., None] * k_t[:, :, None, :])
        return state, jnp.einsum('bhvk,bhk->bhv', state, r_t)

    def seq_first(t):
        return jnp.moveaxis(t, 1, 0)

    s0 = jnp.zeros((b, RWKV_HEADS, RWKV_HEAD_DIM, RWKV_HEAD_DIM), f32)
    _, y = lax.scan(step, s0, (seq_first(r_h), seq_first(w_h), seq_first(k_h), seq_first(v_h),
                               seq_first(-kk), seq_first(kk * a_h)))
    y = jnp.moveaxis(y, 0, 1)
    mean = jnp.mean(y, axis=-1, keepdims=True)
    var = jnp.mean(jnp.square(y - mean), axis=-1, keepdims=True)
    y = ((y - mean) * lax.rsqrt(var + RWKV_LN_EPS)).reshape(b, l, gw) * ln_w + ln_b
    bonus = jnp.sum(r_h * k_h * r_k, axis=-1, keepdims=True) * v_h
    return (y + bonus.reshape(b, l, gw)) * g


def segsum(a):
    t = a.shape[-1]
    cs = jnp.cumsum(a, axis=-1)
    ss = cs[..., :, None] - cs[..., None, :]
    return jnp.where(jnp.tril(jnp.ones((t, t), dtype=bool)), ss, -jnp.inf)


def mamba2_ssd(p, conv_w, conv_b, dt_bias, a_log, d_skip, norm_w):
    b, l, _ = p.shape
    f32 = jnp.float32
    gw = GROUP_WIDTH
    z, xbc, dt = jnp.split(p, [gw, gw + SSD_CONV_CH], axis=-1)
    xbc = lax.conv_general_dilated(
        xbc, conv_w[:, None, :].astype(xbc.dtype), window_strides=(1,),
        padding=[(SSD_CONV - 1, 0)], dimension_numbers=('NWC', 'WIO', 'NWC'),
        feature_group_count=SSD_CONV_CH)
    xbc = jax.nn.silu(xbc + conv_b)
    x, bm, cm = jnp.split(xbc, [gw, gw + SSD_GROUPS * SSD_STATE], axis=-1)
    nc = l // SSD_CHUNK
    g, e = SSD_GROUPS, SSD_HEADS_PER_GROUP
    dt = jax.nn.softplus((dt + dt_bias).astype(f32))
    a = -jnp.exp(a_log.astype(f32))
    xh = x.astype(f32).reshape(b, nc, SSD_CHUNK, g, e, SSD_HEAD_DIM)
    dtc = dt.reshape(b, nc, SSD_CHUNK, g, e)
    xdt = xh * dtc[..., None]
    da = jnp.moveaxis(dtc * a.reshape(g, e), 2, -1)
    bc = bm.astype(f32).reshape(b, nc, SSD_CHUNK, g, SSD_STATE)
    cc = cm.astype(f32).reshape(b, nc, SSD_CHUNK, g, SSD_STATE)
    da_cs = jnp.cumsum(da, axis=-1)
    lmat = jnp.exp(segsum(da))
    cb = jnp.einsum('bclgn,bcsgn->bcgls', cc, bc)
    y_diag = jnp.einsum('bcgls,bcgels,bcsgep->bclgep', cb, lmat, xdt)
    decay_states = jnp.exp(da_cs[..., -1:] - da_cs)
    states = jnp.einsum('bclgn,bcgel,bclgep->bcgepn', bc, decay_states, xdt)
    chunk_a = jnp.moveaxis(da_cs[..., -1], 1, -1)
    dec = jnp.exp(segsum(jnp.pad(chunk_a, ((0, 0), (0, 0), (0, 0), (1, 0)))))
    states = jnp.concatenate([jnp.zeros_like(states[:, :1]), states], axis=1)
    states = jnp.einsum('bgezc,bcgepn->bzgepn', dec, states)[:, :-1]
    y_off = jnp.einsum('bclgn,bcgepn,bcgel->bclgep', cc, states, jnp.exp(da_cs))
    y = (y_diag + y_off).reshape(b, l, SSD_HEADS, SSD_HEAD_DIM)
    y = y + x.astype(f32).reshape(b, l, SSD_HEADS, SSD_HEAD_DIM) * d_skip.astype(f32)[:, None]
    y = y.reshape(b, l, SSD_GROUPS, gw // SSD_GROUPS) * jax.nn.silu(
        z.astype(f32)).reshape(b, l, SSD_GROUPS, gw // SSD_GROUPS)
    y = y * lax.rsqrt(jnp.mean(y * y, axis=-1, keepdims=True) + SSD_NORM_EPS)
    return y.reshape(b, l, gw) * norm_w


def s5_mix(u, lam_re, lam_im, log_step, b_re, b_im, c_re, c_im, d_skip, glu_w, glu_b):
    bsz, l, _ = u.shape
    f32 = jnp.float32
    uf = u.astype(f32)
    ug = uf.reshape(bsz, l, S5_GROUPS, S5_GROUP_CH)
    step = jnp.exp(log_step.astype(f32))[:, None]
    lr, li = lam_re.astype(f32), lam_im.astype(f32)
    mag = jnp.exp(lr * step)
    ab_re, ab_im = mag * jnp.cos(li * step), mag * jnp.sin(li * step)
    den = lr * lr + li * li
    coef_re = ((ab_re - 1) * lr + ab_im * li) / den
    coef_im = (ab_im * lr - (ab_re - 1) * li) / den
    bb_re = coef_re[..., None] * b_re - coef_im[..., None] * b_im
    bb_im = coef_re[..., None] * b_im + coef_im[..., None] * b_re
    bu_re = jnp.einsum('blgi,gni->blgn', ug, bb_re)
    bu_im = jnp.einsum('blgi,gni->blgn', ug, bb_im)
    a_re = jnp.broadcast_to(ab_re, (1, l) + ab_re.shape)
    a_im = jnp.broadcast_to(ab_im, (1, l) + ab_im.shape)

    def combine(e1, e2):
        a1r, a1i, b1r, b1i = e1
        a2r, a2i, b2r, b2i = e2
        return (a1r * a2r - a1i * a2i, a1r * a2i + a1i * a2r,
                a2r * b1r - a2i * b1i + b2r, a2r * b1i + a2i * b1r + b2i)

    _, _, h_re, h_im = lax.associative_scan(combine, (a_re, a_im, bu_re, bu_im), axis=1)
    y = jnp.einsum('gin,blgn->blgi', c_re, h_re) - jnp.einsum('gin,blgn->blgi', c_im, h_im)
    y = jax.nn.gelu(y.reshape(bsz, l, GROUP_WIDTH) + d_skip * uf)
    return y * jax.nn.sigmoid(y @ glu_w + glu_b)


def swiglu(h, w_gate, w_up, w_down):
    return (jax.nn.silu(h @ w_gate) * (h @ w_up)) @ w_down


def routed_swiglu(h, router_w, router_b, w_gate, w_up, w_down):
    t, d = h.shape
    f32 = jnp.float32
    logits = (h @ router_w + router_b).astype(f32)
    top_logit, top_idx = lax.top_k(logits, TOP_K)
    top_p = jax.nn.softmax(top_logit, axis=-1)
    n_assign = t * TOP_K
    flat_e = top_idx.reshape(-1)
    flat_tok = jnp.repeat(jnp.arange(t, dtype=jnp.int32), TOP_K)
    order = jnp.argsort(flat_e)
    e_sorted = flat_e[order]
    counts = jnp.bincount(flat_e, length=N_EXPERTS)
    padded = (counts + MOE_BLOCK - 1) // MOE_BLOCK * MOE_BLOCK
    pad_end = jnp.cumsum(padded)
    start = jnp.cumsum(counts) - counts
    dest = (pad_end - padded)[e_sorted] + jnp.arange(n_assign) - start[e_sorted]
    n_blocks = -(-n_assign // MOE_BLOCK) + N_EXPERTS
    rows = n_blocks * MOE_BLOCK
    row_tok = jnp.full((rows,), t, jnp.int32).at[dest].set(flat_tok[order])
    row_p = jnp.zeros((rows,), f32).at[dest].set(top_p.reshape(-1)[order])
    block_e = jnp.minimum(jnp.searchsorted(pad_end, jnp.arange(n_blocks) * MOE_BLOCK, side='right'),
                          N_EXPERTS - 1)
    h_pad = jnp.concatenate([h, jnp.zeros((1, d), h.dtype)], axis=0)
    xb = h_pad[row_tok].reshape(n_blocks, MOE_BLOCK, d)
    yb = lax.map(lambda a: swiglu(a[0], w_gate[a[1]], w_up[a[1]], w_down[a[1]]), (xb, block_e))
    y = jnp.zeros((t + 1, d), f32).at[row_tok].add(yb.reshape(rows, d).astype(f32) * row_p[:, None])
    return y[:t]


def setup_inputs(seed: int = 0) -> dict:
    key = jax.random.key(seed)
    ks = iter(jax.random.split(key, 64))
    f32 = jnp.float32
    L = DEPTH
    gw = GROUP_WIDTH

    def nrm(shape, scale):
        return jax.random.normal(next(ks), shape, f32) * scale

    def unif(shape, lo, hi):
        return jax.random.uniform(next(ks), shape, f32, lo, hi)

    ramp = (jnp.arange(gw, dtype=f32) / (gw - 1)) ** 0.85
    dt0 = jnp.exp(unif((L, SSD_HEADS), math.log(1e-3), math.log(1e-1)))
    return {
        'x': nrm((BATCH, SEQ, D_MODEL), 1.0),
        'c': nrm((BATCH, D_MODEL), 1.0),
        'ada_w': nrm((L, 2, D_MODEL, 3 * D_MODEL), 0.5 * D_MODEL ** -0.5),
        'ada_b': nrm((L, 2, 3 * D_MODEL), 0.02),
        'norm_pre': 1.0 + nrm((L, 2, D_MODEL), 0.02),
        'norm_post': 1.0 + nrm((L, 2, D_MODEL), 0.02),
        'w_in': nrm((L, D_MODEL, IN_WIDTH), D_MODEL ** -0.5),
        'w_out': nrm((L, D_MIX, D_MODEL), D_MIX ** -0.5),
        'attn_sink': nrm((L, ATTN_Q_HEADS), 1.0),
        'rwkv_mu': unif((L, RWKV_IN), 0.0, 1.0),
        'rwkv_w0': -6.0 + 5.0 * ramp + nrm((L, gw), 0.1),
        'rwkv_w_up': nrm((L, RWKV_DECAY_RANK, gw), 0.1 * RWKV_DECAY_RANK ** -0.5),
        'rwkv_a0': nrm((L, gw), 0.1),
        'rwkv_a_up': nrm((L, RWKV_ICLR_RANK, gw), 0.1 * RWKV_ICLR_RANK ** -0.5),
        'rwkv_g_up': nrm((L, RWKV_GATE_RANK, gw), RWKV_GATE_RANK ** -0.5),
        'rwkv_k_k': 0.85 + nrm((L, gw), 0.02),
        'rwkv_k_a': 1.0 + nrm((L, gw), 0.02),
        'rwkv_r_k': nrm((L, RWKV_HEADS, RWKV_HEAD_DIM), 0.1),
        'rwkv_ln_w': 1.0 + nrm((L, gw), 0.02),
        'rwkv_ln_b': nrm((L, gw), 0.02),
        'ssd_conv_w': nrm((L, SSD_CONV, SSD_CONV_CH), SSD_CONV ** -0.5),
        'ssd_conv_b': nrm((L, SSD_CONV_CH), 0.02),
        'ssd_dt_bias': dt0 + jnp.log(-jnp.expm1(-dt0)),
        'ssd_a_log': jnp.log(unif((L, SSD_HEADS), 1.0, 16.0)),
        'ssd_d': 1.0 + nrm((L, SSD_HEADS), 0.1),
        'ssd_norm_w': 1.0 + nrm((L, gw), 0.02),
        's5_lam_re': -0.5 + nrm((L, S5_GROUPS, S5_STATE), 0.01),
        's5_lam_im': math.pi * jnp.arange(S5_STATE, dtype=f32) + nrm((L, S5_GROUPS, S5_STATE), 0.01),
        's5_log_step': unif((L, S5_GROUPS), math.log(1e-3), math.log(1e-1)),
        's5_b_re': nrm((L, S5_GROUPS, S5_STATE, S5_GROUP_CH), (2 * S5_GROUP_CH) ** -0.5),
        's5_b_im': nrm((L, S5_GROUPS, S5_STATE, S5_GROUP_CH), (2 * S5_GROUP_CH) ** -0.5),
        's5_c_re': nrm((L, S5_GROUPS, S5_GROUP_CH, S5_STATE), (2 * S5_STATE) ** -0.5),
        's5_c_im': nrm((L, S5_GROUPS, S5_GROUP_CH, S5_STATE), (2 * S5_STATE) ** -0.5),
        's5_d': nrm((L, gw), 1.0),
        's5_glu_w': nrm((L, gw, gw), gw ** -0.5),
        's5_glu_b': nrm((L, gw), 0.02),
        'ffn_w_gate': nrm((N_DENSE, D_MODEL, D_FF), D_MODEL ** -0.5),
        'ffn_w_up': nrm((N_DENSE, D_MODEL, D_FF), D_MODEL ** -0.5),
        'ffn_w_down': nrm((N_DENSE, D_FF, D_MODEL), D_FF ** -0.5),
        'moe_router_w': nrm((N_MOE, D_MODEL, N_EXPERTS), D_MODEL ** -0.5),
        'moe_router_b': nrm((N_MOE, N_EXPERTS), 0.01),
        'moe_w_gate': nrm((N_MOE, N_EXPERTS, D_MODEL, D_EXPERT), D_MODEL ** -0.5),
        'moe_w_up': nrm((N_MOE, N_EXPERTS, D_MODEL, D_EXPERT), D_MODEL ** -0.5),
        'moe_w_down': nrm((N_MOE, N_EXPERTS, D_EXPERT, D_MODEL), D_EXPERT ** -0.5),
    }


def reference(x, c, ada_w, ada_b, norm_pre, norm_post, w_in, w_out, attn_sink,
              rwkv_mu, rwkv_w0, rwkv_w_up, rwkv_a0, rwkv_a_up, rwkv_g_up, rwkv_k_k, rwkv_k_a,
              rwkv_r_k, rwkv_ln_w, rwkv_ln_b,
              ssd_conv_w, ssd_conv_b, ssd_dt_bias, ssd_a_log, ssd_d, ssd_norm_w,
              s5_lam_re, s5_lam_im, s5_log_step, s5_b_re, s5_b_im, s5_c_re, s5_c_im, s5_d,
              s5_glu_w, s5_glu_b,
              ffn_w_gate, ffn_w_up, ffn_w_down,
              moe_router_w, moe_router_b, moe_w_gate, moe_w_up, moe_w_down):
    bsz, seq, d = x.shape
    c_act = jax.nn.silu(c)
    cuts = [ATTN_IN, ATTN_IN + RWKV_IN, ATTN_IN + RWKV_IN + SSD_IN]
    for i in range(DEPTH):
        shift, scale, gate = ada_modulation(c_act, ada_w[i, 0], ada_b[i, 0])
        h = rms_norm(x, norm_pre[i, 0]) * (1 + scale) + shift
        p = h @ w_in[i]
        p_attn, p_rwkv, p_ssd, p_s5 = jnp.split(p, cuts, axis=-1)
        q, k, v = jnp.split(p_attn, [ATTN_Q_WIDTH, ATTN_Q_WIDTH + ATTN_KV_WIDTH], axis=-1)
        y_attn = sliding_window_attention(
            q.reshape(bsz, seq, ATTN_Q_HEADS, ATTN_HEAD_DIM),
            k.reshape(bsz, seq, ATTN_KV_HEADS, ATTN_HEAD_DIM),
            v.reshape(bsz, seq, ATTN_KV_HEADS, ATTN_HEAD_DIM), attn_sink[i])
        y_rwkv = rwkv7_time_mix(p_rwkv, rwkv_mu[i], rwkv_w0[i], rwkv_w_up[i], rwkv_a0[i],
                                rwkv_a_up[i], rwkv_g_up[i], rwkv_k_k[i], rwkv_k_a[i],
                                rwkv_r_k[i], rwkv_ln_w[i], rwkv_ln_b[i])
        y_ssd = mamba2_ssd(p_ssd, ssd_conv_w[i], ssd_conv_b[i], ssd_dt_bias[i], ssd_a_log[i],
                           ssd_d[i], ssd_norm_w[i])
        y_s5 = s5_mix(p_s5, s5_lam_re[i], s5_lam_im[i], s5_log_step[i], s5_b_re[i], s5_b_im[i],
                      s5_c_re[i], s5_c_im[i], s5_d[i], s5_glu_w[i], s5_glu_b[i])
        y = jnp.concatenate([y_attn, y_rwkv, y_ssd, y_s5], axis=-1).astype(x.dtype) @ w_out[i]
        x = x + gate * rms_norm(y, norm_post[i, 0])
        shift, scale, gate = ada_modulation(c_act, ada_w[i, 1], ada_b[i, 1])
        h = (rms_norm(x, norm_pre[i, 1]) * (1 + scale) + shift).reshape(bsz * seq, d)
        j = i // 2
        if i % 2 == 0:
            y = swiglu(h, ffn_w_gate[j], ffn_w_up[j], ffn_w_down[j])
        else:
            y = routed_swiglu(h, moe_router_w[j], moe_router_b[j], moe_w_gate[j],
                              moe_w_up[j], moe_w_down[j])
        x = x + gate * rms_norm(y.reshape(bsz, seq, d).astype(x.dtype), norm_post[i, 1])
    return x
```

```python
import functools
import math

import jax
import jax.numpy as jnp
from jax import lax
from jax.experimental import pallas as pl
from jax.experimental.pallas import tpu as pltpu

F32 = jnp.float32
BF16 = jnp.bfloat16
HIGHEST = lax.Precision.HIGHEST

D_MODEL = 2048
DEPTH = 4
GW = 512
NORM_EPS = 1e-6
HEAD = 64
N_HEADS = 8

ATTN_KV_HEADS = 2
ATTN_GROUP = 4
ATTN_BLOCK = 128
ATTN_IN = 768

RWKV_IN = 1792
RWKV_LN_EPS = 64e-5
RWKV_CHUNK = 64

SSD_STATE = 128
SSD_CONV = 4
SSD_CHUNK = 128
SSD_CONV_CH = 1024
SSD_IN = 1544
SSD_NORM_EPS = 1e-5

S5_GROUPS = 32
S5_GROUP_CH = 16
S5_STATE = 64
S5_LANES = S5_GROUPS * S5_STATE
S5_CHUNK = 256
S5_TILE = 8

D_FF = 5632
N_EXPERTS = 8
TOP_K = 2
D_EXPERT = 2816
MOE_BLOCK = 512

SEC_R = 0
SEC_S = 1792
SEC_F = 3584
SEC_A = 4608
P_WIDTH = 5376
SEC_W = 1792

VMEM_LIMIT = 52 * 1024 * 1024


def _params(sem):
    return pltpu.CompilerParams(dimension_semantics=sem, vmem_limit_bytes=VMEM_LIMIT)


def _nt(a, b, **kw):
    return lax.dot_general(a, b, (((1,), (1,)), ((), ())), preferred_element_type=F32, **kw)


def _tn(a, b, **kw):
    return lax.dot_general(a, b, (((0,), (0,)), ((), ())), preferred_element_type=F32, **kw)


def _mm(a, b, **kw):
    return jnp.dot(a, b, preferred_element_type=F32, **kw)


def _softplus(x):
    return jnp.maximum(x, 0.0) + jnp.log1p(jnp.exp(-jnp.abs(x)))


def _silu(x):
    return x * jax.nn.sigmoid(x)


def _rms(x, eps):
    return x * lax.rsqrt(jnp.mean(x * x, axis=-1, keepdims=True) + eps)


def _ada_kernel(c_ref, w_ref, b_ref, o_ref):
    c = c_ref[...]
    o_ref[0] = _mm(_silu(c).astype(BF16), w_ref[0].astype(BF16)) + b_ref[0]


def _ada_modulation(c, ada_w, ada_b):
    bsz = c.shape[0]
    rows = 8 * pl.cdiv(bsz, 8)
    c_pad = jnp.zeros((rows, D_MODEL), F32).at[:bsz].set(c)
    n_mod = ada_w.shape[0] * 2
    w = ada_w.reshape(n_mod, D_MODEL, 3 * D_MODEL)
    b = ada_b.reshape(n_mod, 1, 3 * D_MODEL)
    tn = 768
    out = pl.pallas_call(
        _ada_kernel,
        grid=(n_mod, 3 * D_MODEL // tn),
        in_specs=[
            pl.BlockSpec((rows, D_MODEL), lambda m, j: (0, 0)),
            pl.BlockSpec((1, D_MODEL, tn), lambda m, j: (m, 0, j)),
            pl.BlockSpec((1, 1, tn), lambda m, j: (m, 0, j)),
        ],
        out_specs=pl.BlockSpec((1, rows, tn), lambda m, j: (m, 0, j)),
        out_shape=jax.ShapeDtypeStruct((n_mod, rows, 3 * D_MODEL), F32),
        compiler_params=_params(("parallel", "parallel")),
        name="ada_modulation",
    )(c_pad, w, b)
    return out[:, :bsz]


def _prenorm(x, nw, scale, shift):
    return (_rms(x, NORM_EPS) * nw) * (1.0 + scale) + shift


def _inproj_kernel(x_ref, nw_ref, sc_ref, sh_ref, w_ref, o_ref, h_ref):
    @pl.when(pl.program_id(1) == 0)
    def _():
        h_ref[...] = _prenorm(x_ref[...], nw_ref[...], sc_ref[0], sh_ref[0]).astype(BF16)

    o_ref[...] = _mm(h_ref[...], w_ref[...])


def _in_projection(x2, nw, scale, shift, w, seq):
    t = x2.shape[0]
    tm, tn = 1024, 768
    per_b = seq // tm
    return pl.pallas_call(
        _inproj_kernel,
        grid=(t // tm, P_WIDTH // tn),
        in_specs=[
            pl.BlockSpec((tm, D_MODEL), lambda i, j: (i, 0)),
            pl.BlockSpec((1, D_MODEL), lambda i, j: (0, 0)),
            pl.BlockSpec((1, 1, D_MODEL), lambda i, j: (i // per_b, 0, 0)),
            pl.BlockSpec((1, 1, D_MODEL), lambda i, j: (i // per_b, 0, 0)),
            pl.BlockSpec((D_MODEL, tn), lambda i, j: (0, j)),
        ],
        out_specs=pl.BlockSpec((tm, tn), lambda i, j: (i, j)),
        out_shape=jax.ShapeDtypeStruct((t, P_WIDTH), F32),
        scratch_shapes=[pltpu.VMEM((tm, D_MODEL), BF16)],
        compiler_params=_params(("parallel", "arbitrary")),
        name="in_projection",
    )(x2, nw, scale, shift, w)


def _alibi_slope(h):
    return 2.0 ** (-8.0 * (h + 1) / N_HEADS)


def _attn_kernel(sink_ref, cur_ref, prev_ref, o_ref):
    n = pl.program_id(1)
    blk = ATTN_BLOCK
    cur = cur_ref[...]
    prev = prev_ref[...]
    q = cur[:, :GW]
    k2 = jnp.concatenate([prev[:, :128], cur[:, GW:GW + 128]], axis=0).astype(BF16)
    v2 = jnp.concatenate([prev[:, 128:256], cur[:, GW + 128:GW + 256]], axis=0).astype(BF16)
    qi = lax.broadcasted_iota(jnp.int32, (blk, 2 * blk), 0) + blk
    kj = lax.broadcasted_iota(jnp.int32, (blk, 2 * blk), 1)
    dist = qi - kj
    valid = (dist >= 0) & (dist < blk) & ((kj >= blk) | (n > 0))
    distf = dist.astype(F32)
    outs = []
    for h in range(N_HEADS):
        g = h // ATTN_GROUP
        qh = q[:, HEAD * h:HEAD * (h + 1)].astype(BF16)
        kg = k2[:, HEAD * g:HEAD * (g + 1)]
        vg = v2[:, HEAD * g:HEAD * (g + 1)]
        s = _nt(qh, kg) * (HEAD ** -0.5) - _alibi_slope(h) * distf
        s = jnp.where(valid, s, -jnp.inf)
        sink = sink_ref[0, h]
        m = jnp.maximum(jnp.max(s, axis=-1, keepdims=True), sink)
        p = jnp.exp(s - m)
        denom = jnp.sum(p, axis=-1, keepdims=True) + jnp.exp(sink - m)
        outs.append(_mm(p.astype(BF16), vg) / denom)
    o_ref[...] = jnp.concatenate(outs, axis=-1).astype(o_ref.dtype)


def _attention(p, sinks, bsz, seq):
    nb = seq // ATTN_BLOCK
    cur_col = SEC_A // ATTN_IN
    kv_col = (SEC_A + GW) // 256
    return pl.pallas_call(
        _attn_kernel,
        grid=(bsz, nb),
        in_specs=[
            pl.BlockSpec(memory_space=pltpu.SMEM),
            pl.BlockSpec((ATTN_BLOCK, ATTN_IN), lambda b, n: (b * nb + n, cur_col)),
            pl.BlockSpec((ATTN_BLOCK, 256), lambda b, n: (b * nb + jnp.maximum(n - 1, 0), kv_col)),
        ],
        out_specs=pl.BlockSpec((ATTN_BLOCK, GW), lambda b, n: (b * nb + n, 0)),
        out_shape=jax.ShapeDtypeStruct((bsz * seq, GW), BF16),
        compiler_params=_params(("parallel", "arbitrary")),
        name="swa_attention",
    )(sinks, p, p)


def _rwkv_kernel(p_ref, mu_ref, w0_ref, wup_ref, a0_ref, aup_ref, gup_ref, kk_ref, ka_ref,
                 rk_ref, lnw_ref, lnb_ref, hsum_ref, o_ref, last_ref, state_ref):
    c = RWKV_CHUNK

    @pl.when(pl.program_id(1) == 0)
    def _():
        last_ref[...] = jnp.zeros_like(last_ref)
        state_ref[...] = jnp.zeros_like(state_ref)

    p = p_ref[...]
    row = lax.broadcasted_iota(jnp.int32, (c, 1), 0)
    shifted = jnp.where(row == 0, last_ref[0:1, :], pltpu.roll(p, 1, axis=0))
    last_ref[0:1, :] = p[c - 1:c, :]
    pm = p + mu_ref[...] * (shifted - p)

    r = pm[:, 0:GW]
    k = pm[:, GW:2 * GW]
    v = pm[:, 2 * GW:3 * GW]
    wa = pm[:, 3 * GW:3 * GW + 128]
    gd = pm[:, 3 * GW + 128:3 * GW + 256]

    w = -_softplus(-(w0_ref[...] + _mm(jnp.tanh(wa).astype(BF16), wup_ref[...]))) - 0.5
    logd = -jnp.exp(w)
    a = jax.nn.sigmoid(a0_ref[...] + _mm(wa.astype(BF16), aup_ref[...]))
    g = _mm(jax.nn.sigmoid(gd).astype(BF16), gup_ref[...])

    kk = k * kk_ref[...]
    sumsq = _mm(kk * kk, hsum_ref[...], precision=HIGHEST)
    kk = kk / jnp.maximum(jnp.sqrt(sumsq), 1e-12)
    k = k * (1.0 + (a - 1.0) * ka_ref[...])
    b = kk * a

    ri = lax.broadcasted_iota(jnp.int32, (c, c), 0)
    ci = lax.broadcasted_iota(jnp.int32, (c, c), 1)
    lower = ri > ci
    lower_eq = ri >= ci
    eye = (ri == ci).astype(F32)
    cum = _mm(lower_eq.astype(F32), logd, precision=HIGHEST)
    g_inv = jnp.exp(-cum)
    at = (-kk) * jnp.exp(cum - logd)
    bt = (b * g_inv).astype(BF16)
    kt = (k * g_inv).astype(BF16)
    rt = (r * jnp.exp(cum)).astype(BF16)
    g_end = jnp.exp(cum[c - 1:c, :])
    at16 = at.astype(BF16)
    v16 = v.astype(BF16)
    rk = r * k * rk_ref[...]

    outs = []
    for h in range(N_HEADS):
        sl = slice(HEAD * h, HEAD * (h + 1))
        a_h, b_h, k_h, r_h, v_h = at16[:, sl], bt[:, sl], kt[:, sl], rt[:, sl], v16[:, sl]
        s0 = state_ref[h]
        s016 = s0.astype(BF16)
        l_ab = jnp.where(lower, _nt(a_h, b_h), 0.0)
        l_ak = jnp.where(lower, _nt(a_h, k_h), 0.0)
        pw = l_ab
        inv = eye + l_ab
        for _ in range(5):
            pw = _mm(pw, pw, precision=HIGHEST)
            inv = inv + _mm(inv, pw, precision=HIGHEST)
        x = _nt(a_h, s016) + _mm(l_ak.astype(BF16), v_h)
        u = _mm(inv, x, precision=HIGHEST)
        u16 = u.astype(BF16)
        m_rb = jnp.where(lower_eq, _nt(r_h, b_h), 0.0).astype(BF16)
        m_rk = jnp.where(lower_eq, _nt(r_h, k_h), 0.0).astype(BF16)
        y = _nt(r_h, s016) + _mm(m_rb, u16) + _mm(m_rk, v_h)
        state_ref[h] = (s0 + _tn(u16, b_h) + _tn(v_h, k_h)) * g_end[:, sl]
        mean = jnp.mean(y, axis=-1, keepdims=True)
        yc = y - mean
        var = jnp.mean(yc * yc, axis=-1, keepdims=True)
        bonus = jnp.sum(rk[:, sl], axis=-1, keepdims=True) * v[:, sl]
        outs.append(yc * lax.rsqrt(var + RWKV_LN_EPS) * lnw_ref[:, sl] + lnb_ref[:, sl] + bonus)
    o_ref[...] = (jnp.concatenate(outs, axis=-1) * g).astype(o_ref.dtype)


def _rwkv(p, prm, bsz, seq):
    c = RWKV_CHUNK
    nc = seq // c
    col = SEC_R // SEC_W
    vec = lambda width: pl.BlockSpec((1, width), lambda b, n: (0, 0))
    mat = lambda rows: pl.BlockSpec((rows, GW), lambda b, n: (0, 0))
    return pl.pallas_call(
        _rwkv_kernel,
        grid=(bsz, nc),
        in_specs=[
            pl.BlockSpec((c, RWKV_IN), lambda b, n: (b * nc + n, col)),
            vec(RWKV_IN), vec(GW), mat(128), vec(GW), mat(128), mat(128),
            vec(GW), vec(GW), vec(GW), vec(GW), vec(GW),
            pl.BlockSpec((GW, GW), lambda b, n: (0, 0)),
        ],
        out_specs=pl.BlockSpec((c, GW), lambda b, n: (b * nc + n, 0)),
        out_shape=jax.ShapeDtypeStruct((bsz * seq, GW), BF16),
        scratch_shapes=[pltpu.VMEM((8, RWKV_IN), F32), pltpu.VMEM((N_HEADS, HEAD, HEAD), F32)],
        compiler_params=_params(("parallel", "arbitrary")),
        name="rwkv7_chunked",
    )(p, *prm)


def _rwkv_params(mu, w0, w_up, a0, a_up, g_up, k_k, k_a, r_k, ln_w, ln_b):
    zeros = jnp.zeros((64, GW), F32)
    wup = jnp.concatenate([w_up, zeros], axis=0).astype(BF16)
    aup = jnp.concatenate([zeros, a_up], axis=0).astype(BF16)
    head = jnp.arange(GW) // HEAD
    hsum = (head[:, None] == head[None, :]).astype(F32)
    row = lambda t: t.reshape(1, -1)
    return (row(mu), row(w0), wup, row(a0), aup, g_up.astype(BF16), row(k_k), row(k_a),
            row(r_k), row(ln_w), row(ln_b), hsum)


def _ssd_kernel(p_ref, cw_ref, cb_ref, dtb_ref, a_ref, dsk_ref, nw_ref, o_ref, tail_ref, state_ref):
    c = SSD_CHUNK

    @pl.when(pl.program_id(1) == 0)
    def _():
        tail_ref[...] = jnp.zeros_like(tail_ref)
        state_ref[...] = jnp.zeros_like(state_ref)

    blk = p_ref[...]
    z = blk[:, :GW]
    raw = blk[:, GW:GW + SSD_CONV_CH]
    ext = jnp.concatenate([tail_ref[...], raw], axis=0)
    tail_ref[...] = raw[c - 8:c, :]
    conv = cb_ref[...]
    for i in range(SSD_CONV):
        off = 8 - (SSD_CONV - 1) + i
        conv = conv + cw_ref[i:i + 1, :] * ext[off:off + c, :]
    xbc = _silu(conv)
    x = xbc[:, :GW]
    bm = xbc[:, GW:GW + 256].astype(BF16)
    cm = xbc[:, GW + 256:GW + 512].astype(BF16)

    dt = _softplus(blk[:, GW + SSD_CONV_CH:GW + SSD_CONV_CH + 128] + dtb_ref[...])
    da = dt * a_ref[...]
    ri = lax.broadcasted_iota(jnp.int32, (c, c), 0)
    ci = lax.broadcasted_iota(jnp.int32, (c, c), 1)
    lower_eq = ri >= ci
    cs = _mm(lower_eq.astype(F32), da, precision=HIGHEST)
    cs_t = cs.T
    cs_end = cs[c - 1:c, :]
    dec_in = jnp.exp(cs)
    dec_out = jnp.exp(cs_end - cs)
    dec_all = jnp.exp(cs_end)

    outs = []
    cb = None
    for h in range(N_HEADS):
        gidx = h // 4
        sl = slice(HEAD * h, HEAD * (h + 1))
        bg = bm[:, SSD_STATE * gidx:SSD_STATE * (gidx + 1)]
        cg = cm[:, SSD_STATE * gidx:SSD_STATE * (gidx + 1)]
        if h % 4 == 0:
            cb = _nt(cg, bg)
        lmat = jnp.exp(jnp.where(lower_eq, cs[:, h:h + 1] - cs_t[h:h + 1, :], -jnp.inf))
        x_h = x[:, sl]
        xdt = x_h * dt[:, h:h + 1]
        s0 = state_ref[h]
        y = _mm((cb * lmat).astype(BF16), xdt.astype(BF16))
        y = y + _nt(cg, s0.astype(BF16)) * dec_in[:, h:h + 1]
        new = _tn((xdt * dec_out[:, h:h + 1]).astype(BF16), bg)
        state_ref[h] = s0 * dec_all[:, h:h + 1] + new
        outs.append(y + x_h * dsk_ref[:, sl])
    y = jnp.concatenate(outs, axis=-1) * _silu(z)
    half = GW // 2
    y = jnp.concatenate([_rms(y[:, :half], SSD_NORM_EPS), _rms(y[:, half:], SSD_NORM_EPS)], axis=-1)
    o_ref[...] = (y * nw_ref[...]).astype(o_ref.dtype)


def _ssd(p, prm, bsz, seq):
    c = SSD_CHUNK
    nc = seq // c
    col = SEC_S // SEC_W
    vec = lambda width: pl.BlockSpec((1, width), lambda b, n: (0, 0))
    return pl.pallas_call(
        _ssd_kernel,
        grid=(bsz, nc),
        in_specs=[
            pl.BlockSpec((c, SEC_W), lambda b, n: (b * nc + n, col)),
            pl.BlockSpec((SSD_CONV, SSD_CONV_CH), lambda b, n: (0, 0)),
            vec(SSD_CONV_CH), vec(128), vec(128), vec(GW), vec(GW),
        ],
        out_specs=pl.BlockSpec((c, GW), lambda b, n: (b * nc + n, 0)),
        out_shape=jax.ShapeDtypeStruct((bsz * seq, GW), BF16),
        scratch_shapes=[pltpu.VMEM((8, SSD_CONV_CH), F32), pltpu.VMEM((N_HEADS, HEAD, SSD_STATE), F32)],
        compiler_params=_params(("parallel", "arbitrary")),
        name="mamba2_ssd",
    )(p, *prm)


def _ssd_params(conv_w, conv_b, dt_bias, a_log, d_skip, norm_w):
    pad = lambda t: jnp.zeros((1, 128), F32).at[0, :N_HEADS].set(t)
    return (conv_w, conv_b.reshape(1, -1), pad(dt_bias), pad(-jnp.exp(a_log)),
            jnp.repeat(d_skip, HEAD).reshape(1, -1), norm_w.reshape(1, -1))


def _gelu_tanh(x):
    return 0.5 * x * (1.0 + jnp.tanh(math.sqrt(2.0 / math.pi) * (x + 0.044715 * (x * x * x))))


def _s5_kernel(u_ref, bbr_ref, bbi_ref, lvr_ref, lvi_ref, pwr_ref, pwi_ref, ccr_ref, cci_ref,
               dsk_ref, gw_ref, gb_ref, o_ref, hr_ref, hi_ref, carry_ref):
    @pl.when(pl.program_id(1) == 0)
    def _():
        carry_ref[...] = jnp.zeros_like(carry_ref)

    u = u_ref[...]
    u16 = u.astype(BF16)
    hr_ref[...] = _mm(u16, bbr_ref[...])
    hi_ref[...] = _mm(u16, bbi_ref[...])

    def tile(t, carry):
        cr, ci = carry
        rows = pl.ds(pl.multiple_of(t * S5_TILE, S5_TILE), S5_TILE)
        xr = hr_ref[rows, :]
        xi = hi_ref[rows, :]
        for lvl, s in enumerate((1, 2, 4)):
            lr = lvr_ref[lvl]
            li = lvi_ref[lvl]
            sr = pltpu.roll(xr, s, axis=0)
            si = pltpu.roll(xi, s, axis=0)
            xr, xi = xr + lr * sr - li * si, xi + lr * si + li * sr
        pr = pwr_ref[...]
        pi = pwi_ref[...]
        xr, xi = xr + pr * cr - pi * ci, xi + pr * ci + pi * cr
        hr_ref[rows, :] = xr
        hi_ref[rows, :] = xi
        last = S5_TILE - 1
        return (jnp.broadcast_to(xr[last:last + 1, :], xr.shape), jnp.broadcast_to(xi[last:last + 1, :], xi.shape))

    cr, ci = lax.fori_loop(0, u.shape[0] // S5_TILE, tile, (carry_ref[0], carry_ref[1]))
    carry_ref[0] = cr
    carry_ref[1] = ci

    y = _mm(hr_ref[...].astype(BF16), ccr_ref[...]) - _mm(hi_ref[...].astype(BF16), cci_ref[...])
    y = _gelu_tanh(y + dsk_ref[...] * u)
    gate = jax.nn.sigmoid(_mm(y.astype(BF16), gw_ref[...]) + gb_ref[...])
    o_ref[...] = (y * gate).astype(o_ref.dtype)


def _s5(p, prm, bsz, seq):
    c = S5_CHUNK
    nc = seq // c
    col = SEC_F // GW
    full = lambda shape: pl.BlockSpec(shape, lambda b, n: (0,) * len(shape))
    return pl.pallas_call(
        _s5_kernel,
        grid=(bsz, nc),
        in_specs=[
            pl.BlockSpec((c, GW), lambda b, n: (b * nc + n, col)),
            full((GW, S5_LANES)), full((GW, S5_LANES)),
            full((3, S5_TILE, S5_LANES)), full((3, S5_TILE, S5_LANES)),
            full((S5_TILE, S5_LANES)), full((S5_TILE, S5_LANES)),
            full((S5_LANES, GW)), full((S5_LANES, GW)),
            full((1, GW)), full((GW, GW)), full((1, GW)),
        ],
        out_specs=pl.BlockSpec((c, GW), lambda b, n: (b * nc + n, 0)),
        out_shape=jax.ShapeDtypeStruct((bsz * seq, GW), BF16),
        scratch_shapes=[pltpu.VMEM((c, S5_LANES), F32), pltpu.VMEM((c, S5_LANES), F32),
                        pltpu.VMEM((2, S5_TILE, S5_LANES), F32)],
        compiler_params=_params(("parallel", "arbitrary")),
        name="s5_scan",
    )(p, *prm)


def _s5_params(lam_re, lam_im, log_step, b_re, b_im, c_re, c_im, d_skip, glu_w, glu_b):
    step = jnp.exp(log_step)[:, None]
    mag = jnp.exp(lam_re * step)
    ab_re, ab_im = mag * jnp.cos(lam_im * step), mag * jnp.sin(lam_im * step)
    den = lam_re * lam_re + lam_im * lam_im
    coef_re = ((ab_re - 1) * lam_re + ab_im * lam_im) / den
    coef_im = (ab_im * lam_re - (ab_re - 1) * lam_im) / den
    bb_re = coef_re[..., None] * b_re - coef_im[..., None] * b_im
    bb_im = coef_re[..., None] * b_im + coef_im[..., None] * b_re
    eye = jnp.eye(S5_GROUPS, dtype=F32)
    in_proj = lambda t: jnp.einsum('gni,gh->gihn', t, eye).reshape(GW, S5_LANES).astype(BF16)
    out_proj = lambda t: jnp.einsum('gin,gh->gnhi', t, eye).reshape(S5_LANES, GW).astype(BF16)

    def cmul(a, b):
        return a[0] * b[0] - a[1] * b[1], a[0] * b[1] + a[1] * b[0]

    lam1 = (ab_re.reshape(-1), ab_im.reshape(-1))
    lam2 = cmul(lam1, lam1)
    lam4 = cmul(lam2, lam2)
    rows = jnp.arange(S5_TILE)[:, None]
    lvl_re = jnp.stack([jnp.where(rows >= s, l[0][None, :], 0.0) for s, l in ((1, lam1), (2, lam2), (4, lam4))])
    lvl_im = jnp.stack([jnp.where(rows >= s, l[1][None, :], 0.0) for s, l in ((1, lam1), (2, lam2), (4, lam4))])
    powers = [lam1]
    for _ in range(S5_TILE - 1):
        powers.append(cmul(powers[-1], lam1))
    pw_re = jnp.stack([q[0] for q in powers])
    pw_im = jnp.stack([q[1] for q in powers])
    return (in_proj(bb_re), in_proj(bb_im), lvl_re, lvl_im, pw_re, pw_im, out_proj(c_re), out_proj(c_im),
            d_skip.reshape(1, -1), glu_w.astype(BF16), glu_b.reshape(1, -1))


def _outproj_kernel(ya_ref, yr_ref, ys_ref, yf_ref, w_ref, x_ref, g_ref, nw_ref, o_ref):
    acc = _mm(ya_ref[...], w_ref[0:GW, :])
    acc = acc + _mm(yr_ref[...], w_ref[GW:2 * GW, :])
    acc = acc + _mm(ys_ref[...], w_ref[2 * GW:3 * GW, :])
    acc = acc + _mm(yf_ref[...], w_ref[3 * GW:4 * GW, :])
    o_ref[...] = x_ref[...] + g_ref[0] * (_rms(acc, NORM_EPS) * nw_ref[...])


def _out_projection(ys, w, x2, gate, nw, seq):
    t = x2.shape[0]
    tm = 512
    per_b = seq // tm
    ymap = pl.BlockSpec((tm, GW), lambda i: (i, 0))
    return pl.pallas_call(
        _outproj_kernel,
        grid=(t // tm,),
        in_specs=[
            ymap, ymap, ymap, ymap,
            pl.BlockSpec((D_MODEL, D_MODEL), lambda i: (0, 0)),
            pl.BlockSpec((tm, D_MODEL), lambda i: (i, 0)),
            pl.BlockSpec((1, 1, D_MODEL), lambda i: (i // per_b, 0, 0)),
            pl.BlockSpec((1, D_MODEL), lambda i: (0, 0)),
        ],
        out_specs=pl.BlockSpec((tm, D_MODEL), lambda i: (i, 0)),
        out_shape=jax.ShapeDtypeStruct((t, D_MODEL), F32),
        compiler_params=_params(("parallel",)),
        name="out_projection",
    )(*ys, w, x2, gate, nw)


def _ffn_kernel(x_ref, npre_ref, sc_ref, sh_ref, wg_ref, wu_ref, wd_ref, g_ref, npost_ref, o_ref,
                h_ref, acc_ref):
    f = pl.program_id(1)

    @pl.when(f == 0)
    def _():
        h_ref[...] = _prenorm(x_ref[...], npre_ref[...], sc_ref[0], sh_ref[0]).astype(BF16)
        acc_ref[...] = jnp.zeros_like(acc_ref)

    h = h_ref[...]
    act = _silu(_mm(h, wg_ref[...])) * _mm(h, wu_ref[...])
    acc_ref[...] += _mm(act.astype(BF16), wd_ref[...])

    @pl.when(f == pl.num_programs(1) - 1)
    def _():
        o_ref[...] = x_ref[...] + g_ref[0] * (_rms(acc_ref[...], NORM_EPS) * npost_ref[...])


def _dense_ffn(x2, npre, scale, shift, wg, wu, wd, gate, npost, seq):
    t = x2.shape[0]
    tm, tf = 512, 512
    per_b = seq // tm
    mod = pl.BlockSpec((1, 1, D_MODEL), lambda i, f: (i // per_b, 0, 0))
    vec = pl.BlockSpec((1, D_MODEL), lambda i, f: (0, 0))
    return pl.pallas_call(
        _ffn_kernel,
        grid=(t // tm, D_FF // tf),
        in_specs=[
            pl.BlockSpec((tm, D_MODEL), lambda i, f: (i, 0)),
            vec, mod, mod,
            pl.BlockSpec((D_MODEL, tf), lambda i, f: (0, f)),
            pl.BlockSpec((D_MODEL, tf), lambda i, f: (0, f)),
            pl.BlockSpec((tf, D_MODEL), lambda i, f: (f, 0)),
            mod, vec,
        ],
        out_specs=pl.BlockSpec((tm, D_MODEL), lambda i, f: (i, 0)),
        out_shape=jax.ShapeDtypeStruct((t, D_MODEL), F32),
        scratch_shapes=[pltpu.VMEM((tm, D_MODEL), BF16), pltpu.VMEM((tm, D_MODEL), F32)],
        compiler_params=_params(("parallel", "arbitrary")),
        name="dense_swiglu",
    )(x2, npre, scale, shift, wg, wu, wd, gate, npost)


def _router_kernel(x_ref, npre_ref, sc_ref, sh_ref, rw_ref, rb_ref, h_ref, lg_ref):
    h = _prenorm(x_ref[...], npre_ref[...], sc_ref[0], sh_ref[0])
    h_ref[...] = h.astype(BF16)
    lg_ref[...] = _mm(h, rw_ref[...], precision=HIGHEST) + rb_ref[...]


def _router(x2, npre, scale, shift, rw, rb, seq):
    t = x2.shape[0]
    tm = 512
    per_b = seq // tm
    mod = pl.BlockSpec((1, 1, D_MODEL), lambda i: (i // per_b, 0, 0))
    return pl.pallas_call(
        _router_kernel,
        grid=(t // tm,),
        in_specs=[
            pl.BlockSpec((tm, D_MODEL), lambda i: (i, 0)),
            pl.BlockSpec((1, D_MODEL), lambda i: (0, 0)),
            mod, mod,
            pl.BlockSpec((D_MODEL, 128), lambda i: (0, 0)),
            pl.BlockSpec((1, 128), lambda i: (0, 0)),
        ],
        out_specs=[pl.BlockSpec((tm, D_MODEL), lambda i: (i, 0)), pl.BlockSpec((tm, 128), lambda i: (i, 0))],
        out_shape=[jax.ShapeDtypeStruct((t, D_MODEL), BF16), jax.ShapeDtypeStruct((t, 128), F32)],
        compiler_params=_params(("parallel",)),
        name="moe_router",
    )(x2, npre, scale, shift, rw, rb)


def _moe_ffn_kernel(be_ref, xs_ref, wg_ref, wu_ref, wd_ref, o_ref, acc_ref):
    f = pl.program_id(1)

    @pl.when(f == 0)
    def _():
        acc_ref[...] = jnp.zeros_like(acc_ref)

    xs = xs_ref[...]
    act = _silu(_mm(xs, wg_ref[0])) * _mm(xs, wu_ref[0])
    acc_ref[...] += _mm(act.astype(BF16), wd_ref[0])

    @pl.when(f == pl.num_programs(1) - 1)
    def _():
        o_ref[...] = acc_ref[...]


def _moe_ffn(block_e, xs, wg, wu, wd):
    rows = xs.shape[0]
    tf = 256
    grid_spec = pltpu.PrefetchScalarGridSpec(
        num_scalar_prefetch=1,
        grid=(rows // MOE_BLOCK, D_EXPERT // tf),
        in_specs=[
            pl.BlockSpec((MOE_BLOCK, D_MODEL), lambda i, f, be: (i, 0)),
            pl.BlockSpec((1, D_MODEL, tf), lambda i, f, be: (be[i], 0, f)),
            pl.BlockSpec((1, D_MODEL, tf), lambda i, f, be: (be[i], 0, f)),
            pl.BlockSpec((1, tf, D_MODEL), lambda i, f, be: (be[i], f, 0)),
        ],
        out_specs=pl.BlockSpec((MOE_BLOCK, D_MODEL), lambda i, f, be: (i, 0)),
        scratch_shapes=[pltpu.VMEM((MOE_BLOCK, D_MODEL), F32)],
    )
    return pl.pallas_call(
        _moe_ffn_kernel,
        grid_spec=grid_spec,
        out_shape=jax.ShapeDtypeStruct((rows, D_MODEL), F32),
        compiler_params=_params(("parallel", "arbitrary")),
        name="moe_swiglu",
    )(block_e, xs, wg, wu, wd)


def _combine_kernel(y0_ref, y1_ref, p_ref, x_ref, g_ref, npost_ref, o_ref):
    p = p_ref[...]
    y = y0_ref[...] * p[:, 0:1] + y1_ref[...] * p[:, 1:2]
    o_ref[...] = x_ref[...] + g_ref[0] * (_rms(y, NORM_EPS) * npost_ref[...])


def _moe_combine(y0, y1, top_p, x2, gate, npost, seq):
    t = x2.shape[0]
    tm = 512
    per_b = seq // tm
    row = pl.BlockSpec((tm, D_MODEL), lambda i: (i, 0))
    return pl.pallas_call(
        _combine_kernel,
        grid=(t // tm,),
        in_specs=[row, row, pl.BlockSpec((tm, 128), lambda i: (i, 0)), row,
                  pl.BlockSpec((1, 1, D_MODEL), lambda i: (i // per_b, 0, 0)),
                  pl.BlockSpec((1, D_MODEL), lambda i: (0, 0))],
        out_specs=row,
        out_shape=jax.ShapeDtypeStruct((t, D_MODEL), F32),
        compiler_params=_params(("parallel",)),
        name="moe_combine",
    )(y0, y1, top_p, x2, gate, npost)


def _routed_ffn(x2, npre, scale, shift, rw, rb, wg, wu, wd, gate, npost, seq):
    t = x2.shape[0]
    rw_pad = jnp.zeros((D_MODEL, 128), F32).at[:, :N_EXPERTS].set(rw)
    rb_pad = jnp.zeros((1, 128), F32).at[0, :N_EXPERTS].set(rb)
    h, logits = _router(x2, npre, scale, shift, rw_pad, rb_pad, seq)
    logits = logits[:, :N_EXPERTS]
    top_logit, top_idx = lax.top_k(logits, TOP_K)
    top_p = jax.nn.softmax(top_logit, axis=-1)
    n_assign = t * TOP_K
    flat_e = top_idx.reshape(-1)
    flat_tok = jnp.repeat(jnp.arange(t, dtype=jnp.int32), TOP_K)
    order = jnp.argsort(flat_e)
    e_sorted = flat_e[order]
    counts = jnp.bincount(flat_e, length=N_EXPERTS)
    padded = (counts + MOE_BLOCK - 1) // MOE_BLOCK * MOE_BLOCK
    pad_end = jnp.cumsum(padded)
    start = jnp.cumsum(counts) - counts
    dest = ((pad_end - padded)[e_sorted] + jnp.arange(n_assign) - start[e_sorted]).astype(jnp.int32)
    n_blocks = -(-n_assign // MOE_BLOCK) + N_EXPERTS
    rows = n_blocks * MOE_BLOCK
    row_tok = jnp.full((rows,), t, jnp.int32).at[dest].set(flat_tok[order])
    block_e = jnp.minimum(jnp.searchsorted(pad_end, jnp.arange(n_blocks) * MOE_BLOCK, side='right'),
                          N_EXPERTS - 1).astype(jnp.int32)
    h_pad = jnp.concatenate([h, jnp.zeros((1, D_MODEL), h.dtype)], axis=0)
    xs = h_pad[row_tok]
    yb = _moe_ffn(block_e, xs, wg, wu, wd)
    pos = jnp.zeros((n_assign,), jnp.int32).at[order].set(dest).reshape(t, TOP_K)
    p_pad = jnp.zeros((t, 128), F32).at[:, :TOP_K].set(top_p)
    return _moe_combine(yb[pos[:, 0]], yb[pos[:, 1]], p_pad, x2, gate, npost, seq)


def _in_weights(w_in):
    a0, r0, s0, f0 = 0, ATTN_IN, ATTN_IN + RWKV_IN, ATTN_IN + RWKV_IN + SSD_IN
    zeros = lambda n: jnp.zeros((D_MODEL, n), w_in.dtype)
    return jnp.concatenate([
        w_in[:, r0:s0],
        w_in[:, s0:f0], zeros(SEC_W - SSD_IN),
        w_in[:, f0:f0 + GW],
        zeros(SEC_A - SEC_F - GW),
        w_in[:, a0:r0],
    ], axis=1).astype(BF16)


def kernel(x, c, ada_w, ada_b, norm_pre, norm_post, w_in, w_out, attn_sink, rwkv_mu, rwkv_w0, rwkv_w_up, rwkv_a0, rwkv_a_up, rwkv_g_up, rwkv_k_k, rwkv_k_a, rwkv_r_k, rwkv_ln_w, rwkv_ln_b, ssd_conv_w, ssd_conv_b, ssd_dt_bias, ssd_a_log, ssd_d, ssd_norm_w, s5_lam_re, s5_lam_im, s5_log_step, s5_b_re, s5_b_im, s5_c_re, s5_c_im, s5_d, s5_glu_w, s5_glu_b, ffn_w_gate, ffn_w_up, ffn_w_down, moe_router_w, moe_router_b, moe_w_gate, moe_w_up, moe_w_down):
    bsz, seq, d = x.shape
    depth = ada_w.shape[0]
    mod = _ada_modulation(c, ada_w, ada_b)
    mod = mod.reshape(depth, 2, bsz, 3, 1, d)
    x2 = x.reshape(bsz * seq, d)
    for i in range(depth):
        shift, scale, gate = mod[i, 0, :, 0], mod[i, 0, :, 1], mod[i, 0, :, 2]
        p = _in_projection(x2, norm_pre[i, 0][None], scale, shift, _in_weights(w_in[i]), seq)
        y_attn = _attention(p, attn_sink[i][None], bsz, seq)
        y_rwkv = _rwkv(p, _rwkv_params(rwkv_mu[i], rwkv_w0[i], rwkv_w_up[i], rwkv_a0[i], rwkv_a_up[i],
                                       rwkv_g_up[i], rwkv_k_k[i], rwkv_k_a[i], rwkv_r_k[i].reshape(-1),
                                       rwkv_ln_w[i], rwkv_ln_b[i]), bsz, seq)
        y_ssd = _ssd(p, _ssd_params(ssd_conv_w[i], ssd_conv_b[i], ssd_dt_bias[i], ssd_a_log[i], ssd_d[i],
                                    ssd_norm_w[i]), bsz, seq)
        y_s5 = _s5(p, _s5_params(s5_lam_re[i], s5_lam_im[i], s5_log_step[i], s5_b_re[i], s5_b_im[i],
                                 s5_c_re[i], s5_c_im[i], s5_d[i], s5_glu_w[i], s5_glu_b[i]), bsz, seq)
        x2 = _out_projection((y_attn, y_rwkv, y_ssd, y_s5), w_out[i].astype(BF16), x2, gate,
                             norm_post[i, 0][None], seq)
        shift, scale, gate = mod[i, 1, :, 0], mod[i, 1, :, 1], mod[i, 1, :, 2]
        j = i // 2
        if i % 2 == 0:
            x2 = _dense_ffn(x2, norm_pre[i, 1][None], scale, shift, ffn_w_gate[j].astype(BF16),
                            ffn_w_up[j].astype(BF16), ffn_w_down[j].astype(BF16), gate, norm_post[i, 1][None], seq)
        else:
            x2 = _routed_ffn(x2, norm_pre[i, 1][None], scale, shift, moe_router_w[j], moe_router_b[j],
                             moe_w_gate[j].astype(BF16), moe_w_up[j].astype(BF16), moe_w_down[j].astype(BF16),
                             gate, norm_post[i, 1][None], seq)
    return x2.reshape(bsz, seq, d)
```

```python
import functools
import math

import jax
import jax.numpy as jnp
from jax import lax
from jax.experimental import pallas as pl
from jax.experimental.pallas import tpu as pltpu

F32 = jnp.float32
BF16 = jnp.bfloat16
HIGHEST = lax.Precision.HIGHEST

D_MODEL = 2048
DEPTH = 4
GW = 512
NORM_EPS = 1e-6
HEAD = 64
N_HEADS = 8

ATTN_KV_HEADS = 2
ATTN_GROUP = 4
ATTN_BLOCK = 128
ATTN_IN = 768

RWKV_IN = 1792
RWKV_LN_EPS = 64e-5
RWKV_CHUNK = 64
RWKV_SEQS = 4

SSD_STATE = 128
SSD_CONV = 4
SSD_CHUNK = 128
SSD_CONV_CH = 1024
SSD_IN = 1544
SSD_NORM_EPS = 1e-5

S5_GROUPS = 32
S5_GROUP_CH = 16
S5_STATE = 64
S5_LANES = S5_GROUPS * S5_STATE
S5_CHUNK = 256
S5_TILE = 8

D_FF = 5632
N_EXPERTS = 8
TOP_K = 2
D_EXPERT = 2816
MOE_BLOCK = 512

SEC_R = 0
SEC_S = 1792
SEC_F = 3584
SEC_A = 4608
P_WIDTH = 5376
SEC_W = 1792

VMEM_LIMIT = 52 * 1024 * 1024


def _params(sem):
    return pltpu.CompilerParams(dimension_semantics=sem, vmem_limit_bytes=VMEM_LIMIT)


def _nt(a, b, **kw):
    return lax.dot_general(a, b, (((1,), (1,)), ((), ())), preferred_element_type=F32, **kw)


def _tn(a, b, **kw):
    return lax.dot_general(a, b, (((0,), (0,)), ((), ())), preferred_element_type=F32, **kw)


def _mm(a, b, **kw):
    return jnp.dot(a, b, preferred_element_type=F32, **kw)


def _split_mm(a, x, parts):
    acc = None
    for _ in range(parts):
        piece = x.astype(BF16)
        term = _mm(a, piece)
        acc = term if acc is None else acc + term
        x = x - piece.astype(F32)
    return acc


def _split_mm_rhs(x, b, parts):
    acc = None
    for _ in range(parts):
        piece = x.astype(BF16)
        term = _mm(piece, b)
        acc = term if acc is None else acc + term
        x = x - piece.astype(F32)
    return acc


def _softplus(x):
    return jnp.maximum(x, 0.0) + jnp.log1p(jnp.exp(-jnp.abs(x)))


def _silu(x):
    return x * jax.nn.sigmoid(x)


def _rms(x, eps):
    return x * lax.rsqrt(jnp.mean(x * x, axis=-1, keepdims=True) + eps)


def _ada_kernel(c_ref, w_ref, b_ref, o_ref):
    c = c_ref[...]
    o_ref[0] = _mm(_silu(c).astype(BF16), w_ref[0].astype(BF16)) + b_ref[0]


def _ada_modulation(c, ada_w, ada_b):
    bsz = c.shape[0]
    rows = 8 * pl.cdiv(bsz, 8)
    c_pad = jnp.zeros((rows, D_MODEL), F32).at[:bsz].set(c)
    n_mod = ada_w.shape[0] * 2
    w = ada_w.reshape(n_mod, D_MODEL, 3 * D_MODEL)
    b = ada_b.reshape(n_mod, 1, 3 * D_MODEL)
    tn = 768
    out = pl.pallas_call(
        _ada_kernel,
        grid=(n_mod, 3 * D_MODEL // tn),
        in_specs=[
            pl.BlockSpec((rows, D_MODEL), lambda m, j: (0, 0)),
            pl.BlockSpec((1, D_MODEL, tn), lambda m, j: (m, 0, j)),
            pl.BlockSpec((1, 1, tn), lambda m, j: (m, 0, j)),
        ],
        out_specs=pl.BlockSpec((1, rows, tn), lambda m, j: (m, 0, j)),
        out_shape=jax.ShapeDtypeStruct((n_mod, rows, 3 * D_MODEL), F32),
        compiler_params=_params(("parallel", "parallel")),
        name="ada_modulation",
    )(c_pad, w, b)
    return out[:, :bsz]


def _prenorm(x, nw, scale, shift):
    return (_rms(x, NORM_EPS) * nw) * (1.0 + scale) + shift


def _inproj_kernel(x_ref, nw_ref, sc_ref, sh_ref, w_ref, o_ref, h_ref):
    @pl.when(pl.program_id(1) == 0)
    def _():
        h_ref[...] = _prenorm(x_ref[...], nw_ref[...], sc_ref[0], sh_ref[0]).astype(BF16)

    o_ref[...] = _mm(h_ref[...], w_ref[...])


def _in_projection(x2, nw, scale, shift, w, seq):
    t = x2.shape[0]
    tm, tn = 1024, 768
    per_b = seq // tm
    return pl.pallas_call(
        _inproj_kernel,
        grid=(t // tm, P_WIDTH // tn),
        in_specs=[
            pl.BlockSpec((tm, D_MODEL), lambda i, j: (i, 0)),
            pl.BlockSpec((1, D_MODEL), lambda i, j: (0, 0)),
            pl.BlockSpec((1, 1, D_MODEL), lambda i, j: (i // per_b, 0, 0)),
            pl.BlockSpec((1, 1, D_MODEL), lambda i, j: (i // per_b, 0, 0)),
            pl.BlockSpec((D_MODEL, tn), lambda i, j: (0, j)),
        ],
        out_specs=pl.BlockSpec((tm, tn), lambda i, j: (i, j)),
        out_shape=jax.ShapeDtypeStruct((t, P_WIDTH), F32),
        scratch_shapes=[pltpu.VMEM((tm, D_MODEL), BF16)],
        compiler_params=_params(("parallel", "arbitrary")),
        name="in_projection",
    )(x2, nw, scale, shift, w)


def _alibi_slope(h):
    return 2.0 ** (-8.0 * (h + 1) / N_HEADS)


def _attn_kernel(sink_ref, cur_ref, prev_ref, o_ref):
    n = pl.program_id(1)
    blk = ATTN_BLOCK
    cur = cur_ref[...]
    prev = prev_ref[...]
    q = cur[:, :GW]
    k2 = jnp.concatenate([prev[:, :128], cur[:, GW:GW + 128]], axis=0).astype(BF16)
    v2 = jnp.concatenate([prev[:, 128:256], cur[:, GW + 128:GW + 256]], axis=0).astype(BF16)
    qi = lax.broadcasted_iota(jnp.int32, (blk, 2 * blk), 0) + blk
    kj = lax.broadcasted_iota(jnp.int32, (blk, 2 * blk), 1)
    dist = qi - kj
    valid = (dist >= 0) & (dist < blk) & ((kj >= blk) | (n > 0))
    distf = dist.astype(F32)
    heads = range(N_HEADS)
    kg = [k2[:, HEAD * g:HEAD * (g + 1)] for g in range(ATTN_KV_HEADS)]
    vg = [v2[:, HEAD * g:HEAD * (g + 1)] for g in range(ATTN_KV_HEADS)]
    s = [_nt(q[:, HEAD * h:HEAD * (h + 1)].astype(BF16), kg[h // ATTN_GROUP]) for h in heads]
    s = [jnp.where(valid, s[h] * (HEAD ** -0.5) - _alibi_slope(h) * distf, -jnp.inf) for h in heads]
    m = [jnp.maximum(jnp.max(s[h], axis=-1, keepdims=True), sink_ref[0, h]) for h in heads]
    p = [jnp.exp(s[h] - m[h]) for h in heads]
    denom = [jnp.sum(p[h], axis=-1, keepdims=True) + jnp.exp(sink_ref[0, h] - m[h]) for h in heads]
    outs = [_mm(p[h].astype(BF16), vg[h // ATTN_GROUP]) / denom[h] for h in heads]
    o_ref[...] = jnp.concatenate(outs, axis=-1).astype(o_ref.dtype)


def _attention(p, sinks, bsz, seq):
    nb = seq // ATTN_BLOCK
    cur_col = SEC_A // ATTN_IN
    kv_col = (SEC_A + GW) // 256
    return pl.pallas_call(
        _attn_kernel,
        grid=(bsz, nb),
        in_specs=[
            pl.BlockSpec(memory_space=pltpu.SMEM),
            pl.BlockSpec((ATTN_BLOCK, ATTN_IN), lambda b, n: (b * nb + n, cur_col)),
            pl.BlockSpec((ATTN_BLOCK, 256), lambda b, n: (b * nb + jnp.maximum(n - 1, 0), kv_col)),
        ],
        out_specs=pl.BlockSpec((ATTN_BLOCK, GW), lambda b, n: (b * nb + n, 0)),
        out_shape=jax.ShapeDtypeStruct((bsz * seq, GW), BF16),
        compiler_params=_params(("parallel", "arbitrary")),
        name="swa_attention",
    )(sinks, p, p)


def _rwkv_kernel(p_ref, mu_ref, w0_ref, wup_ref, a0_ref, aup_ref, gup_ref, kk_ref, ka_ref,
                 rk_ref, lnw_ref, lnb_ref, hsum_ref, o_ref, last_ref, state_ref):
    c = RWKV_CHUNK
    nseq = p_ref.shape[0]
    rows = nseq * c

    @pl.when(pl.program_id(1) == 0)
    def _():
        last_ref[...] = jnp.zeros_like(last_ref)
        state_ref[...] = jnp.zeros_like(state_ref)

    p = p_ref[...].reshape(rows, RWKV_IN)
    row = lax.broadcasted_iota(jnp.int32, (rows, 1), 0)
    shifted = pltpu.roll(p, 1, axis=0)
    for j in range(nseq):
        shifted = jnp.where(row == j * c, last_ref[j:j + 1, :], shifted)
        last_ref[j:j + 1, :] = p[(j + 1) * c - 1:(j + 1) * c, :]
    pm = p + mu_ref[...] * (shifted - p)

    r = pm[:, 0:GW]
    k = pm[:, GW:2 * GW]
    v = pm[:, 2 * GW:3 * GW]
    wa = pm[:, 3 * GW:3 * GW + 128]
    gd = pm[:, 3 * GW + 128:3 * GW + 256]

    w = -_softplus(-(w0_ref[...] + _mm(jnp.tanh(wa).astype(BF16), wup_ref[...]))) - 0.5
    logd = -jnp.exp(w)
    a = jax.nn.sigmoid(a0_ref[...] + _mm(wa.astype(BF16), aup_ref[...]))
    g = _mm(jax.nn.sigmoid(gd).astype(BF16), gup_ref[...])

    kk = k * kk_ref[...]
    sumsq = _split_mm_rhs(kk * kk, hsum_ref[...], 2)
    kk = kk / jnp.maximum(jnp.sqrt(sumsq), 1e-12)
    k = k * (1.0 + (a - 1.0) * ka_ref[...])
    b = kk * a

    rr = lax.broadcasted_iota(jnp.int32, (rows, rows), 0)
    cr = lax.broadcasted_iota(jnp.int32, (rows, rows), 1)
    tril_seq = ((rr >= cr) & (rr // c == cr // c)).astype(BF16)
    cum = _split_mm(tril_seq, logd, 3)
    g_inv = jnp.exp(-cum)
    at16 = ((-kk) * jnp.exp(cum - logd)).astype(BF16)
    bt = (b * g_inv).astype(BF16)
    kt = (k * g_inv).astype(BF16)
    rt = (r * jnp.exp(cum)).astype(BF16)
    v16 = v.astype(BF16)
    rk = r * k * rk_ref[...]

    ri = lax.broadcasted_iota(jnp.int32, (c, c), 0)
    ci = lax.broadcasted_iota(jnp.int32, (c, c), 1)
    lower = ri > ci
    eye = (ri == ci).astype(F32)
    ri2 = lax.broadcasted_iota(jnp.int32, (c, 2 * c), 0)
    ci2 = lax.broadcasted_iota(jnp.int32, (c, 2 * c), 1)
    k_half = ci2 >= c
    cj2 = jnp.where(k_half, ci2 - c, ci2)
    strict_k = k_half & (ri2 > cj2)
    lower_eq2 = ri2 >= cj2
    levels = []
    size = 1
    while size < c:
        levels.append((ri // (2 * size) == ci // (2 * size)) & (ri // size > ci // size))
        size *= 2

    units = [(j, h) for j in range(nseq) for h in range(N_HEADS)]
    rs = {u: slice(u[0] * c, (u[0] + 1) * c) for u in units}
    ls = {u: slice(HEAD * u[1], HEAD * (u[1] + 1)) for u in units}
    v_h = {u: v16[rs[u], ls[u]] for u in units}
    ar = {u: jnp.concatenate([at16[rs[u], ls[u]], rt[rs[u], ls[u]]], axis=0) for u in units}
    bk = {u: jnp.concatenate([bt[rs[u], ls[u]], kt[rs[u], ls[u]]], axis=0) for u in units}
    s0 = {u: state_ref[u[0], u[1]] for u in units}
    gram = {u: _nt(ar[u], bk[u]) for u in units}
    l_ab = {u: jnp.where(lower, gram[u][:c, :c], 0.0).astype(BF16) for u in units}
    l_ak = {u: jnp.where(strict_k, gram[u][:c, :], 0.0).astype(BF16) for u in units}
    m_r = {u: jnp.where(lower_eq2, gram[u][c:, :], 0.0).astype(BF16) for u in units}
    inv = {u: eye + jnp.where(levels[0], l_ab[u].astype(F32), 0.0) for u in units}
    for lvl in levels[1:]:
        inv16 = {u: inv[u].astype(BF16) for u in units}
        wl = {u: _mm(jnp.where(lvl, l_ab[u], jnp.zeros_like(l_ab[u])), inv16[u]).astype(BF16) for u in units}
        inv = {u: inv[u] + _mm(inv16[u], wl[u]) for u in units}
    sx = {u: _nt(ar[u], s0[u].astype(BF16)) for u in units}
    x = {u: sx[u][:c] + _mm(l_ak[u], jnp.concatenate([v_h[u], v_h[u]], axis=0)) for u in units}
    us = {u: _mm(inv[u].astype(BF16), x[u].astype(BF16)).astype(BF16) for u in units}
    uv = {u: jnp.concatenate([us[u], v_h[u]], axis=0) for u in units}
    y = {u: sx[u][c:] + _mm(m_r[u], uv[u]) for u in units}
    for u in units:
        g_end = jnp.exp(cum[rs[u].stop - 1:rs[u].stop, ls[u]])
        state_ref[u[0], u[1]] = (s0[u] + _tn(uv[u], bk[u])) * g_end
    outs = []
    for j in range(nseq):
        heads = []
        for h in range(N_HEADS):
            u = (j, h)
            mean = jnp.mean(y[u], axis=-1, keepdims=True)
            yc = y[u] - mean
            var = jnp.mean(yc * yc, axis=-1, keepdims=True)
            bonus = jnp.sum(rk[rs[u], ls[u]], axis=-1, keepdims=True) * v[rs[u], ls[u]]
            heads.append(yc * lax.rsqrt(var + RWKV_LN_EPS) * lnw_ref[:, ls[u]] + lnb_ref[:, ls[u]] + bonus)
        outs.append(jnp.concatenate(heads, axis=-1))
    out = jnp.concatenate(outs, axis=0) * g
    o_ref[...] = out.reshape(nseq, c, GW).astype(o_ref.dtype)


def _rwkv(p, prm, bsz, seq):
    c = RWKV_CHUNK
    nc = seq // c
    nseq = math.gcd(bsz, RWKV_SEQS)
    col = SEC_R // SEC_W
    vec = lambda width: pl.BlockSpec((1, width), lambda b, n: (0, 0))
    mat = lambda rows: pl.BlockSpec((rows, GW), lambda b, n: (0, 0))
    out = pl.pallas_call(
        _rwkv_kernel,
        grid=(bsz // nseq, nc),
        in_specs=[
            pl.BlockSpec((nseq, c, RWKV_IN), lambda b, n: (b, n, col)),
            vec(RWKV_IN), vec(GW), mat(128), vec(GW), mat(128), mat(128),
            vec(GW), vec(GW), vec(GW), vec(GW), vec(GW),
            pl.BlockSpec((GW, GW), lambda b, n: (0, 0)),
        ],
        out_specs=pl.BlockSpec((nseq, c, GW), lambda b, n: (b, n, 0)),
        out_shape=jax.ShapeDtypeStruct((bsz, seq, GW), BF16),
        scratch_shapes=[pltpu.VMEM((8, RWKV_IN), F32), pltpu.VMEM((nseq, N_HEADS, HEAD, HEAD), F32)],
        compiler_params=_params(("parallel", "arbitrary")),
        name="rwkv7_chunked",
    )(p.reshape(bsz, seq, P_WIDTH), *prm)
    return out.reshape(bsz * seq, GW)


def _rwkv_params(mu, w0, w_up, a0, a_up, g_up, k_k, k_a, r_k, ln_w, ln_b):
    zeros = jnp.zeros((64, GW), F32)
    wup = jnp.concatenate([w_up, zeros], axis=0).astype(BF16)
    aup = jnp.concatenate([zeros, a_up], axis=0).astype(BF16)
    head = jnp.arange(GW) // HEAD
    hsum = (head[:, None] == head[None, :]).astype(BF16)
    row = lambda t: t.reshape(1, -1)
    return (row(mu), row(w0), wup, row(a0), aup, g_up.astype(BF16), row(k_k), row(k_a),
            row(r_k), row(ln_w), row(ln_b), hsum)


def _ssd_kernel(p_ref, cw_ref, cb_ref, dtb_ref, a_ref, dsk_ref, nw_ref, o_ref, tail_ref, state_ref):
    c = SSD_CHUNK

    @pl.when(pl.program_id(1) == 0)
    def _():
        tail_ref[...] = jnp.zeros_like(tail_ref)
        state_ref[...] = jnp.zeros_like(state_ref)

    blk = p_ref[...]
    z = blk[:, :GW]
    raw = blk[:, GW:GW + SSD_CONV_CH]
    ext = jnp.concatenate([tail_ref[...], raw], axis=0)
    tail_ref[...] = raw[c - 8:c, :]
    conv = cb_ref[...]
    for i in range(SSD_CONV):
        off = 8 - (SSD_CONV - 1) + i
        conv = conv + cw_ref[i:i + 1, :] * ext[off:off + c, :]
    xbc = _silu(conv)
    x = xbc[:, :GW]
    bm = xbc[:, GW:GW + 256].astype(BF16)
    cm = xbc[:, GW + 256:GW + 512].astype(BF16)

    dt = _softplus(blk[:, GW + SSD_CONV_CH:GW + SSD_CONV_CH + 128] + dtb_ref[...])
    da = dt * a_ref[...]
    ri = lax.broadcasted_iota(jnp.int32, (c, c), 0)
    ci = lax.broadcasted_iota(jnp.int32, (c, c), 1)
    lower_eq = ri >= ci
    cs = _mm(lower_eq.astype(F32), da, precision=HIGHEST)
    cs_t = cs.T
    cs_end = cs[c - 1:c, :]
    dec_in = jnp.exp(cs)
    dec_out = jnp.exp(cs_end - cs)
    dec_all = jnp.exp(cs_end)

    heads = range(N_HEADS)
    groups = range(N_HEADS // 4)
    sls = [slice(HEAD * h, HEAD * (h + 1)) for h in heads]
    bg = [bm[:, SSD_STATE * g:SSD_STATE * (g + 1)] for g in groups]
    cg = [cm[:, SSD_STATE * g:SSD_STATE * (g + 1)] for g in groups]
    cb = [_nt(cg[g], bg[g]) for g in groups]
    s0 = [state_ref[h] for h in heads]
    lmat = [jnp.exp(jnp.where(lower_eq, cs[:, h:h + 1] - cs_t[h:h + 1, :], -jnp.inf)) for h in heads]
    xdt = [x[:, sls[h]] * dt[:, h:h + 1] for h in heads]
    y_in = [_mm((cb[h // 4] * lmat[h]).astype(BF16), xdt[h].astype(BF16)) for h in heads]
    y_st = [_nt(cg[h // 4], s0[h].astype(BF16)) * dec_in[:, h:h + 1] for h in heads]
    new = [_tn((xdt[h] * dec_out[:, h:h + 1]).astype(BF16), bg[h // 4]) for h in heads]
    for h in heads:
        state_ref[h] = s0[h] * dec_all[:, h:h + 1] + new[h]
    outs = [y_in[h] + y_st[h] + x[:, sls[h]] * dsk_ref[:, sls[h]] for h in heads]
    y = jnp.concatenate(outs, axis=-1) * _silu(z)
    half = GW // 2
    y = jnp.concatenate([_rms(y[:, :half], SSD_NORM_EPS), _rms(y[:, half:], SSD_NORM_EPS)], axis=-1)
    o_ref[...] = (y * nw_ref[...]).astype(o_ref.dtype)


def _ssd(p, prm, bsz, seq):
    c = SSD_CHUNK
    nc = seq // c
    col = SEC_S // SEC_W
    vec = lambda width: pl.BlockSpec((1, width), lambda b, n: (0, 0))
    return pl.pallas_call(
        _ssd_kernel,
        grid=(bsz, nc),
        in_specs=[
            pl.BlockSpec((c, SEC_W), lambda b, n: (b * nc + n, col)),
            pl.BlockSpec((SSD_CONV, SSD_CONV_CH), lambda b, n: (0, 0)),
            vec(SSD_CONV_CH), vec(128), vec(128), vec(GW), vec(GW),
        ],
        out_specs=pl.BlockSpec((c, GW), lambda b, n: (b * nc + n, 0)),
        out_shape=jax.ShapeDtypeStruct((bsz * seq, GW), BF16),
        scratch_shapes=[pltpu.VMEM((8, SSD_CONV_CH), F32), pltpu.VMEM((N_HEADS, HEAD, SSD_STATE), F32)],
        compiler_params=_params(("parallel", "arbitrary")),
        name="mamba2_ssd",
    )(p, *prm)


def _ssd_params(conv_w, conv_b, dt_bias, a_log, d_skip, norm_w):
    pad = lambda t: jnp.zeros((1, 128), F32).at[0, :N_HEADS].set(t)
    return (conv_w, conv_b.reshape(1, -1), pad(dt_bias), pad(-jnp.exp(a_log)),
            jnp.repeat(d_skip, HEAD).reshape(1, -1), norm_w.reshape(1, -1))


def _gelu_tanh(x):
    return 0.5 * x * (1.0 + jnp.tanh(math.sqrt(2.0 / math.pi) * (x + 0.044715 * (x * x * x))))


def _s5_kernel(u_ref, bbr_ref, bbi_ref, lvr_ref, lvi_ref, pwr_ref, pwi_ref, ccr_ref, cci_ref,
               dsk_ref, gw_ref, gb_ref, o_ref, hr_ref, hi_ref, carry_ref):
    @pl.when(pl.program_id(1) == 0)
    def _():
        carry_ref[...] = jnp.zeros_like(carry_ref)

    u = u_ref[...]
    u16 = u.astype(BF16)
    hr_ref[...] = _mm(u16, bbr_ref[...])
    hi_ref[...] = _mm(u16, bbi_ref[...])

    def tile(t, carry):
        cr, ci = carry
        rows = pl.ds(pl.multiple_of(t * S5_TILE, S5_TILE), S5_TILE)
        xr = hr_ref[rows, :]
        xi = hi_ref[rows, :]
        for lvl, s in enumerate((1, 2, 4)):
            lr = lvr_ref[lvl]
            li = lvi_ref[lvl]
            sr = pltpu.roll(xr, s, axis=0)
            si = pltpu.roll(xi, s, axis=0)
            xr, xi = xr + lr * sr - li * si, xi + lr * si + li * sr
        pr = pwr_ref[...]
        pi = pwi_ref[...]
        xr, xi = xr + pr * cr - pi * ci, xi + pr * ci + pi * cr
        hr_ref[rows, :] = xr
        hi_ref[rows, :] = xi
        last = S5_TILE - 1
        return (jnp.broadcast_to(xr[last:last + 1, :], xr.shape), jnp.broadcast_to(xi[last:last + 1, :], xi.shape))

    cr, ci = lax.fori_loop(0, u.shape[0] // S5_TILE, tile, (carry_ref[0], carry_ref[1]))
    carry_ref[0] = cr
    carry_ref[1] = ci

    y = _mm(hr_ref[...].astype(BF16), ccr_ref[...]) - _mm(hi_ref[...].astype(BF16), cci_ref[...])
    y = _gelu_tanh(y + dsk_ref[...] * u)
    gate = jax.nn.sigmoid(_mm(y.astype(BF16), gw_ref[...]) + gb_ref[...])
    o_ref[...] = (y * gate).astype(o_ref.dtype)


def _s5(p, prm, bsz, seq):
    c = S5_CHUNK
    nc = seq // c
    col = SEC_F // GW
    full = lambda shape: pl.BlockSpec(shape, lambda b, n: (0,) * len(shape))
    return pl.pallas_call(
        _s5_kernel,
        grid=(bsz, nc),
        in_specs=[
            pl.BlockSpec((c, GW), lambda b, n: (b * nc + n, col)),
            full((GW, S5_LANES)), full((GW, S5_LANES)),
            full((3, S5_TILE, S5_LANES)), full((3, S5_TILE, S5_LANES)),
            full((S5_TILE, S5_LANES)), full((S5_TILE, S5_LANES)),
            full((S5_LANES, GW)), full((S5_LANES, GW)),
            full((1, GW)), full((GW, GW)), full((1, GW)),
        ],
        out_specs=pl.BlockSpec((c, GW), lambda b, n: (b * nc + n, 0)),
        out_shape=jax.ShapeDtypeStruct((bsz * seq, GW), BF16),
        scratch_shapes=[pltpu.VMEM((c, S5_LANES), F32), pltpu.VMEM((c, S5_LANES), F32),
                        pltpu.VMEM((2, S5_TILE, S5_LANES), F32)],
        compiler_params=_params(("parallel", "arbitrary")),
        name="s5_scan",
    )(p, *prm)


def _s5_params(lam_re, lam_im, log_step, b_re, b_im, c_re, c_im, d_skip, glu_w, glu_b):
    step = jnp.exp(log_step)[:, None]
    mag = jnp.exp(lam_re * step)
    ab_re, ab_im = mag * jnp.cos(lam_im * step), mag * jnp.sin(lam_im * step)
    den = lam_re * lam_re + lam_im * lam_im
    coef_re = ((ab_re - 1) * lam_re + ab_im * lam_im) / den
    coef_im = (ab_im * lam_re - (ab_re - 1) * lam_im) / den
    bb_re = coef_re[..., None] * b_re - coef_im[..., None] * b_im
    bb_im = coef_re[..., None] * b_im + coef_im[..., None] * b_re
    eye = jnp.eye(S5_GROUPS, dtype=F32)
    in_proj = lambda t: jnp.einsum('gni,gh->gihn', t, eye).reshape(GW, S5_LANES).astype(BF16)
    out_proj = lambda t: jnp.einsum('gin,gh->gnhi', t, eye).reshape(S5_LANES, GW).astype(BF16)

    def cmul(a, b):
        return a[0] * b[0] - a[1] * b[1], a[0] * b[1] + a[1] * b[0]

    lam1 = (ab_re.reshape(-1), ab_im.reshape(-1))
    lam2 = cmul(lam1, lam1)
    lam4 = cmul(lam2, lam2)
    rows = jnp.arange(S5_TILE)[:, None]
    lvl_re = jnp.stack([jnp.where(rows >= s, l[0][None, :], 0.0) for s, l in ((1, lam1), (2, lam2), (4, lam4))])
    lvl_im = jnp.stack([jnp.where(rows >= s, l[1][None, :], 0.0) for s, l in ((1, lam1), (2, lam2), (4, lam4))])
    powers = [lam1]
    for _ in range(S5_TILE - 1):
        powers.append(cmul(powers[-1], lam1))
    pw_re = jnp.stack([q[0] for q in powers])
    pw_im = jnp.stack([q[1] for q in powers])
    return (in_proj(bb_re), in_proj(bb_im), lvl_re, lvl_im, pw_re, pw_im, out_proj(c_re), out_proj(c_im),
            d_skip.reshape(1, -1), glu_w.astype(BF16), glu_b.reshape(1, -1))


def _outproj_kernel(ya_ref, yr_ref, ys_ref, yf_ref, w_ref, x_ref, g_ref, nw_ref, o_ref):
    acc = _mm(ya_ref[...], w_ref[0:GW, :])
    acc = acc + _mm(yr_ref[...], w_ref[GW:2 * GW, :])
    acc = acc + _mm(ys_ref[...], w_ref[2 * GW:3 * GW, :])
    acc = acc + _mm(yf_ref[...], w_ref[3 * GW:4 * GW, :])
    o_ref[...] = x_ref[...] + g_ref[0] * (_rms(acc, NORM_EPS) * nw_ref[...])


def _out_projection(ys, w, x2, gate, nw, seq):
    t = x2.shape[0]
    tm = 512
    per_b = seq // tm
    ymap = pl.BlockSpec((tm, GW), lambda i: (i, 0))
    return pl.pallas_call(
        _outproj_kernel,
        grid=(t // tm,),
        in_specs=[
            ymap, ymap, ymap, ymap,
            pl.BlockSpec((D_MODEL, D_MODEL), lambda i: (0, 0)),
            pl.BlockSpec((tm, D_MODEL), lambda i: (i, 0)),
            pl.BlockSpec((1, 1, D_MODEL), lambda i: (i // per_b, 0, 0)),
            pl.BlockSpec((1, D_MODEL), lambda i: (0, 0)),
        ],
        out_specs=pl.BlockSpec((tm, D_MODEL), lambda i: (i, 0)),
        out_shape=jax.ShapeDtypeStruct((t, D_MODEL), F32),
        compiler_params=_params(("parallel",)),
        name="out_projection",
    )(*ys, w, x2, gate, nw)


def _ffn_kernel(x_ref, npre_ref, sc_ref, sh_ref, wg_ref, wu_ref, wd_ref, g_ref, npost_ref, o_ref,
                h_ref, acc_ref):
    f = pl.program_id(1)

    @pl.when(f == 0)
    def _():
        h_ref[...] = _prenorm(x_ref[...], npre_ref[...], sc_ref[0], sh_ref[0]).astype(BF16)
        acc_ref[...] = jnp.zeros_like(acc_ref)

    h = h_ref[...]
    act = _silu(_mm(h, wg_ref[...])) * _mm(h, wu_ref[...])
    acc_ref[...] += _mm(act.astype(BF16), wd_ref[...])

    @pl.when(f == pl.num_programs(1) - 1)
    def _():
        o_ref[...] = x_ref[...] + g_ref[0] * (_rms(acc_ref[...], NORM_EPS) * npost_ref[...])


def _dense_ffn(x2, npre, scale, shift, wg, wu, wd, gate, npost, seq):
    t = x2.shape[0]
    tm, tf = 512, 512
    per_b = seq // tm
    mod = pl.BlockSpec((1, 1, D_MODEL), lambda i, f: (i // per_b, 0, 0))
    vec = pl.BlockSpec((1, D_MODEL), lambda i, f: (0, 0))
    return pl.pallas_call(
        _ffn_kernel,
        grid=(t // tm, D_FF // tf),
        in_specs=[
            pl.BlockSpec((tm, D_MODEL), lambda i, f: (i, 0)),
            vec, mod, mod,
            pl.BlockSpec((D_MODEL, tf), lambda i, f: (0, f)),
            pl.BlockSpec((D_MODEL, tf), lambda i, f: (0, f)),
            pl.BlockSpec((tf, D_MODEL), lambda i, f: (f, 0)),
            mod, vec,
        ],
        out_specs=pl.BlockSpec((tm, D_MODEL), lambda i, f: (i, 0)),
        out_shape=jax.ShapeDtypeStruct((t, D_MODEL), F32),
        scratch_shapes=[pltpu.VMEM((tm, D_MODEL), BF16), pltpu.VMEM((tm, D_MODEL), F32)],
        compiler_params=_params(("parallel", "arbitrary")),
        name="dense_swiglu",
    )(x2, npre, scale, shift, wg, wu, wd, gate, npost)


def _router_kernel(x_ref, npre_ref, sc_ref, sh_ref, rw_ref, rb_ref, h_ref, lg_ref):
    h = _prenorm(x_ref[...], npre_ref[...], sc_ref[0], sh_ref[0])
    h_ref[...] = h.astype(BF16)
    lg_ref[...] = _mm(h, rw_ref[...], precision=HIGHEST) + rb_ref[...]


def _router(x2, npre, scale, shift, rw, rb, seq):
    t = x2.shape[0]
    tm = 512
    per_b = seq // tm
    mod = pl.BlockSpec((1, 1, D_MODEL), lambda i: (i // per_b, 0, 0))
    return pl.pallas_call(
        _router_kernel,
        grid=(t // tm,),
        in_specs=[
            pl.BlockSpec((tm, D_MODEL), lambda i: (i, 0)),
            pl.BlockSpec((1, D_MODEL), lambda i: (0, 0)),
            mod, mod,
            pl.BlockSpec((D_MODEL, 128), lambda i: (0, 0)),
            pl.BlockSpec((1, 128), lambda i: (0, 0)),
        ],
        out_specs=[pl.BlockSpec((tm, D_MODEL), lambda i: (i, 0)), pl.BlockSpec((tm, 128), lambda i: (i, 0))],
        out_shape=[jax.ShapeDtypeStruct((t, D_MODEL), BF16), jax.ShapeDtypeStruct((t, 128), F32)],
        compiler_params=_params(("parallel",)),
        name="moe_router",
    )(x2, npre, scale, shift, rw, rb)


def _moe_ffn_kernel(be_ref, xs_ref, wg_ref, wu_ref, wd_ref, o_ref, acc_ref):
    f = pl.program_id(1)

    @pl.when(f == 0)
    def _():
        acc_ref[...] = jnp.zeros_like(acc_ref)

    xs = xs_ref[...]
    act = _silu(_mm(xs, wg_ref[0])) * _mm(xs, wu_ref[0])
    acc_ref[...] += _mm(act.astype(BF16), wd_ref[0])

    @pl.when(f == pl.num_programs(1) - 1)
    def _():
        o_ref[...] = acc_ref[...]


def _moe_ffn(block_e, xs, wg, wu, wd):
    rows = xs.shape[0]
    tf = 256
    grid_spec = pltpu.PrefetchScalarGridSpec(
        num_scalar_prefetch=1,
        grid=(rows // MOE_BLOCK, D_EXPERT // tf),
        in_specs=[
            pl.BlockSpec((MOE_BLOCK, D_MODEL), lambda i, f, be: (i, 0)),
            pl.BlockSpec((1, D_MODEL, tf), lambda i, f, be: (be[i], 0, f)),
            pl.BlockSpec((1, D_MODEL, tf), lambda i, f, be: (be[i], 0, f)),
            pl.BlockSpec((1, tf, D_MODEL), lambda i, f, be: (be[i], f, 0)),
        ],
        out_specs=pl.BlockSpec((MOE_BLOCK, D_MODEL), lambda i, f, be: (i, 0)),
        scratch_shapes=[pltpu.VMEM((MOE_BLOCK, D_MODEL), F32)],
    )
    return pl.pallas_call(
        _moe_ffn_kernel,
        grid_spec=grid_spec,
        out_shape=jax.ShapeDtypeStruct((rows, D_MODEL), F32),
        compiler_params=_params(("parallel", "arbitrary")),
        name="moe_swiglu",
    )(block_e, xs, wg, wu, wd)


def _combine_kernel(y0_ref, y1_ref, p_ref, x_ref, g_ref, npost_ref, o_ref):
    p = p_ref[...]
    y = y0_ref[...] * p[:, 0:1] + y1_ref[...] * p[:, 1:2]
    o_ref[...] = x_ref[...] + g_ref[0] * (_rms(y, NORM_EPS) * npost_ref[...])


def _moe_combine(y0, y1, top_p, x2, gate, npost, seq):
    t = x2.shape[0]
    tm = 512
    per_b = seq // tm
    row = pl.BlockSpec((tm, D_MODEL), lambda i: (i, 0))
    return pl.pallas_call(
        _combine_kernel,
        grid=(t // tm,),
        in_specs=[row, row, pl.BlockSpec((tm, 128), lambda i: (i, 0)), row,
                  pl.BlockSpec((1, 1, D_MODEL), lambda i: (i // per_b, 0, 0)),
                  pl.BlockSpec((1, D_MODEL), lambda i: (0, 0))],
        out_specs=row,
        out_shape=jax.ShapeDtypeStruct((t, D_MODEL), F32),
        compiler_params=_params(("parallel",)),
        name="moe_combine",
    )(y0, y1, top_p, x2, gate, npost)


def _routed_ffn(x2, npre, scale, shift, rw, rb, wg, wu, wd, gate, npost, seq):
    t = x2.shape[0]
    rw_pad = jnp.zeros((D_MODEL, 128), F32).at[:, :N_EXPERTS].set(rw)
    rb_pad = jnp.zeros((1, 128), F32).at[0, :N_EXPERTS].set(rb)
    h, logits = _router(x2, npre, scale, shift, rw_pad, rb_pad, seq)
    logits = logits[:, :N_EXPERTS]
    top_logit, top_idx = lax.top_k(logits, TOP_K)
    top_p = jax.nn.softmax(top_logit, axis=-1)
    n_assign = t * TOP_K
    flat_e = top_idx.reshape(-1)
    flat_tok = jnp.repeat(jnp.arange(t, dtype=jnp.int32), TOP_K)
    order = jnp.argsort(flat_e)
    e_sorted = flat_e[order]
    counts = jnp.bincount(flat_e, length=N_EXPERTS)
    padded = (counts + MOE_BLOCK - 1) // MOE_BLOCK * MOE_BLOCK
    pad_end = jnp.cumsum(padded)
    start = jnp.cumsum(counts) - counts
    dest = ((pad_end - padded)[e_sorted] + jnp.arange(n_assign) - start[e_sorted]).astype(jnp.int32)
    n_blocks = -(-n_assign // MOE_BLOCK) + N_EXPERTS
    rows = n_blocks * MOE_BLOCK
    row_tok = jnp.full((rows,), t, jnp.int32).at[dest].set(flat_tok[order])
    block_e = jnp.minimum(jnp.searchsorted(pad_end, jnp.arange(n_blocks) * MOE_BLOCK, side='right'),
                          N_EXPERTS - 1).astype(jnp.int32)
    h_pad = jnp.concatenate([h, jnp.zeros((1, D_MODEL), h.dtype)], axis=0)
    xs = h_pad[row_tok]
    yb = _moe_ffn(block_e, xs, wg, wu, wd)
    pos = jnp.zeros((n_assign,), jnp.int32).at[order].set(dest).reshape(t, TOP_K)
    p_pad = jnp.zeros((t, 128), F32).at[:, :TOP_K].set(top_p)
    return _moe_combine(yb[pos[:, 0]], yb[pos[:, 1]], p_pad, x2, gate, npost, seq)


def _in_weights(w_in):
    a0, r0, s0, f0 = 0, ATTN_IN, ATTN_IN + RWKV_IN, ATTN_IN + RWKV_IN + SSD_IN
    zeros = lambda n: jnp.zeros((D_MODEL, n), w_in.dtype)
    return jnp.concatenate([
        w_in[:, r0:s0],
        w_in[:, s0:f0], zeros(SEC_W - SSD_IN),
        w_in[:, f0:f0 + GW],
        zeros(SEC_A - SEC_F - GW),
        w_in[:, a0:r0],
    ], axis=1).astype(BF16)


def kernel(x, c, ada_w, ada_b, norm_pre, norm_post, w_in, w_out, attn_sink, rwkv_mu, rwkv_w0, rwkv_w_up, rwkv_a0, rwkv_a_up, rwkv_g_up, rwkv_k_k, rwkv_k_a, rwkv_r_k, rwkv_ln_w, rwkv_ln_b, ssd_conv_w, ssd_conv_b, ssd_dt_bias, ssd_a_log, ssd_d, ssd_norm_w, s5_lam_re, s5_lam_im, s5_log_step, s5_b_re, s5_b_im, s5_c_re, s5_c_im, s5_d, s5_glu_w, s5_glu_b, ffn_w_gate, ffn_w_up, ffn_w_down, moe_router_w, moe_router_b, moe_w_gate, moe_w_up, moe_w_down):
    bsz, seq, d = x.shape
    depth = ada_w.shape[0]
    mod = _ada_modulation(c, ada_w, ada_b)
    mod = mod.reshape(depth, 2, bsz, 3, 1, d)
    x2 = x.reshape(bsz * seq, d)
    for i in range(depth):
        shift, scale, gate = mod[i, 0, :, 0], mod[i, 0, :, 1], mod[i, 0, :, 2]
        p = _in_projection(x2, norm_pre[i, 0][None], scale, shift, _in_weights(w_in[i]), seq)
        y_attn = _attention(p, attn_sink[i][None], bsz, seq)
        y_rwkv = _rwkv(p, _rwkv_params(rwkv_mu[i], rwkv_w0[i], rwkv_w_up[i], rwkv_a0[i], rwkv_a_up[i],
                                       rwkv_g_up[i], rwkv_k_k[i], rwkv_k_a[i], rwkv_r_k[i].reshape(-1),
                                       rwkv_ln_w[i], rwkv_ln_b[i]), bsz, seq)
        y_ssd = _ssd(p, _ssd_params(ssd_conv_w[i], ssd_conv_b[i], ssd_dt_bias[i], ssd_a_log[i], ssd_d[i],
                                    ssd_norm_w[i]), bsz, seq)
        y_s5 = _s5(p, _s5_params(s5_lam_re[i], s5_lam_im[i], s5_log_step[i], s5_b_re[i], s5_b_im[i],
                                 s5_c_re[i], s5_c_im[i], s5_d[i], s5_glu_w[i], s5_glu_b[i]), bsz, seq)
        x2 = _out_projection((y_attn, y_rwkv, y_ssd, y_s5), w_out[i].astype(BF16), x2, gate,
                             norm_post[i, 0][None], seq)
        shift, scale, gate = mod[i, 1, :, 0], mod[i, 1, :, 1], mod[i, 1, :, 2]
        j = i // 2
        if i % 2 == 0:
            x2 = _dense_ffn(x2, norm_pre[i, 1][None], scale, shift, ffn_w_gate[j].astype(BF16),
                            ffn_w_up[j].astype(BF16), ffn_w_down[j].astype(BF16), gate, norm_post[i, 1][None], seq)
        else:
            x2 = _routed_ffn(x2, norm_pre[i, 1][None], scale, shift, moe_router_w[j], moe_router_b[j],
                             moe_w_gate[j].astype(BF16), moe_w_up[j].astype(BF16), moe_w_down[j].astype(BF16),
                             gate, norm_post[i, 1][None], seq)
    return x2.reshape(bsz, seq, d)
```

```python
import functools
import math

import jax
import jax.numpy as jnp
from jax import lax
from jax.experimental import pallas as pl
from jax.experimental.pallas import tpu as pltpu

F32 = jnp.float32
BF16 = jnp.bfloat16
HIGHEST = lax.Precision.HIGHEST

D_MODEL = 2048
DEPTH = 4
GW = 512
NORM_EPS = 1e-6
HEAD = 64
N_HEADS = 8

ATTN_KV_HEADS = 2
ATTN_GROUP = 4
ATTN_BLOCK = 128
ATTN_IN = 768

RWKV_IN = 1792
RWKV_LN_EPS = 64e-5
RWKV_CHUNK = 64
RWKV_SEQS = 4

SSD_STATE = 128
SSD_CONV = 4
SSD_CHUNK = 128
SSD_CONV_CH = 1024
SSD_IN = 1544
SSD_NORM_EPS = 1e-5

S5_GROUPS = 32
S5_GROUP_CH = 16
S5_STATE = 64
S5_LANES = S5_GROUPS * S5_STATE
S5_CHUNK = 256
S5_TILE = 8
S5_QUADS = 4

D_FF = 5632
N_EXPERTS = 8
TOP_K = 2
D_EXPERT = 2816
MOE_BLOCK = 512

SEC_R = 0
SEC_S = 1792
SEC_F = 3584
SEC_Q = 4096
SEC_KV = 4608
P_WIDTH = 4864
SEC_W = 1792

VMEM_LIMIT = 52 * 1024 * 1024


def _params(sem):
    return pltpu.CompilerParams(dimension_semantics=sem, vmem_limit_bytes=VMEM_LIMIT)


def _nt(a, b, **kw):
    return lax.dot_general(a, b, (((1,), (1,)), ((), ())), preferred_element_type=F32, **kw)


def _tn(a, b, **kw):
    return lax.dot_general(a, b, (((0,), (0,)), ((), ())), preferred_element_type=F32, **kw)


def _mm(a, b, **kw):
    return jnp.dot(a, b, preferred_element_type=F32, **kw)


def _split_mm(a, x, parts):
    acc = None
    for _ in range(parts):
        piece = x.astype(BF16)
        term = _mm(a, piece)
        acc = term if acc is None else acc + term
        x = x - piece.astype(F32)
    return acc


def _split_mm_rhs(x, b, parts):
    acc = None
    for _ in range(parts):
        piece = x.astype(BF16)
        term = _mm(piece, b)
        acc = term if acc is None else acc + term
        x = x - piece.astype(F32)
    return acc


def _softplus(x):
    return jnp.maximum(x, 0.0) + jnp.log1p(jnp.exp(-jnp.abs(x)))


def _silu(x):
    return x * jax.nn.sigmoid(x)


def _rms(x, eps):
    return x * lax.rsqrt(jnp.mean(x * x, axis=-1, keepdims=True) + eps)


def _ada_kernel(c_ref, w_ref, b_ref, o_ref):
    c = c_ref[...]
    o_ref[0] = _mm(_silu(c).astype(BF16), w_ref[0].astype(BF16)) + b_ref[0]


def _ada_modulation(c, ada_w, ada_b):
    bsz = c.shape[0]
    rows = 8 * pl.cdiv(bsz, 8)
    c_pad = jnp.zeros((rows, D_MODEL), F32).at[:bsz].set(c)
    n_mod = ada_w.shape[0] * 2
    w = ada_w.reshape(n_mod, D_MODEL, 3 * D_MODEL)
    b = ada_b.reshape(n_mod, 1, 3 * D_MODEL)
    tn = 768
    out = pl.pallas_call(
        _ada_kernel,
        grid=(n_mod, 3 * D_MODEL // tn),
        in_specs=[
            pl.BlockSpec((rows, D_MODEL), lambda m, j: (0, 0)),
            pl.BlockSpec((1, D_MODEL, tn), lambda m, j: (m, 0, j)),
            pl.BlockSpec((1, 1, tn), lambda m, j: (m, 0, j)),
        ],
        out_specs=pl.BlockSpec((1, rows, tn), lambda m, j: (m, 0, j)),
        out_shape=jax.ShapeDtypeStruct((n_mod, rows, 3 * D_MODEL), F32),
        compiler_params=_params(("parallel", "parallel")),
        name="ada_modulation",
    )(c_pad, w, b)
    return out[:, :bsz]


def _prenorm(x, nw, scale, shift):
    return (_rms(x, NORM_EPS) * nw) * (1.0 + scale) + shift


def _inproj_kernel(x_ref, nw_ref, sc_ref, sh_ref, w_ref, o_ref, h_ref):
    @pl.when(pl.program_id(1) == 0)
    def _():
        h_ref[...] = _prenorm(x_ref[...], nw_ref[...], sc_ref[0], sh_ref[0]).astype(BF16)

    o_ref[...] = _mm(h_ref[...], w_ref[0])


def _in_projection(x2, nw, scale, shift, w, layer, seq):
    t = x2.shape[0]
    tm, tn = 512, P_WIDTH // 2
    per_b = seq // tm
    return pl.pallas_call(
        _inproj_kernel,
        grid=(t // tm, P_WIDTH // tn),
        in_specs=[
            pl.BlockSpec((tm, D_MODEL), lambda i, j: (i, 0)),
            pl.BlockSpec((1, D_MODEL), lambda i, j: (0, 0)),
            pl.BlockSpec((1, 1, D_MODEL), lambda i, j: (i // per_b, 0, 0)),
            pl.BlockSpec((1, 1, D_MODEL), lambda i, j: (i // per_b, 0, 0)),
            pl.BlockSpec((1, D_MODEL, tn), lambda i, j: (layer, 0, j)),
        ],
        out_specs=pl.BlockSpec((tm, tn), lambda i, j: (i, j)),
        out_shape=jax.ShapeDtypeStruct((t, P_WIDTH), F32),
        scratch_shapes=[pltpu.VMEM((tm, D_MODEL), BF16)],
        compiler_params=_params(("parallel", "arbitrary")),
        name="in_projection",
    )(x2, nw, scale, shift, w)


def _alibi_slope(h):
    return 2.0 ** (-8.0 * (h + 1) / N_HEADS)


def _attn_kernel(sink_ref, q_ref, cur_ref, prev_ref, o_ref):
    n = pl.program_id(1)
    blk = ATTN_BLOCK
    cur = cur_ref[...]
    prev = prev_ref[...]
    q = q_ref[...]
    k2 = jnp.concatenate([prev[:, :128], cur[:, :128]], axis=0).astype(BF16)
    v2 = jnp.concatenate([prev[:, 128:256], cur[:, 128:256]], axis=0).astype(BF16)
    qi = lax.broadcasted_iota(jnp.int32, (blk, 2 * blk), 0) + blk
    kj = lax.broadcasted_iota(jnp.int32, (blk, 2 * blk), 1)
    dist = qi - kj
    valid = (dist >= 0) & (dist < blk) & ((kj >= blk) | (n > 0))
    distf = dist.astype(F32)
    heads = range(N_HEADS)
    kg = [k2[:, HEAD * g:HEAD * (g + 1)] for g in range(ATTN_KV_HEADS)]
    vg = [v2[:, HEAD * g:HEAD * (g + 1)] for g in range(ATTN_KV_HEADS)]
    s = [_nt(q[:, HEAD * h:HEAD * (h + 1)].astype(BF16), kg[h // ATTN_GROUP]) for h in heads]
    s = [jnp.where(valid, s[h] * (HEAD ** -0.5) - _alibi_slope(h) * distf, -jnp.inf) for h in heads]
    m = [jnp.maximum(jnp.max(s[h], axis=-1, keepdims=True), sink_ref[0, h]) for h in heads]
    p = [jnp.exp(s[h] - m[h]) for h in heads]
    denom = [jnp.sum(p[h], axis=-1, keepdims=True) + jnp.exp(sink_ref[0, h] - m[h]) for h in heads]
    outs = [_mm(p[h].astype(BF16), vg[h // ATTN_GROUP]) / denom[h] for h in heads]
    o_ref[...] = jnp.concatenate(outs, axis=-1).astype(o_ref.dtype)


def _attention(p, sinks, bsz, seq):
    nb = seq // ATTN_BLOCK
    q_col = SEC_Q // GW
    kv_col = SEC_KV // 256
    return pl.pallas_call(
        _attn_kernel,
        grid=(bsz, nb),
        in_specs=[
            pl.BlockSpec(memory_space=pltpu.SMEM),
            pl.BlockSpec((ATTN_BLOCK, GW), lambda b, n: (b * nb + n, q_col)),
            pl.BlockSpec((ATTN_BLOCK, 256), lambda b, n: (b * nb + n, kv_col)),
            pl.BlockSpec((ATTN_BLOCK, 256), lambda b, n: (b * nb + jnp.maximum(n - 1, 0), kv_col)),
        ],
        out_specs=pl.BlockSpec((ATTN_BLOCK, GW), lambda b, n: (b * nb + n, 0)),
        out_shape=jax.ShapeDtypeStruct((bsz * seq, GW), BF16),
        compiler_params=_params(("parallel", "arbitrary")),
        name="swa_attention",
    )(sinks, p, p, p)


def _rwkv_kernel(p_ref, mu_ref, w0_ref, wup_ref, a0_ref, aup_ref, gup_ref, kk_ref, ka_ref,
                 rk_ref, lnw_ref, lnb_ref, hsum_ref, o_ref, last_ref, state_ref):
    c = RWKV_CHUNK
    nseq = p_ref.shape[0]
    rows = nseq * c

    @pl.when(pl.program_id(1) == 0)
    def _():
        last_ref[...] = jnp.zeros_like(last_ref)
        state_ref[...] = jnp.zeros_like(state_ref)

    p = p_ref[...].reshape(rows, RWKV_IN)
    row = lax.broadcasted_iota(jnp.int32, (rows, 1), 0)
    shifted = pltpu.roll(p, 1, axis=0)
    for j in range(nseq):
        shifted = jnp.where(row == j * c, last_ref[j:j + 1, :], shifted)
        last_ref[j:j + 1, :] = p[(j + 1) * c - 1:(j + 1) * c, :]
    pm = p + mu_ref[...] * (shifted - p)

    r = pm[:, 0:GW]
    k = pm[:, GW:2 * GW]
    v = pm[:, 2 * GW:3 * GW]
    wa = pm[:, 3 * GW:3 * GW + 128]
    gd = pm[:, 3 * GW + 128:3 * GW + 256]

    w = -_softplus(-(w0_ref[...] + _mm(jnp.tanh(wa).astype(BF16), wup_ref[...]))) - 0.5
    logd = -jnp.exp(w)
    a = jax.nn.sigmoid(a0_ref[...] + _mm(wa.astype(BF16), aup_ref[...]))
    g = _mm(jax.nn.sigmoid(gd).astype(BF16), gup_ref[...])

    kk = k * kk_ref[...]
    sumsq = _split_mm_rhs(kk * kk, hsum_ref[...], 2)
    kk = kk / jnp.maximum(jnp.sqrt(sumsq), 1e-12)
    k = k * (1.0 + (a - 1.0) * ka_ref[...])
    b = kk * a

    rr = lax.broadcasted_iota(jnp.int32, (rows, rows), 0)
    cr = lax.broadcasted_iota(jnp.int32, (rows, rows), 1)
    tril_seq = ((rr >= cr) & (rr // c == cr // c)).astype(BF16)
    cum = _split_mm(tril_seq, logd, 3)
    g_inv = jnp.exp(-cum)
    at16 = ((-kk) * jnp.exp(cum - logd)).astype(BF16)
    bt = (b * g_inv).astype(BF16)
    kt = (k * g_inv).astype(BF16)
    rt = (r * jnp.exp(cum)).astype(BF16)
    v16 = v.astype(BF16)
    rk = r * k * rk_ref[...]

    ri = lax.broadcasted_iota(jnp.int32, (c, c), 0)
    ci = lax.broadcasted_iota(jnp.int32, (c, c), 1)
    lower = ri > ci
    eye = (ri == ci).astype(F32)
    ri2 = lax.broadcasted_iota(jnp.int32, (c, 2 * c), 0)
    ci2 = lax.broadcasted_iota(jnp.int32, (c, 2 * c), 1)
    k_half = ci2 >= c
    cj2 = jnp.where(k_half, ci2 - c, ci2)
    strict_k = k_half & (ri2 > cj2)
    lower_eq2 = ri2 >= cj2
    levels = []
    size = 1
    while size < c:
        levels.append((ri // (2 * size) == ci // (2 * size)) & (ri // size > ci // size))
        size *= 2

    units = [(j, h) for j in range(nseq) for h in range(N_HEADS)]
    rs = {u: slice(u[0] * c, (u[0] + 1) * c) for u in units}
    ls = {u: slice(HEAD * u[1], HEAD * (u[1] + 1)) for u in units}
    v_h = {u: v16[rs[u], ls[u]] for u in units}
    ar = {u: jnp.concatenate([at16[rs[u], ls[u]], rt[rs[u], ls[u]]], axis=0) for u in units}
    bk = {u: jnp.concatenate([bt[rs[u], ls[u]], kt[rs[u], ls[u]]], axis=0) for u in units}
    s0 = {u: state_ref[u[0], u[1]] for u in units}
    gram = {u: _nt(ar[u], bk[u]) for u in units}
    l_ab = {u: jnp.where(lower, gram[u][:c, :c], 0.0).astype(BF16) for u in units}
    l_ak = {u: jnp.where(strict_k, gram[u][:c, :], 0.0).astype(BF16) for u in units}
    m_r = {u: jnp.where(lower_eq2, gram[u][c:, :], 0.0).astype(BF16) for u in units}
    inv = {u: eye + jnp.where(levels[0], l_ab[u].astype(F32), 0.0) for u in units}
    for lvl in levels[1:]:
        inv16 = {u: inv[u].astype(BF16) for u in units}
        wl = {u: _mm(jnp.where(lvl, l_ab[u], jnp.zeros_like(l_ab[u])), inv16[u]).astype(BF16) for u in units}
        inv = {u: inv[u] + _mm(inv16[u], wl[u]) for u in units}
    sx = {u: _nt(ar[u], s0[u].astype(BF16)) for u in units}
    x = {u: sx[u][:c] + _mm(l_ak[u], jnp.concatenate([v_h[u], v_h[u]], axis=0)) for u in units}
    us = {u: _mm(inv[u].astype(BF16), x[u].astype(BF16)).astype(BF16) for u in units}
    uv = {u: jnp.concatenate([us[u], v_h[u]], axis=0) for u in units}
    y = {u: sx[u][c:] + _mm(m_r[u], uv[u]) for u in units}
    for u in units:
        g_end = jnp.exp(cum[rs[u].stop - 1:rs[u].stop, ls[u]])
        state_ref[u[0], u[1]] = (s0[u] + _tn(uv[u], bk[u])) * g_end
    outs = []
    for j in range(nseq):
        heads = []
        for h in range(N_HEADS):
            u = (j, h)
            mean = jnp.mean(y[u], axis=-1, keepdims=True)
            yc = y[u] - mean
            var = jnp.mean(yc * yc, axis=-1, keepdims=True)
            bonus = jnp.sum(rk[rs[u], ls[u]], axis=-1, keepdims=True) * v[rs[u], ls[u]]
            heads.append(yc * lax.rsqrt(var + RWKV_LN_EPS) * lnw_ref[:, ls[u]] + lnb_ref[:, ls[u]] + bonus)
        outs.append(jnp.concatenate(heads, axis=-1))
    out = jnp.concatenate(outs, axis=0) * g
    o_ref[...] = out.reshape(nseq, c, GW).astype(o_ref.dtype)


def _rwkv(p, prm, bsz, seq):
    c = RWKV_CHUNK
    nc = seq // c
    nseq = math.gcd(bsz, RWKV_SEQS)
    col = SEC_R // SEC_W
    vec = lambda width: pl.BlockSpec((1, width), lambda b, n: (0, 0))
    mat = lambda rows: pl.BlockSpec((rows, GW), lambda b, n: (0, 0))
    out = pl.pallas_call(
        _rwkv_kernel,
        grid=(bsz // nseq, nc),
        in_specs=[
            pl.BlockSpec((nseq, c, RWKV_IN), lambda b, n: (b, n, col)),
            vec(RWKV_IN), vec(GW), mat(128), vec(GW), mat(128), mat(128),
            vec(GW), vec(GW), vec(GW), vec(GW), vec(GW),
            pl.BlockSpec((GW, GW), lambda b, n: (0, 0)),
        ],
        out_specs=pl.BlockSpec((nseq, c, GW), lambda b, n: (b, n, 0)),
        out_shape=jax.ShapeDtypeStruct((bsz, seq, GW), BF16),
        scratch_shapes=[pltpu.VMEM((8, RWKV_IN), F32), pltpu.VMEM((nseq, N_HEADS, HEAD, HEAD), F32)],
        compiler_params=_params(("parallel", "arbitrary")),
        name="rwkv7_chunked",
    )(p.reshape(bsz, seq, P_WIDTH), *prm)
    return out.reshape(bsz * seq, GW)


def _rwkv_params(mu, w0, w_up, a0, a_up, g_up, k_k, k_a, r_k, ln_w, ln_b):
    zeros = jnp.zeros((64, GW), F32)
    wup = jnp.concatenate([w_up, zeros], axis=0).astype(BF16)
    aup = jnp.concatenate([zeros, a_up], axis=0).astype(BF16)
    head = jnp.arange(GW) // HEAD
    hsum = (head[:, None] == head[None, :]).astype(BF16)
    row = lambda t: t.reshape(1, -1)
    return (row(mu), row(w0), wup, row(a0), aup, g_up.astype(BF16), row(k_k), row(k_a),
            row(r_k), row(ln_w), row(ln_b), hsum)


def _ssd_kernel(p_ref, cw_ref, cb_ref, dtb_ref, a_ref, dsk_ref, nw_ref, o_ref, tail_ref, state_ref):
    c = SSD_CHUNK

    @pl.when(pl.program_id(1) == 0)
    def _():
        tail_ref[...] = jnp.zeros_like(tail_ref)
        state_ref[...] = jnp.zeros_like(state_ref)

    blk = p_ref[...]
    z = blk[:, :GW]
    raw = blk[:, GW:GW + SSD_CONV_CH]
    ext = jnp.concatenate([tail_ref[...], raw], axis=0)
    tail_ref[...] = raw[c - 8:c, :]
    conv = cb_ref[...]
    for i in range(SSD_CONV):
        off = 8 - (SSD_CONV - 1) + i
        conv = conv + cw_ref[i:i + 1, :] * ext[off:off + c, :]
    xbc = _silu(conv)
    x = xbc[:, :GW]
    bm = xbc[:, GW:GW + 256].astype(BF16)
    cm = xbc[:, GW + 256:GW + 512].astype(BF16)

    dt = _softplus(blk[:, GW + SSD_CONV_CH:GW + SSD_CONV_CH + 128] + dtb_ref[...])
    da = dt * a_ref[...]
    ri = lax.broadcasted_iota(jnp.int32, (c, c), 0)
    ci = lax.broadcasted_iota(jnp.int32, (c, c), 1)
    lower_eq = ri >= ci
    cs = _mm(lower_eq.astype(F32), da, precision=HIGHEST)
    cs_t = cs.T
    cs_end = cs[c - 1:c, :]
    dec_in = jnp.exp(cs)
    dec_out = jnp.exp(cs_end - cs)
    dec_all = jnp.exp(cs_end)

    heads = range(N_HEADS)
    groups = range(N_HEADS // 4)
    sls = [slice(HEAD * h, HEAD * (h + 1)) for h in heads]
    bg = [bm[:, SSD_STATE * g:SSD_STATE * (g + 1)] for g in groups]
    cg = [cm[:, SSD_STATE * g:SSD_STATE * (g + 1)] for g in groups]
    cb = [_nt(cg[g], bg[g]) for g in groups]
    s0 = [state_ref[h] for h in heads]
    lmat = [jnp.exp(jnp.where(lower_eq, cs[:, h:h + 1] - cs_t[h:h + 1, :], -jnp.inf)) for h in heads]
    xdt = [x[:, sls[h]] * dt[:, h:h + 1] for h in heads]
    y_in = [_mm((cb[h // 4] * lmat[h]).astype(BF16), xdt[h].astype(BF16)) for h in heads]
    y_st = [_nt(cg[h // 4], s0[h].astype(BF16)) * dec_in[:, h:h + 1] for h in heads]
    new = [_tn((xdt[h] * dec_out[:, h:h + 1]).astype(BF16), bg[h // 4]) for h in heads]
    for h in heads:
        state_ref[h] = s0[h] * dec_all[:, h:h + 1] + new[h]
    outs = [y_in[h] + y_st[h] + x[:, sls[h]] * dsk_ref[:, sls[h]] for h in heads]
    y = jnp.concatenate(outs, axis=-1) * _silu(z)
    half = GW // 2
    y = jnp.concatenate([_rms(y[:, :half], SSD_NORM_EPS), _rms(y[:, half:], SSD_NORM_EPS)], axis=-1)
    o_ref[...] = (y * nw_ref[...]).astype(o_ref.dtype)


def _ssd(p, prm, bsz, seq):
    c = SSD_CHUNK
    nc = seq // c
    col = SEC_S // SEC_W
    vec = lambda width: pl.BlockSpec((1, width), lambda b, n: (0, 0))
    return pl.pallas_call(
        _ssd_kernel,
        grid=(bsz, nc),
        in_specs=[
            pl.BlockSpec((c, SEC_W), lambda b, n: (b * nc + n, col)),
            pl.BlockSpec((SSD_CONV, SSD_CONV_CH), lambda b, n: (0, 0)),
            vec(SSD_CONV_CH), vec(128), vec(128), vec(GW), vec(GW),
        ],
        out_specs=pl.BlockSpec((c, GW), lambda b, n: (b * nc + n, 0)),
        out_shape=jax.ShapeDtypeStruct((bsz * seq, GW), BF16),
        scratch_shapes=[pltpu.VMEM((8, SSD_CONV_CH), F32), pltpu.VMEM((N_HEADS, HEAD, SSD_STATE), F32)],
        compiler_params=_params(("parallel", "arbitrary")),
        name="mamba2_ssd",
    )(p, *prm)


def _ssd_params(conv_w, conv_b, dt_bias, a_log, d_skip, norm_w):
    pad = lambda t: jnp.zeros((1, 128), F32).at[0, :N_HEADS].set(t)
    return (conv_w, conv_b.reshape(1, -1), pad(dt_bias), pad(-jnp.exp(a_log)),
            jnp.repeat(d_skip, HEAD).reshape(1, -1), norm_w.reshape(1, -1))


def _gelu_tanh(x):
    return 0.5 * x * (1.0 + jnp.tanh(math.sqrt(2.0 / math.pi) * (x + 0.044715 * (x * x * x))))


def _s5_kernel(u_ref, bbr_ref, bbi_ref, lvr_ref, lvi_ref, pwr_ref, pwi_ref, ccr_ref, cci_ref,
               dsk_ref, gw_ref, gb_ref, o_ref, hr_ref, hi_ref, carry_ref):
    @pl.when(pl.program_id(1) == 0)
    def _():
        carry_ref[...] = jnp.zeros_like(carry_ref)

    u = u_ref[...]
    u16 = u.astype(BF16)
    q_in, q_st = GW // S5_QUADS, S5_LANES // S5_QUADS
    for q in range(S5_QUADS):
        uq = u16[:, q * q_in:(q + 1) * q_in]
        hr_ref[:, q * q_st:(q + 1) * q_st] = _mm(uq, bbr_ref[q])
        hi_ref[:, q * q_st:(q + 1) * q_st] = _mm(uq, bbi_ref[q])

    def tile(t, carry):
        cr, ci = carry
        rows = pl.ds(pl.multiple_of(t * S5_TILE, S5_TILE), S5_TILE)
        xr = hr_ref[rows, :]
        xi = hi_ref[rows, :]
        for lvl, s in enumerate((1, 2, 4)):
            lr = lvr_ref[lvl]
            li = lvi_ref[lvl]
            sr = pltpu.roll(xr, s, axis=0)
            si = pltpu.roll(xi, s, axis=0)
            xr, xi = xr + lr * sr - li * si, xi + lr * si + li * sr
        pr = pwr_ref[...]
        pi = pwi_ref[...]
        xr, xi = xr + pr * cr - pi * ci, xi + pr * ci + pi * cr
        hr_ref[rows, :] = xr
        hi_ref[rows, :] = xi
        last = S5_TILE - 1
        return (jnp.broadcast_to(xr[last:last + 1, :], xr.shape), jnp.broadcast_to(xi[last:last + 1, :], xi.shape))

    cr, ci = lax.fori_loop(0, u.shape[0] // S5_TILE, tile, (carry_ref[0], carry_ref[1]))
    carry_ref[0] = cr
    carry_ref[1] = ci

    y = jnp.concatenate(
        [_mm(hr_ref[:, q * q_st:(q + 1) * q_st].astype(BF16), ccr_ref[q])
         - _mm(hi_ref[:, q * q_st:(q + 1) * q_st].astype(BF16), cci_ref[q]) for q in range(S5_QUADS)], axis=-1)
    y = _gelu_tanh(y + dsk_ref[...] * u)
    gate = jax.nn.sigmoid(_mm(y.astype(BF16), gw_ref[...]) + gb_ref[...])
    o_ref[...] = (y * gate).astype(o_ref.dtype)


def _s5(p, prm, bsz, seq):
    c = S5_CHUNK
    nc = seq // c
    col = SEC_F // GW
    full = lambda shape: pl.BlockSpec(shape, lambda b, n: (0,) * len(shape))
    return pl.pallas_call(
        _s5_kernel,
        grid=(bsz, nc),
        in_specs=[
            pl.BlockSpec((c, GW), lambda b, n: (b * nc + n, col)),
            full((S5_QUADS, GW // S5_QUADS, S5_LANES // S5_QUADS)),
            full((S5_QUADS, GW // S5_QUADS, S5_LANES // S5_QUADS)),
            full((3, S5_TILE, S5_LANES)), full((3, S5_TILE, S5_LANES)),
            full((S5_TILE, S5_LANES)), full((S5_TILE, S5_LANES)),
            full((S5_QUADS, S5_LANES // S5_QUADS, GW // S5_QUADS)),
            full((S5_QUADS, S5_LANES // S5_QUADS, GW // S5_QUADS)),
            full((1, GW)), full((GW, GW)), full((1, GW)),
        ],
        out_specs=pl.BlockSpec((c, GW), lambda b, n: (b * nc + n, 0)),
        out_shape=jax.ShapeDtypeStruct((bsz * seq, GW), BF16),
        scratch_shapes=[pltpu.VMEM((c, S5_LANES), F32), pltpu.VMEM((c, S5_LANES), F32),
                        pltpu.VMEM((2, S5_TILE, S5_LANES), F32)],
        compiler_params=_params(("parallel", "arbitrary")),
        name="s5_scan",
    )(p, *prm)


def _s5_params(lam_re, lam_im, log_step, b_re, b_im, c_re, c_im, d_skip, glu_w, glu_b):
    step = jnp.exp(log_step)[:, None]
    mag = jnp.exp(lam_re * step)
    ab_re, ab_im = mag * jnp.cos(lam_im * step), mag * jnp.sin(lam_im * step)
    den = lam_re * lam_re + lam_im * lam_im
    coef_re = ((ab_re - 1) * lam_re + ab_im * lam_im) / den
    coef_im = (ab_im * lam_re - (ab_re - 1) * lam_im) / den
    bb_re = coef_re[..., None] * b_re - coef_im[..., None] * b_im
    bb_im = coef_re[..., None] * b_im + coef_im[..., None] * b_re
    gq = S5_GROUPS // S5_QUADS
    eye = jnp.eye(gq, dtype=F32)
    in_proj = lambda t: jnp.einsum(
        'qgni,gh->qgihn', t.reshape(S5_QUADS, gq, S5_STATE, S5_GROUP_CH), eye
    ).reshape(S5_QUADS, gq * S5_GROUP_CH, gq * S5_STATE).astype(BF16)
    out_proj = lambda t: jnp.einsum(
        'qgin,gh->qgnhi', t.reshape(S5_QUADS, gq, S5_GROUP_CH, S5_STATE), eye
    ).reshape(S5_QUADS, gq * S5_STATE, gq * S5_GROUP_CH).astype(BF16)

    def cmul(a, b):
        return a[0] * b[0] - a[1] * b[1], a[0] * b[1] + a[1] * b[0]

    lam1 = (ab_re.reshape(-1), ab_im.reshape(-1))
    lam2 = cmul(lam1, lam1)
    lam4 = cmul(lam2, lam2)
    rows = jnp.arange(S5_TILE)[:, None]
    lvl_re = jnp.stack([jnp.where(rows >= s, l[0][None, :], 0.0) for s, l in ((1, lam1), (2, lam2), (4, lam4))])
    lvl_im = jnp.stack([jnp.where(rows >= s, l[1][None, :], 0.0) for s, l in ((1, lam1), (2, lam2), (4, lam4))])
    powers = [lam1]
    for _ in range(S5_TILE - 1):
        powers.append(cmul(powers[-1], lam1))
    pw_re = jnp.stack([q[0] for q in powers])
    pw_im = jnp.stack([q[1] for q in powers])
    return (in_proj(bb_re), in_proj(bb_im), lvl_re, lvl_im, pw_re, pw_im, out_proj(c_re), out_proj(c_im),
            d_skip.reshape(1, -1), glu_w.astype(BF16), glu_b.reshape(1, -1))


def _outproj_kernel(ya_ref, yr_ref, ys_ref, yf_ref, w_ref, x_ref, g_ref, nw_ref, o_ref):
    acc = _mm(ya_ref[...], w_ref[0, 0:GW, :])
    acc = acc + _mm(yr_ref[...], w_ref[0, GW:2 * GW, :])
    acc = acc + _mm(ys_ref[...], w_ref[0, 2 * GW:3 * GW, :])
    acc = acc + _mm(yf_ref[...], w_ref[0, 3 * GW:4 * GW, :])
    o_ref[...] = x_ref[...] + g_ref[0] * (_rms(acc, NORM_EPS) * nw_ref[...])


def _out_projection(ys, w, layer, x2, gate, nw, seq):
    t = x2.shape[0]
    tm = 512
    per_b = seq // tm
    ymap = pl.BlockSpec((tm, GW), lambda i: (i, 0))
    return pl.pallas_call(
        _outproj_kernel,
        grid=(t // tm,),
        in_specs=[
            ymap, ymap, ymap, ymap,
            pl.BlockSpec((1, D_MODEL, D_MODEL), lambda i: (layer, 0, 0)),
            pl.BlockSpec((tm, D_MODEL), lambda i: (i, 0)),
            pl.BlockSpec((1, 1, D_MODEL), lambda i: (i // per_b, 0, 0)),
            pl.BlockSpec((1, D_MODEL), lambda i: (0, 0)),
        ],
        out_specs=pl.BlockSpec((tm, D_MODEL), lambda i: (i, 0)),
        out_shape=jax.ShapeDtypeStruct((t, D_MODEL), F32),
        compiler_params=_params(("parallel",)),
        name="out_projection",
    )(*ys, w, x2, gate, nw)


def _ffn_kernel(x_ref, npre_ref, sc_ref, sh_ref, wg_ref, wu_ref, wd_ref, g_ref, npost_ref, o_ref,
                h_ref, acc_ref):
    f = pl.program_id(1)

    @pl.when(f == 0)
    def _():
        h_ref[...] = _prenorm(x_ref[...], npre_ref[...], sc_ref[0], sh_ref[0]).astype(BF16)
        acc_ref[...] = jnp.zeros_like(acc_ref)

    h = h_ref[...]
    act = _silu(_mm(h, wg_ref[0])) * _mm(h, wu_ref[0])
    acc_ref[...] += _mm(act.astype(BF16), wd_ref[0])

    @pl.when(f == pl.num_programs(1) - 1)
    def _():
        o_ref[...] = x_ref[...] + g_ref[0] * (_rms(acc_ref[...], NORM_EPS) * npost_ref[...])


def _dense_ffn(x2, npre, scale, shift, wg, wu, wd, layer, gate, npost, seq):
    t = x2.shape[0]
    tm, tf = 512, 512
    per_b = seq // tm
    mod = pl.BlockSpec((1, 1, D_MODEL), lambda i, f: (i // per_b, 0, 0))
    vec = pl.BlockSpec((1, D_MODEL), lambda i, f: (0, 0))
    return pl.pallas_call(
        _ffn_kernel,
        grid=(t // tm, D_FF // tf),
        in_specs=[
            pl.BlockSpec((tm, D_MODEL), lambda i, f: (i, 0)),
            vec, mod, mod,
            pl.BlockSpec((1, D_MODEL, tf), lambda i, f: (layer, 0, f)),
            pl.BlockSpec((1, D_MODEL, tf), lambda i, f: (layer, 0, f)),
            pl.BlockSpec((1, tf, D_MODEL), lambda i, f: (layer, f, 0)),
            mod, vec,
        ],
        out_specs=pl.BlockSpec((tm, D_MODEL), lambda i, f: (i, 0)),
        out_shape=jax.ShapeDtypeStruct((t, D_MODEL), F32),
        scratch_shapes=[pltpu.VMEM((tm, D_MODEL), BF16), pltpu.VMEM((tm, D_MODEL), F32)],
        compiler_params=_params(("parallel", "arbitrary")),
        name="dense_swiglu",
    )(x2, npre, scale, shift, wg, wu, wd, gate, npost)


def _router_kernel(x_ref, npre_ref, sc_ref, sh_ref, rw_ref, rb_ref, h_ref, lg_ref):
    h = _prenorm(x_ref[...], npre_ref[...], sc_ref[0], sh_ref[0])
    h_ref[...] = h.astype(BF16)
    lg_ref[...] = _mm(h, rw_ref[...], precision=HIGHEST) + rb_ref[...]


def _router(x2, npre, scale, shift, rw, rb, seq):
    t = x2.shape[0]
    tm = 512
    per_b = seq // tm
    mod = pl.BlockSpec((1, 1, D_MODEL), lambda i: (i // per_b, 0, 0))
    return pl.pallas_call(
        _router_kernel,
        grid=(t // tm,),
        in_specs=[
            pl.BlockSpec((tm, D_MODEL), lambda i: (i, 0)),
            pl.BlockSpec((1, D_MODEL), lambda i: (0, 0)),
            mod, mod,
            pl.BlockSpec((D_MODEL, 128), lambda i: (0, 0)),
            pl.BlockSpec((1, 128), lambda i: (0, 0)),
        ],
        out_specs=[pl.BlockSpec((tm, D_MODEL), lambda i: (i, 0)), pl.BlockSpec((tm, 128), lambda i: (i, 0))],
        out_shape=[jax.ShapeDtypeStruct((t, D_MODEL), BF16), jax.ShapeDtypeStruct((t, 128), F32)],
        compiler_params=_params(("parallel",)),
        name="moe_router",
    )(x2, npre, scale, shift, rw, rb)


def _moe_ffn_kernel(be_ref, used_ref, xs_ref, wg_ref, wu_ref, wd_ref, o_ref, acc_ref):
    i = pl.program_id(0)
    f = pl.program_id(1)
    live = i < used_ref[0]

    @pl.when(f == 0)
    def _():
        acc_ref[...] = jnp.zeros_like(acc_ref)

    @pl.when(live)
    def _():
        xs = xs_ref[...]
        act = _silu(_mm(xs, wg_ref[0, 0])) * _mm(xs, wu_ref[0, 0])
        acc_ref[...] += _mm(act.astype(BF16), wd_ref[0, 0])

    @pl.when(f == pl.num_programs(1) - 1)
    def _():
        o_ref[...] = acc_ref[...].astype(o_ref.dtype)


def _moe_ffn(block_e, n_used, xs, wg, wu, wd, layer):
    rows = xs.shape[0]
    tf = 256
    nf = D_EXPERT // tf

    def tile(i, f, used):
        return jnp.where(i < used[0], f, nf - 1)

    grid_spec = pltpu.PrefetchScalarGridSpec(
        num_scalar_prefetch=2,
        grid=(rows // MOE_BLOCK, nf),
        in_specs=[
            pl.BlockSpec((MOE_BLOCK, D_MODEL), lambda i, f, be, used: (i, 0)),
            pl.BlockSpec((1, 1, D_MODEL, tf), lambda i, f, be, used: (layer, be[i], 0, tile(i, f, used))),
            pl.BlockSpec((1, 1, D_MODEL, tf), lambda i, f, be, used: (layer, be[i], 0, tile(i, f, used))),
            pl.BlockSpec((1, 1, tf, D_MODEL), lambda i, f, be, used: (layer, be[i], tile(i, f, used), 0)),
        ],
        out_specs=pl.BlockSpec((MOE_BLOCK, D_MODEL), lambda i, f, be, used: (i, 0)),
        scratch_shapes=[pltpu.VMEM((MOE_BLOCK, D_MODEL), F32)],
    )
    return pl.pallas_call(
        _moe_ffn_kernel,
        grid_spec=grid_spec,
        out_shape=jax.ShapeDtypeStruct((rows, D_MODEL), BF16),
        compiler_params=_params(("parallel", "arbitrary")),
        name="moe_swiglu",
    )(block_e, n_used, xs, wg, wu, wd)


def _combine_kernel(y0_ref, y1_ref, p_ref, x_ref, g_ref, npost_ref, o_ref):
    p = p_ref[...]
    y = y0_ref[...].astype(F32) * p[:, 0:1] + y1_ref[...].astype(F32) * p[:, 1:2]
    o_ref[...] = x_ref[...] + g_ref[0] * (_rms(y, NORM_EPS) * npost_ref[...])


def _moe_combine(y0, y1, top_p, x2, gate, npost, seq):
    t = x2.shape[0]
    tm = 512
    per_b = seq // tm
    row = pl.BlockSpec((tm, D_MODEL), lambda i: (i, 0))
    return pl.pallas_call(
        _combine_kernel,
        grid=(t // tm,),
        in_specs=[row, row, pl.BlockSpec((tm, 128), lambda i: (i, 0)), row,
                  pl.BlockSpec((1, 1, D_MODEL), lambda i: (i // per_b, 0, 0)),
                  pl.BlockSpec((1, D_MODEL), lambda i: (0, 0))],
        out_specs=row,
        out_shape=jax.ShapeDtypeStruct((t, D_MODEL), F32),
        compiler_params=_params(("parallel",)),
        name="moe_combine",
    )(y0, y1, top_p, x2, gate, npost)


def _routed_ffn(x2, npre, scale, shift, rw, rb, wg, wu, wd, layer, gate, npost, seq):
    t = x2.shape[0]
    rw_pad = jnp.zeros((D_MODEL, 128), F32).at[:, :N_EXPERTS].set(rw)
    rb_pad = jnp.zeros((1, 128), F32).at[0, :N_EXPERTS].set(rb)
    h, logits = _router(x2, npre, scale, shift, rw_pad, rb_pad, seq)
    logits = logits[:, :N_EXPERTS]
    top_logit, top_idx = lax.top_k(logits, TOP_K)
    top_p = jax.nn.softmax(top_logit, axis=-1)
    n_assign = t * TOP_K
    flat_e = top_idx.reshape(-1)
    flat_tok = jnp.repeat(jnp.arange(t, dtype=jnp.int32), TOP_K)
    order = jnp.argsort(flat_e)
    e_sorted = flat_e[order]
    counts = jnp.bincount(flat_e, length=N_EXPERTS)
    padded = (counts + MOE_BLOCK - 1) // MOE_BLOCK * MOE_BLOCK
    pad_end = jnp.cumsum(padded)
    start = jnp.cumsum(counts) - counts
    dest = ((pad_end - padded)[e_sorted] + jnp.arange(n_assign) - start[e_sorted]).astype(jnp.int32)
    n_blocks = -(-n_assign // MOE_BLOCK) + N_EXPERTS
    rows = n_blocks * MOE_BLOCK
    row_tok = jnp.full((rows,), t, jnp.int32).at[dest].set(flat_tok[order])
    block_e = jnp.minimum(jnp.searchsorted(pad_end, jnp.arange(n_blocks) * MOE_BLOCK, side='right'),
                          N_EXPERTS - 1).astype(jnp.int32)
    h_pad = jnp.concatenate([h, jnp.zeros((1, D_MODEL), h.dtype)], axis=0)
    xs = h_pad[row_tok]
    n_used = (pad_end[-1:] // MOE_BLOCK).astype(jnp.int32)
    yb = _moe_ffn(block_e, n_used, xs, wg, wu, wd, layer)
    pos = jnp.zeros((n_assign,), jnp.int32).at[order].set(dest).reshape(t, TOP_K)
    p_pad = jnp.zeros((t, 128), F32).at[:, :TOP_K].set(top_p)
    return _moe_combine(yb[pos[:, 0]], yb[pos[:, 1]], p_pad, x2, gate, npost, seq)


def _in_weights(w_in):
    a0, r0, s0, f0 = 0, ATTN_IN, ATTN_IN + RWKV_IN, ATTN_IN + RWKV_IN + SSD_IN
    zeros = jnp.zeros(w_in.shape[:-1] + (SEC_W - SSD_IN,), w_in.dtype)
    return jnp.concatenate([
        w_in[..., r0:s0],
        w_in[..., s0:f0], zeros,
        w_in[..., f0:f0 + GW],
        w_in[..., a0:r0],
    ], axis=-1).astype(BF16)


def kernel(x, c, ada_w, ada_b, norm_pre, norm_post, w_in, w_out, attn_sink, rwkv_mu, rwkv_w0, rwkv_w_up, rwkv_a0, rwkv_a_up, rwkv_g_up, rwkv_k_k, rwkv_k_a, rwkv_r_k, rwkv_ln_w, rwkv_ln_b, ssd_conv_w, ssd_conv_b, ssd_dt_bias, ssd_a_log, ssd_d, ssd_norm_w, s5_lam_re, s5_lam_im, s5_log_step, s5_b_re, s5_b_im, s5_c_re, s5_c_im, s5_d, s5_glu_w, s5_glu_b, ffn_w_gate, ffn_w_up, ffn_w_down, moe_router_w, moe_router_b, moe_w_gate, moe_w_up, moe_w_down):
    bsz, seq, d = x.shape
    depth = ada_w.shape[0]
    mod = _ada_modulation(c, ada_w, ada_b)
    mod = mod.reshape(depth, 2, bsz, 3, 1, d)
    x2 = x.reshape(bsz * seq, d)
    w_in16 = _in_weights(w_in)
    w_out16 = w_out.astype(BF16)
    ffn16 = (ffn_w_gate.astype(BF16), ffn_w_up.astype(BF16), ffn_w_down.astype(BF16))
    moe16 = (moe_w_gate.astype(BF16), moe_w_up.astype(BF16), moe_w_down.astype(BF16))
    for i in range(depth):
        shift, scale, gate = mod[i, 0, :, 0], mod[i, 0, :, 1], mod[i, 0, :, 2]
        p = _in_projection(x2, norm_pre[i, 0][None], scale, shift, w_in16, i, seq)
        y_attn = _attention(p, attn_sink[i][None], bsz, seq)
        y_rwkv = _rwkv(p, _rwkv_params(rwkv_mu[i], rwkv_w0[i], rwkv_w_up[i], rwkv_a0[i], rwkv_a_up[i],
                                       rwkv_g_up[i], rwkv_k_k[i], rwkv_k_a[i], rwkv_r_k[i].reshape(-1),
                                       rwkv_ln_w[i], rwkv_ln_b[i]), bsz, seq)
        y_ssd = _ssd(p, _ssd_params(ssd_conv_w[i], ssd_conv_b[i], ssd_dt_bias[i], ssd_a_log[i], ssd_d[i],
                                    ssd_norm_w[i]), bsz, seq)
        y_s5 = _s5(p, _s5_params(s5_lam_re[i], s5_lam_im[i], s5_log_step[i], s5_b_re[i], s5_b_im[i],
                                 s5_c_re[i], s5_c_im[i], s5_d[i], s5_glu_w[i], s5_glu_b[i]), bsz, seq)
        x2 = _out_projection((y_attn, y_rwkv, y_ssd, y_s5), w_out16, i, x2, gate, norm_post[i, 0][None], seq)
        shift, scale, gate = mod[i, 1, :, 0], mod[i, 1, :, 1], mod[i, 1, :, 2]
        j = i // 2
        if i % 2 == 0:
            x2 = _dense_ffn(x2, norm_pre[i, 1][None], scale, shift, *ffn16, j, gate, norm_post[i, 1][None], seq)
        else:
            x2 = _routed_ffn(x2, norm_pre[i, 1][None], scale, shift, moe_router_w[j], moe_router_b[j],
                             *moe16, j, gate, norm_post[i, 1][None], seq)
    return x2.reshape(bsz, seq, d)
```

```python
import functools
import math

import jax
import jax.numpy as jnp
from jax import lax
from jax.experimental import pallas as pl
from jax.experimental.pallas import tpu as pltpu

F32 = jnp.float32
BF16 = jnp.bfloat16
HIGHEST = lax.Precision.HIGHEST

D_MODEL = 2048
DEPTH = 4
GW = 512
NORM_EPS = 1e-6
HEAD = 64
N_HEADS = 8

ATTN_KV_HEADS = 2
ATTN_GROUP = 4
ATTN_BLOCK = 128
ATTN_IN = 768
ATTN_BLOCKS_PER_STEP = 4

RWKV_IN = 1792
RWKV_LN_EPS = 64e-5
RWKV_CHUNK = 64
RWKV_SEQS = 4

SSD_STATE = 128
SSD_CONV = 4
SSD_CHUNK = 128
SSD_SEQS = 4
SSD_CONV_CH = 1024
SSD_IN = 1544
SSD_NORM_EPS = 1e-5

S5_GROUPS = 32
S5_GROUP_CH = 16
S5_STATE = 64
S5_LANES = S5_GROUPS * S5_STATE
S5_CHUNK = 256
S5_TILE = 8
S5_QUADS = 4

D_FF = 5632
N_EXPERTS = 8
TOP_K = 2
D_EXPERT = 2816
MOE_BLOCK = 512

SEC_R = 0
SEC_S = 1792
SEC_F = 3584
SEC_Q = 4096
SEC_KV = 4608
P_WIDTH = 4864
SEC_W = 1792

VMEM_LIMIT = 52 * 1024 * 1024
MOE_VMEM_LIMIT = 58 * 1024 * 1024


def _params(sem, vmem=VMEM_LIMIT):
    return pltpu.CompilerParams(dimension_semantics=sem, vmem_limit_bytes=vmem)


def _nt(a, b, **kw):
    return lax.dot_general(a, b, (((1,), (1,)), ((), ())), preferred_element_type=F32, **kw)


def _tn(a, b, **kw):
    return lax.dot_general(a, b, (((0,), (0,)), ((), ())), preferred_element_type=F32, **kw)


def _mm(a, b, **kw):
    return jnp.dot(a, b, preferred_element_type=F32, **kw)


def _split_mm(a, x, parts):
    acc = None
    for _ in range(parts):
        piece = x.astype(BF16)
        term = _mm(a, piece)
        acc = term if acc is None else acc + term
        x = x - piece.astype(F32)
    return acc


def _split_mm_rhs(x, b, parts):
    acc = None
    for _ in range(parts):
        piece = x.astype(BF16)
        term = _mm(piece, b)
        acc = term if acc is None else acc + term
        x = x - piece.astype(F32)
    return acc


def _softplus(x):
    return jnp.maximum(x, 0.0) + jnp.log1p(jnp.exp(-jnp.abs(x)))


def _silu(x):
    return x * jax.nn.sigmoid(x)


def _rms(x, eps):
    return x * lax.rsqrt(jnp.mean(x * x, axis=-1, keepdims=True) + eps)


def _ada_kernel(c_ref, w_ref, b_ref, o_ref):
    c = c_ref[...]
    o_ref[0] = _mm(_silu(c).astype(BF16), w_ref[0].astype(BF16)) + b_ref[0]


def _ada_modulation(c, ada_w, ada_b):
    bsz = c.shape[0]
    rows = 8 * pl.cdiv(bsz, 8)
    c_pad = jnp.zeros((rows, D_MODEL), F32).at[:bsz].set(c)
    n_mod = ada_w.shape[0] * 2
    w = ada_w.reshape(n_mod, D_MODEL, 3 * D_MODEL)
    b = ada_b.reshape(n_mod, 1, 3 * D_MODEL)
    tn = 768
    out = pl.pallas_call(
        _ada_kernel,
        grid=(n_mod, 3 * D_MODEL // tn),
        in_specs=[
            pl.BlockSpec((rows, D_MODEL), lambda m, j: (0, 0)),
            pl.BlockSpec((1, D_MODEL, tn), lambda m, j: (m, 0, j)),
            pl.BlockSpec((1, 1, tn), lambda m, j: (m, 0, j)),
        ],
        out_specs=pl.BlockSpec((1, rows, tn), lambda m, j: (m, 0, j)),
        out_shape=jax.ShapeDtypeStruct((n_mod, rows, 3 * D_MODEL), F32),
        compiler_params=_params(("parallel", "parallel")),
        name="ada_modulation",
    )(c_pad, w, b)
    return out[:, :bsz]


def _prenorm(x, nw, scale, shift):
    return (_rms(x, NORM_EPS) * nw) * (1.0 + scale) + shift


def _inproj_kernel(x_ref, nw_ref, sc_ref, sh_ref, w_ref, o_ref, h_ref):
    i = pl.program_id(0)
    j = pl.program_id(1)
    last = pl.num_programs(1) - 1
    slot = i % 2

    def normalised():
        return _prenorm(x_ref[...], nw_ref[...], sc_ref[0], sh_ref[0]).astype(BF16)

    @pl.when((i == 0) & (j == 0))
    def _():
        h_ref[0] = normalised()

    @pl.when(j < last)
    def _():
        o_ref[...] = _mm(h_ref[slot], w_ref[0])

    @pl.when(j == last)
    def _():
        o_ref[...] = _mm(h_ref[slot], w_ref[0])
        h_ref[1 - slot] = normalised()


def _in_projection(x2, nw, scale, shift, w, layer, seq):
    t = x2.shape[0]
    tm, tn = 512, P_WIDTH // 2
    per_b = seq // tm
    n_i, n_j = t // tm, P_WIDTH // tn

    def x_tile(i, j):
        return jnp.minimum(jnp.where(j == n_j - 1, i + 1, i), n_i - 1)

    return pl.pallas_call(
        _inproj_kernel,
        grid=(n_i, n_j),
        in_specs=[
            pl.BlockSpec((tm, D_MODEL), lambda i, j: (x_tile(i, j), 0)),
            pl.BlockSpec((1, D_MODEL), lambda i, j: (0, 0)),
            pl.BlockSpec((1, 1, D_MODEL), lambda i, j: (x_tile(i, j) // per_b, 0, 0)),
            pl.BlockSpec((1, 1, D_MODEL), lambda i, j: (x_tile(i, j) // per_b, 0, 0)),
            pl.BlockSpec((1, D_MODEL, tn), lambda i, j: (layer, 0, j)),
        ],
        out_specs=pl.BlockSpec((tm, tn), lambda i, j: (i, j)),
        out_shape=jax.ShapeDtypeStruct((t, P_WIDTH), F32),
        scratch_shapes=[pltpu.VMEM((2, tm, D_MODEL), BF16)],
        compiler_params=_params(("arbitrary", "arbitrary")),
        name="in_projection",
    )(x2, nw, scale, shift, w)


def _alibi_slope(h):
    return 2.0 ** (-8.0 * (h + 1) / N_HEADS)


def _attn_kernel(sink_ref, q_ref, cur_ref, prev_ref, o_ref):
    n = pl.program_id(1)
    blk = ATTN_BLOCK
    nblk = q_ref.shape[0] // blk
    q = q_ref[...]
    kv = jnp.concatenate([prev_ref[...], cur_ref[...]], axis=0).astype(BF16)
    qi = lax.broadcasted_iota(jnp.int32, (blk, 2 * blk), 0) + blk
    kj = lax.broadcasted_iota(jnp.int32, (blk, 2 * blk), 1)
    dist = qi - kj
    window = (dist >= 0) & (dist < blk)
    first = window & ((kj >= blk) | (n > 0))
    distf = dist.astype(F32)
    units = [(i, h) for i in range(nblk) for h in range(N_HEADS)]
    kg = {(i, g): kv[i * blk:(i + 2) * blk, HEAD * g:HEAD * (g + 1)]
          for i in range(nblk) for g in range(ATTN_KV_HEADS)}
    vg = {(i, g): kv[i * blk:(i + 2) * blk, 128 + HEAD * g:128 + HEAD * (g + 1)]
          for i in range(nblk) for g in range(ATTN_KV_HEADS)}
    s = {(i, h): _nt(q[i * blk:(i + 1) * blk, HEAD * h:HEAD * (h + 1)].astype(BF16), kg[i, h // ATTN_GROUP])
         for i, h in units}
    s = {(i, h): jnp.where(first if i == 0 else window,
                           s[i, h] * (HEAD ** -0.5) - _alibi_slope(h) * distf, -jnp.inf) for i, h in units}
    m = {(i, h): jnp.maximum(jnp.max(s[i, h], axis=-1, keepdims=True), sink_ref[0, h]) for i, h in units}
    p = {u: jnp.exp(s[u] - m[u]) for u in units}
    denom = {(i, h): jnp.sum(p[i, h], axis=-1, keepdims=True) + jnp.exp(sink_ref[0, h] - m[i, h]) for i, h in units}
    out = {(i, h): _mm(p[i, h].astype(BF16), vg[i, h // ATTN_GROUP]) / denom[i, h] for i, h in units}
    rows = [jnp.concatenate([out[i, h] for h in range(N_HEADS)], axis=-1) for i in range(nblk)]
    o_ref[...] = jnp.concatenate(rows, axis=0).astype(o_ref.dtype)


def _attention(p, sinks, bsz, seq):
    rows = ATTN_BLOCK * ATTN_BLOCKS_PER_STEP
    steps = seq // rows
    nb = seq // ATTN_BLOCK
    q_col = SEC_Q // GW
    kv_col = SEC_KV // 256
    return pl.pallas_call(
        _attn_kernel,
        grid=(bsz, steps),
        in_specs=[
            pl.BlockSpec(memory_space=pltpu.SMEM),
            pl.BlockSpec((rows, GW), lambda b, n: (b * steps + n, q_col)),
            pl.BlockSpec((rows, 256), lambda b, n: (b * steps + n, kv_col)),
            pl.BlockSpec((ATTN_BLOCK, 256),
                         lambda b, n: (b * nb + jnp.maximum(n * ATTN_BLOCKS_PER_STEP - 1, 0), kv_col)),
        ],
        out_specs=pl.BlockSpec((rows, GW), lambda b, n: (b * steps + n, 0)),
        out_shape=jax.ShapeDtypeStruct((bsz * seq, GW), BF16),
        compiler_params=_params(("parallel", "arbitrary")),
        name="swa_attention",
    )(sinks, p, p, p)


def _rwkv_kernel(p_ref, mu_ref, w0_ref, wup_ref, a0_ref, aup_ref, gup_ref, kk_ref, ka_ref,
                 rk_ref, lnw_ref, lnb_ref, hsum_ref, o_ref, last_ref, state_ref):
    c = RWKV_CHUNK
    nseq = p_ref.shape[0]
    rows = nseq * c

    @pl.when(pl.program_id(1) == 0)
    def _():
        last_ref[...] = jnp.zeros_like(last_ref)
        state_ref[...] = jnp.zeros_like(state_ref)

    p = p_ref[...].reshape(rows, RWKV_IN)
    row = lax.broadcasted_iota(jnp.int32, (rows, 1), 0)
    shifted = pltpu.roll(p, 1, axis=0)
    for j in range(nseq):
        shifted = jnp.where(row == j * c, last_ref[j:j + 1, :], shifted)
        last_ref[j:j + 1, :] = p[(j + 1) * c - 1:(j + 1) * c, :]
    pm = p + mu_ref[...] * (shifted - p)

    r = pm[:, 0:GW]
    k = pm[:, GW:2 * GW]
    v = pm[:, 2 * GW:3 * GW]
    wa = pm[:, 3 * GW:3 * GW + 128]
    gd = pm[:, 3 * GW + 128:3 * GW + 256]

    w = -_softplus(-(w0_ref[...] + _mm(jnp.tanh(wa).astype(BF16), wup_ref[...]))) - 0.5
    logd = -jnp.exp(w)
    a = jax.nn.sigmoid(a0_ref[...] + _mm(wa.astype(BF16), aup_ref[...]))
    g = _mm(jax.nn.sigmoid(gd).astype(BF16), gup_ref[...])

    kk = k * kk_ref[...]
    sumsq = _split_mm_rhs(kk * kk, hsum_ref[...], 2)
    kk = kk / jnp.maximum(jnp.sqrt(sumsq), 1e-12)
    k = k * (1.0 + (a - 1.0) * ka_ref[...])
    b = kk * a

    rr = lax.broadcasted_iota(jnp.int32, (rows, rows), 0)
    cr = lax.broadcasted_iota(jnp.int32, (rows, rows), 1)
    tril_seq = ((rr >= cr) & (rr // c == cr // c)).astype(BF16)
    cum = _split_mm(tril_seq, logd, 3)
    g_inv = jnp.exp(-cum)
    at16 = ((-kk) * jnp.exp(cum - logd)).astype(BF16)
    bt = (b * g_inv).astype(BF16)
    kt = (k * g_inv).astype(BF16)
    rt = (r * jnp.exp(cum)).astype(BF16)
    v16 = v.astype(BF16)
    rk = r * k * rk_ref[...]

    ri = lax.broadcasted_iota(jnp.int32, (c, c), 0)
    ci = lax.broadcasted_iota(jnp.int32, (c, c), 1)
    lower = ri > ci
    eye = (ri == ci).astype(F32)
    ri2 = lax.broadcasted_iota(jnp.int32, (c, 2 * c), 0)
    ci2 = lax.broadcasted_iota(jnp.int32, (c, 2 * c), 1)
    k_half = ci2 >= c
    cj2 = jnp.where(k_half, ci2 - c, ci2)
    strict_k = k_half & (ri2 > cj2)
    lower_eq2 = ri2 >= cj2
    levels = []
    size = 1
    while size < c:
        levels.append((ri // (2 * size) == ci // (2 * size)) & (ri // size > ci // size))
        size *= 2

    units = [(j, h) for j in range(nseq) for h in range(N_HEADS)]
    rs = {u: slice(u[0] * c, (u[0] + 1) * c) for u in units}
    ls = {u: slice(HEAD * u[1], HEAD * (u[1] + 1)) for u in units}
    v_h = {u: v16[rs[u], ls[u]] for u in units}
    ar = {u: jnp.concatenate([at16[rs[u], ls[u]], rt[rs[u], ls[u]]], axis=0) for u in units}
    bk = {u: jnp.concatenate([bt[rs[u], ls[u]], kt[rs[u], ls[u]]], axis=0) for u in units}
    s0 = {u: state_ref[u[0], u[1]] for u in units}
    gram = {u: _nt(ar[u], bk[u]) for u in units}
    l_ab = {u: jnp.where(lower, gram[u][:c, :c], 0.0).astype(BF16) for u in units}
    l_ak = {u: jnp.where(strict_k, gram[u][:c, :], 0.0).astype(BF16) for u in units}
    m_r = {u: jnp.where(lower_eq2, gram[u][c:, :], 0.0).astype(BF16) for u in units}
    inv = {u: eye + jnp.where(levels[0], l_ab[u].astype(F32), 0.0) for u in units}
    for lvl in levels[1:]:
        inv16 = {u: inv[u].astype(BF16) for u in units}
        wl = {u: _mm(jnp.where(lvl, l_ab[u], jnp.zeros_like(l_ab[u])), inv16[u]).astype(BF16) for u in units}
        inv = {u: inv[u] + _mm(inv16[u], wl[u]) for u in units}
    sx = {u: _nt(ar[u], s0[u].astype(BF16)) for u in units}
    x = {u: sx[u][:c] + _mm(l_ak[u], jnp.concatenate([v_h[u], v_h[u]], axis=0)) for u in units}
    us = {u: _mm(inv[u].astype(BF16), x[u].astype(BF16)).astype(BF16) for u in units}
    uv = {u: jnp.concatenate([us[u], v_h[u]], axis=0) for u in units}
    y = {u: sx[u][c:] + _mm(m_r[u], uv[u]) for u in units}
    for u in units:
        g_end = jnp.exp(cum[rs[u].stop - 1:rs[u].stop, ls[u]])
        state_ref[u[0], u[1]] = (s0[u] + _tn(uv[u], bk[u])) * g_end
    outs = []
    for j in range(nseq):
        heads = []
        for h in range(N_HEADS):
            u = (j, h)
            mean = jnp.mean(y[u], axis=-1, keepdims=True)
            yc = y[u] - mean
            var = jnp.mean(yc * yc, axis=-1, keepdims=True)
            bonus = jnp.sum(rk[rs[u], ls[u]], axis=-1, keepdims=True) * v[rs[u], ls[u]]
            heads.append(yc * lax.rsqrt(var + RWKV_LN_EPS) * lnw_ref[:, ls[u]] + lnb_ref[:, ls[u]] + bonus)
        outs.append(jnp.concatenate(heads, axis=-1))
    out = jnp.concatenate(outs, axis=0) * g
    o_ref[...] = out.reshape(nseq, c, GW).astype(o_ref.dtype)


def _rwkv(p, prm, bsz, seq):
    c = RWKV_CHUNK
    nc = seq // c
    nseq = math.gcd(bsz, RWKV_SEQS)
    col = SEC_R // SEC_W
    vec = lambda width: pl.BlockSpec((1, width), lambda b, n: (0, 0))
    mat = lambda rows: pl.BlockSpec((rows, GW), lambda b, n: (0, 0))
    out = pl.pallas_call(
        _rwkv_kernel,
        grid=(bsz // nseq, nc),
        in_specs=[
            pl.BlockSpec((nseq, c, RWKV_IN), lambda b, n: (b, n, col)),
            vec(RWKV_IN), vec(GW), mat(128), vec(GW), mat(128), mat(128),
            vec(GW), vec(GW), vec(GW), vec(GW), vec(GW),
            pl.BlockSpec((GW, GW), lambda b, n: (0, 0)),
        ],
        out_specs=pl.BlockSpec((nseq, c, GW), lambda b, n: (b, n, 0)),
        out_shape=jax.ShapeDtypeStruct((bsz, seq, GW), BF16),
        scratch_shapes=[pltpu.VMEM((8, RWKV_IN), F32), pltpu.VMEM((nseq, N_HEADS, HEAD, HEAD), F32)],
        compiler_params=_params(("parallel", "arbitrary")),
        name="rwkv7_chunked",
    )(p.reshape(bsz, seq, P_WIDTH), *prm)
    return out.reshape(bsz * seq, GW)


def _rwkv_params(mu, w0, w_up, a0, a_up, g_up, k_k, k_a, r_k, ln_w, ln_b):
    zeros = jnp.zeros((64, GW), F32)
    wup = jnp.concatenate([w_up, zeros], axis=0).astype(BF16)
    aup = jnp.concatenate([zeros, a_up], axis=0).astype(BF16)
    head = jnp.arange(GW) // HEAD
    hsum = (head[:, None] == head[None, :]).astype(BF16)
    row = lambda t: t.reshape(1, -1)
    return (row(mu), row(w0), wup, row(a0), aup, g_up.astype(BF16), row(k_k), row(k_a),
            row(r_k), row(ln_w), row(ln_b), hsum)


def _ssd_kernel(p_ref, cw_ref, cb_ref, dtb_ref, a_ref, dsk_ref, nw_ref, o_ref, tail_ref, state_ref):
    @pl.when(pl.program_id(1) == 0)
    def _():
        tail_ref[...] = jnp.zeros_like(tail_ref)
        state_ref[...] = jnp.zeros_like(state_ref)

    for j in range(p_ref.shape[0]):
        _ssd_chunk(p_ref.at[j], cw_ref, cb_ref, dtb_ref, a_ref, dsk_ref, nw_ref, o_ref.at[j],
                   tail_ref.at[j], state_ref.at[j])


def _ssd_chunk(p_ref, cw_ref, cb_ref, dtb_ref, a_ref, dsk_ref, nw_ref, o_ref, tail_ref, state_ref):
    c = SSD_CHUNK
    blk = p_ref[...]
    z = blk[:, :GW]
    raw = blk[:, GW:GW + SSD_CONV_CH]
    ext = jnp.concatenate([tail_ref[...], raw], axis=0)
    tail_ref[...] = raw[c - 8:c, :]
    conv = cb_ref[...]
    for i in range(SSD_CONV):
        off = 8 - (SSD_CONV - 1) + i
        conv = conv + cw_ref[i:i + 1, :] * ext[off:off + c, :]
    xbc = _silu(conv)
    x = xbc[:, :GW]
    bm = xbc[:, GW:GW + 256].astype(BF16)
    cm = xbc[:, GW + 256:GW + 512].astype(BF16)

    dt = _softplus(blk[:, GW + SSD_CONV_CH:GW + SSD_CONV_CH + 128] + dtb_ref[...])
    da = dt * a_ref[...]
    ri = lax.broadcasted_iota(jnp.int32, (c, c), 0)
    ci = lax.broadcasted_iota(jnp.int32, (c, c), 1)
    lower_eq = ri >= ci
    cs = _split_mm(lower_eq.astype(BF16), da, 3)
    cs_t = cs.T
    cs_end = cs[c - 1:c, :]
    dec_in = jnp.exp(cs)
    dec_out = jnp.exp(cs_end - cs)
    dec_all = jnp.exp(cs_end)

    heads = range(N_HEADS)
    groups = range(N_HEADS // 4)
    sls = [slice(HEAD * h, HEAD * (h + 1)) for h in heads]
    bg = [bm[:, SSD_STATE * g:SSD_STATE * (g + 1)] for g in groups]
    cg = [cm[:, SSD_STATE * g:SSD_STATE * (g + 1)] for g in groups]
    cb = [_nt(cg[g], bg[g]) for g in groups]
    s0 = [state_ref[h] for h in heads]
    lmat = [jnp.exp(jnp.where(lower_eq, cs[:, h:h + 1] - cs_t[h:h + 1, :], -jnp.inf)) for h in heads]
    xdt = [x[:, sls[h]] * dt[:, h:h + 1] for h in heads]
    y_in = [_mm((cb[h // 4] * lmat[h]).astype(BF16), xdt[h].astype(BF16)) for h in heads]
    y_st = [_nt(cg[h // 4], s0[h].astype(BF16)) * dec_in[:, h:h + 1] for h in heads]
    new = [_tn((xdt[h] * dec_out[:, h:h + 1]).astype(BF16), bg[h // 4]) for h in heads]
    for h in heads:
        state_ref[h] = s0[h] * dec_all[:, h:h + 1] + new[h]
    outs = [y_in[h] + y_st[h] + x[:, sls[h]] * dsk_ref[:, sls[h]] for h in heads]
    y = jnp.concatenate(outs, axis=-1) * _silu(z)
    half = GW // 2
    y = jnp.concatenate([_rms(y[:, :half], SSD_NORM_EPS), _rms(y[:, half:], SSD_NORM_EPS)], axis=-1)
    o_ref[...] = (y * nw_ref[...]).astype(o_ref.dtype)


def _ssd(p, prm, bsz, seq):
    c = SSD_CHUNK
    nc = seq // c
    nseq = math.gcd(bsz, SSD_SEQS)
    col = SEC_S // SEC_W
    vec = lambda width: pl.BlockSpec((1, width), lambda b, n: (0, 0))
    out = pl.pallas_call(
        _ssd_kernel,
        grid=(bsz // nseq, nc),
        in_specs=[
            pl.BlockSpec((nseq, c, SEC_W), lambda b, n: (b, n, col)),
            pl.BlockSpec((SSD_CONV, SSD_CONV_CH), lambda b, n: (0, 0)),
            vec(SSD_CONV_CH), vec(128), vec(128), vec(GW), vec(GW),
        ],
        out_specs=pl.BlockSpec((nseq, c, GW), lambda b, n: (b, n, 0)),
        out_shape=jax.ShapeDtypeStruct((bsz, seq, GW), BF16),
        scratch_shapes=[pltpu.VMEM((nseq, 8, SSD_CONV_CH), F32),
                        pltpu.VMEM((nseq, N_HEADS, HEAD, SSD_STATE), F32)],
        compiler_params=_params(("parallel", "arbitrary")),
        name="mamba2_ssd",
    )(p.reshape(bsz, seq, P_WIDTH), *prm)
    return out.reshape(bsz * seq, GW)


def _ssd_params(conv_w, conv_b, dt_bias, a_log, d_skip, norm_w):
    pad = lambda t: jnp.zeros((1, 128), F32).at[0, :N_HEADS].set(t)
    return (conv_w, conv_b.reshape(1, -1), pad(dt_bias), pad(-jnp.exp(a_log)),
            jnp.repeat(d_skip, HEAD).reshape(1, -1), norm_w.reshape(1, -1))


def _gelu_tanh(x):
    return 0.5 * x * (1.0 + jnp.tanh(math.sqrt(2.0 / math.pi) * (x + 0.044715 * (x * x * x))))


def _s5_kernel(u_ref, bbr_ref, bbi_ref, lvr_ref, lvi_ref, pwr_ref, pwi_ref, ccr_ref, cci_ref,
               dsk_ref, gw_ref, gb_ref, o_ref, hr_ref, hi_ref, carry_ref):
    @pl.when(pl.program_id(1) == 0)
    def _():
        carry_ref[...] = jnp.zeros_like(carry_ref)

    u = u_ref[...]
    u16 = u.astype(BF16)
    q_in, q_st = GW // S5_QUADS, S5_LANES // S5_QUADS
    for q in range(S5_QUADS):
        uq = u16[:, q * q_in:(q + 1) * q_in]
        hr_ref[:, q * q_st:(q + 1) * q_st] = _mm(uq, bbr_ref[q])
        hi_ref[:, q * q_st:(q + 1) * q_st] = _mm(uq, bbi_ref[q])

    def tile(t, carry):
        cr, ci = carry
        rows = pl.ds(pl.multiple_of(t * S5_TILE, S5_TILE), S5_TILE)
        xr = hr_ref[rows, :]
        xi = hi_ref[rows, :]
        for lvl, s in enumerate((1, 2, 4)):
            lr = lvr_ref[lvl]
            li = lvi_ref[lvl]
            sr = pltpu.roll(xr, s, axis=0)
            si = pltpu.roll(xi, s, axis=0)
            xr, xi = xr + lr * sr - li * si, xi + lr * si + li * sr
        pr = pwr_ref[...]
        pi = pwi_ref[...]
        xr, xi = xr + pr * cr - pi * ci, xi + pr * ci + pi * cr
        hr_ref[rows, :] = xr
        hi_ref[rows, :] = xi
        last = S5_TILE - 1
        return (jnp.broadcast_to(xr[last:last + 1, :], xr.shape), jnp.broadcast_to(xi[last:last + 1, :], xi.shape))

    cr, ci = lax.fori_loop(0, u.shape[0] // S5_TILE, tile, (carry_ref[0], carry_ref[1]))
    carry_ref[0] = cr
    carry_ref[1] = ci

    y = jnp.concatenate(
        [_mm(hr_ref[:, q * q_st:(q + 1) * q_st].astype(BF16), ccr_ref[q])
         - _mm(hi_ref[:, q * q_st:(q + 1) * q_st].astype(BF16), cci_ref[q]) for q in range(S5_QUADS)], axis=-1)
    y = _gelu_tanh(y + dsk_ref[...] * u)
    gate = jax.nn.sigmoid(_mm(y.astype(BF16), gw_ref[...]) + gb_ref[...])
    o_ref[...] = (y * gate).astype(o_ref.dtype)


def _s5(p, prm, bsz, seq):
    c = S5_CHUNK
    nc = seq // c
    col = SEC_F // GW
    full = lambda shape: pl.BlockSpec(shape, lambda b, n: (0,) * len(shape))
    return pl.pallas_call(
        _s5_kernel,
        grid=(bsz, nc),
        in_specs=[
            pl.BlockSpec((c, GW), lambda b, n: (b * nc + n, col)),
            full((S5_QUADS, GW // S5_QUADS, S5_LANES // S5_QUADS)),
            full((S5_QUADS, GW // S5_QUADS, S5_LANES // S5_QUADS)),
            full((3, S5_TILE, S5_LANES)), full((3, S5_TILE, S5_LANES)),
            full((S5_TILE, S5_LANES)), full((S5_TILE, S5_LANES)),
            full((S5_QUADS, S5_LANES // S5_QUADS, GW // S5_QUADS)),
            full((S5_QUADS, S5_LANES // S5_QUADS, GW // S5_QUADS)),
            full((1, GW)), full((GW, GW)), full((1, GW)),
        ],
        out_specs=pl.BlockSpec((c, GW), lambda b, n: (b * nc + n, 0)),
        out_shape=jax.ShapeDtypeStruct((bsz * seq, GW), BF16),
        scratch_shapes=[pltpu.VMEM((c, S5_LANES), F32), pltpu.VMEM((c, S5_LANES), F32),
                        pltpu.VMEM((2, S5_TILE, S5_LANES), F32)],
        compiler_params=_params(("parallel", "arbitrary")),
        name="s5_scan",
    )(p, *prm)


def _s5_params(lam_re, lam_im, log_step, b_re, b_im, c_re, c_im, d_skip, glu_w, glu_b):
    step = jnp.exp(log_step)[:, None]
    mag = jnp.exp(lam_re * step)
    ab_re, ab_im = mag * jnp.cos(lam_im * step), mag * jnp.sin(lam_im * step)
    den = lam_re * lam_re + lam_im * lam_im
    coef_re = ((ab_re - 1) * lam_re + ab_im * lam_im) / den
    coef_im = (ab_im * lam_re - (ab_re - 1) * lam_im) / den
    bb_re = coef_re[..., None] * b_re - coef_im[..., None] * b_im
    bb_im = coef_re[..., None] * b_im + coef_im[..., None] * b_re
    gq = S5_GROUPS // S5_QUADS
    eye = jnp.eye(gq, dtype=F32)
    in_proj = lambda t: jnp.einsum(
        'qgni,gh->qgihn', t.reshape(S5_QUADS, gq, S5_STATE, S5_GROUP_CH), eye
    ).reshape(S5_QUADS, gq * S5_GROUP_CH, gq * S5_STATE).astype(BF16)
    out_proj = lambda t: jnp.einsum(
        'qgin,gh->qgnhi', t.reshape(S5_QUADS, gq, S5_GROUP_CH, S5_STATE), eye
    ).reshape(S5_QUADS, gq * S5_STATE, gq * S5_GROUP_CH).astype(BF16)

    def cmul(a, b):
        return a[0] * b[0] - a[1] * b[1], a[0] * b[1] + a[1] * b[0]

    lam1 = (ab_re.reshape(-1), ab_im.reshape(-1))
    lam2 = cmul(lam1, lam1)
    lam4 = cmul(lam2, lam2)
    rows = jnp.arange(S5_TILE)[:, None]
    lvl_re = jnp.stack([jnp.where(rows >= s, l[0][None, :], 0.0) for s, l in ((1, lam1), (2, lam2), (4, lam4))])
    lvl_im = jnp.stack([jnp.where(rows >= s, l[1][None, :], 0.0) for s, l in ((1, lam1), (2, lam2), (4, lam4))])
    powers = [lam1]
    for _ in range(S5_TILE - 1):
        powers.append(cmul(powers[-1], lam1))
    pw_re = jnp.stack([q[0] for q in powers])
    pw_im = jnp.stack([q[1] for q in powers])
    return (in_proj(bb_re), in_proj(bb_im), lvl_re, lvl_im, pw_re, pw_im, out_proj(c_re), out_proj(c_im),
            d_skip.reshape(1, -1), glu_w.astype(BF16), glu_b.reshape(1, -1))


def _outproj_kernel(ya_ref, yr_ref, ys_ref, yf_ref, w_ref, x_ref, g_ref, nw_ref, o_ref):
    acc = _mm(ya_ref[...], w_ref[0, 0:GW, :])
    acc = acc + _mm(yr_ref[...], w_ref[0, GW:2 * GW, :])
    acc = acc + _mm(ys_ref[...], w_ref[0, 2 * GW:3 * GW, :])
    acc = acc + _mm(yf_ref[...], w_ref[0, 3 * GW:4 * GW, :])
    o_ref[...] = x_ref[...] + g_ref[0] * (_rms(acc, NORM_EPS) * nw_ref[...])


def _out_projection(ys, w, layer, x2, gate, nw, seq):
    t = x2.shape[0]
    tm = 512
    per_b = seq // tm
    ymap = pl.BlockSpec((tm, GW), lambda i: (i, 0))
    return pl.pallas_call(
        _outproj_kernel,
        grid=(t // tm,),
        in_specs=[
            ymap, ymap, ymap, ymap,
            pl.BlockSpec((1, D_MODEL, D_MODEL), lambda i: (layer, 0, 0)),
            pl.BlockSpec((tm, D_MODEL), lambda i: (i, 0)),
            pl.BlockSpec((1, 1, D_MODEL), lambda i: (i // per_b, 0, 0)),
            pl.BlockSpec((1, D_MODEL), lambda i: (0, 0)),
        ],
        out_specs=pl.BlockSpec((tm, D_MODEL), lambda i: (i, 0)),
        out_shape=jax.ShapeDtypeStruct((t, D_MODEL), F32),
        compiler_params=_params(("parallel",)),
        name="out_projection",
    )(*ys, w, x2, gate, nw)


def _ffn_kernel(x_ref, npre_ref, sc_ref, sh_ref, wg_ref, wu_ref, wd_ref, g_ref, npost_ref, o_ref,
                h_ref, acc_ref):
    f = pl.program_id(1)

    @pl.when(f == 0)
    def _():
        h_ref[...] = _prenorm(x_ref[...], npre_ref[...], sc_ref[0], sh_ref[0]).astype(BF16)
        acc_ref[...] = jnp.zeros_like(acc_ref)

    h = h_ref[...]
    act = _silu(_mm(h, wg_ref[0])) * _mm(h, wu_ref[0])
    acc_ref[...] += _mm(act.astype(BF16), wd_ref[0])

    @pl.when(f == pl.num_programs(1) - 1)
    def _():
        o_ref[...] = x_ref[...] + g_ref[0] * (_rms(acc_ref[...], NORM_EPS) * npost_ref[...])


def _dense_ffn(x2, npre, scale, shift, wg, wu, wd, layer, gate, npost, seq):
    t = x2.shape[0]
    tm, tf = 512, 512
    per_b = seq // tm
    mod = pl.BlockSpec((1, 1, D_MODEL), lambda i, f: (i // per_b, 0, 0))
    vec = pl.BlockSpec((1, D_MODEL), lambda i, f: (0, 0))
    return pl.pallas_call(
        _ffn_kernel,
        grid=(t // tm, D_FF // tf),
        in_specs=[
            pl.BlockSpec((tm, D_MODEL), lambda i, f: (i, 0)),
            vec, mod, mod,
            pl.BlockSpec((1, D_MODEL, tf), lambda i, f: (layer, 0, f)),
            pl.BlockSpec((1, D_MODEL, tf), lambda i, f: (layer, 0, f)),
            pl.BlockSpec((1, tf, D_MODEL), lambda i, f: (layer, f, 0)),
            mod, vec,
        ],
        out_specs=pl.BlockSpec((tm, D_MODEL), lambda i, f: (i, 0)),
        out_shape=jax.ShapeDtypeStruct((t, D_MODEL), F32),
        scratch_shapes=[pltpu.VMEM((tm, D_MODEL), BF16), pltpu.VMEM((tm, D_MODEL), F32)],
        compiler_params=_params(("parallel", "arbitrary")),
        name="dense_swiglu",
    )(x2, npre, scale, shift, wg, wu, wd, gate, npost)


def _router_kernel(x_ref, npre_ref, sc_ref, sh_ref, rw_ref, rb_ref, h_ref, lg_ref):
    h = _prenorm(x_ref[...], npre_ref[...], sc_ref[0], sh_ref[0])
    h_ref[...] = h.astype(BF16)
    lg_ref[...] = _mm(h, rw_ref[...], precision=HIGHEST) + rb_ref[...]


def _router(x2, npre, scale, shift, rw, rb, seq):
    t = x2.shape[0]
    tm = 512
    per_b = seq // tm
    mod = pl.BlockSpec((1, 1, D_MODEL), lambda i: (i // per_b, 0, 0))
    return pl.pallas_call(
        _router_kernel,
        grid=(t // tm,),
        in_specs=[
            pl.BlockSpec((tm, D_MODEL), lambda i: (i, 0)),
            pl.BlockSpec((1, D_MODEL), lambda i: (0, 0)),
            mod, mod,
            pl.BlockSpec((D_MODEL, 128), lambda i: (0, 0)),
            pl.BlockSpec((1, 128), lambda i: (0, 0)),
        ],
        out_specs=[pl.BlockSpec((tm, D_MODEL), lambda i: (i, 0)), pl.BlockSpec((tm, 128), lambda i: (i, 0))],
        out_shape=[jax.ShapeDtypeStruct((t, D_MODEL), BF16), jax.ShapeDtypeStruct((t, 128), F32)],
        compiler_params=_params(("parallel",)),
        name="moe_router",
    )(x2, npre, scale, shift, rw, rb)


def _moe_ffn_kernel(be_ref, used_ref, xs_ref, wg_ref, wu_ref, wd_ref, o_ref, acc_ref):
    i = pl.program_id(0)
    f = pl.program_id(1)
    live = i < used_ref[0]

    @pl.when(f == 0)
    def _():
        acc_ref[...] = jnp.zeros_like(acc_ref)

    @pl.when(live)
    def _():
        xs = xs_ref[...]
        act = _silu(_mm(xs, wg_ref[0, 0])) * _mm(xs, wu_ref[0, 0])
        acc_ref[...] += _mm(act.astype(BF16), wd_ref[0, 0])

    @pl.when(f == pl.num_programs(1) - 1)
    def _():
        o_ref[...] = acc_ref[...].astype(o_ref.dtype)


def _moe_ffn(block_e, n_used, xs, wg, wu, wd, layer):
    rows = xs.shape[0]
    tf = D_EXPERT // 2
    nf = D_EXPERT // tf

    def tile(i, f, used):
        return jnp.where(i < used[0], f, nf - 1)

    grid_spec = pltpu.PrefetchScalarGridSpec(
        num_scalar_prefetch=2,
        grid=(rows // MOE_BLOCK, nf),
        in_specs=[
            pl.BlockSpec((MOE_BLOCK, D_MODEL), lambda i, f, be, used: (i, 0)),
            pl.BlockSpec((1, 1, D_MODEL, tf), lambda i, f, be, used: (layer, be[i], 0, tile(i, f, used))),
            pl.BlockSpec((1, 1, D_MODEL, tf), lambda i, f, be, used: (layer, be[i], 0, tile(i, f, used))),
            pl.BlockSpec((1, 1, tf, D_MODEL), lambda i, f, be, used: (layer, be[i], tile(i, f, used), 0)),
        ],
        out_specs=pl.BlockSpec((MOE_BLOCK, D_MODEL), lambda i, f, be, used: (i, 0)),
        scratch_shapes=[pltpu.VMEM((MOE_BLOCK, D_MODEL), F32)],
    )
    return pl.pallas_call(
        _moe_ffn_kernel,
        grid_spec=grid_spec,
        out_shape=jax.ShapeDtypeStruct((rows, D_MODEL), BF16),
        compiler_params=_params(("parallel", "arbitrary"), MOE_VMEM_LIMIT),
        name="moe_swiglu",
    )(block_e, n_used, xs, wg, wu, wd)


def _combine_kernel(y0_ref, y1_ref, p_ref, x_ref, g_ref, npost_ref, o_ref):
    p = p_ref[...]
    y = y0_ref[...].astype(F32) * p[:, 0:1] + y1_ref[...].astype(F32) * p[:, 1:2]
    o_ref[...] = x_ref[...] + g_ref[0] * (_rms(y, NORM_EPS) * npost_ref[...])


def _moe_combine(y0, y1, top_p, x2, gate, npost, seq):
    t = x2.shape[0]
    tm = 512
    per_b = seq // tm
    row = pl.BlockSpec((tm, D_MODEL), lambda i: (i, 0))
    return pl.pallas_call(
        _combine_kernel,
        grid=(t // tm,),
        in_specs=[row, row, pl.BlockSpec((tm, 128), lambda i: (i, 0)), row,
                  pl.BlockSpec((1, 1, D_MODEL), lambda i: (i // per_b, 0, 0)),
                  pl.BlockSpec((1, D_MODEL), lambda i: (0, 0))],
        out_specs=row,
        out_shape=jax.ShapeDtypeStruct((t, D_MODEL), F32),
        compiler_params=_params(("parallel",)),
        name="moe_combine",
    )(y0, y1, top_p, x2, gate, npost)


def _routed_ffn(x2, npre, scale, shift, rw, rb, wg, wu, wd, layer, gate, npost, seq):
    t = x2.shape[0]
    rw_pad = jnp.zeros((D_MODEL, 128), F32).at[:, :N_EXPERTS].set(rw)
    rb_pad = jnp.zeros((1, 128), F32).at[0, :N_EXPERTS].set(rb)
    h, logits = _router(x2, npre, scale, shift, rw_pad, rb_pad, seq)
    logits = logits[:, :N_EXPERTS]
    top_logit, top_idx = lax.top_k(logits, TOP_K)
    top_p = jax.nn.softmax(top_logit, axis=-1)
    n_assign = t * TOP_K
    flat_e = top_idx.reshape(-1).astype(jnp.int32)
    order = jnp.argsort(flat_e).astype(jnp.int32)
    rank = jnp.argsort(order).astype(jnp.int32)
    counts = jnp.sum((flat_e[:, None] == jnp.arange(N_EXPERTS, dtype=jnp.int32)[None, :]).astype(jnp.int32), axis=0)
    padded = (counts + MOE_BLOCK - 1) // MOE_BLOCK * MOE_BLOCK
    pad_end = jnp.cumsum(padded)
    pad_start = pad_end - padded
    start = jnp.cumsum(counts) - counts
    n_blocks = -(-n_assign // MOE_BLOCK) + N_EXPERTS
    rows = n_blocks * MOE_BLOCK
    block_e = jnp.minimum(jnp.searchsorted(pad_end, jnp.arange(n_blocks) * MOE_BLOCK, side='right'),
                          N_EXPERTS - 1).astype(jnp.int32)
    e_row = jnp.repeat(block_e, MOE_BLOCK)
    off = jnp.arange(rows, dtype=jnp.int32) - pad_start[e_row]
    src = order[jnp.clip(start[e_row] + off, 0, n_assign - 1)]
    row_tok = jnp.where(off < counts[e_row], src // TOP_K, t)
    h_pad = jnp.concatenate([h, jnp.zeros((1, D_MODEL), h.dtype)], axis=0)
    xs = h_pad[row_tok]
    n_used = (pad_end[-1:] // MOE_BLOCK).astype(jnp.int32)
    yb = _moe_ffn(block_e, n_used, xs, wg, wu, wd, layer)
    pos = (pad_start[flat_e] + rank - start[flat_e]).reshape(t, TOP_K)
    p_pad = jnp.zeros((t, 128), F32).at[:, :TOP_K].set(top_p)
    return _moe_combine(yb[pos[:, 0]], yb[pos[:, 1]], p_pad, x2, gate, npost, seq)


def _in_weights(w_in):
    a0, r0, s0, f0 = 0, ATTN_IN, ATTN_IN + RWKV_IN, ATTN_IN + RWKV_IN + SSD_IN
    zeros = jnp.zeros(w_in.shape[:-1] + (SEC_W - SSD_IN,), w_in.dtype)
    return jnp.concatenate([
        w_in[..., r0:s0],
        w_in[..., s0:f0], zeros,
        w_in[..., f0:f0 + GW],
        w_in[..., a0:r0],
    ], axis=-1).astype(BF16)


def kernel(x, c, ada_w, ada_b, norm_pre, norm_post, w_in, w_out, attn_sink, rwkv_mu, rwkv_w0, rwkv_w_up, rwkv_a0, rwkv_a_up, rwkv_g_up, rwkv_k_k, rwkv_k_a, rwkv_r_k, rwkv_ln_w, rwkv_ln_b, ssd_conv_w, ssd_conv_b, ssd_dt_bias, ssd_a_log, ssd_d, ssd_norm_w, s5_lam_re, s5_lam_im, s5_log_step, s5_b_re, s5_b_im, s5_c_re, s5_c_im, s5_d, s5_glu_w, s5_glu_b, ffn_w_gate, ffn_w_up, ffn_w_down, moe_router_w, moe_router_b, moe_w_gate, moe_w_up, moe_w_down):
    bsz, seq, d = x.shape
    depth = ada_w.shape[0]
    mod = _ada_modulation(c, ada_w, ada_b)
    mod = mod.reshape(depth, 2, bsz, 3, 1, d)
    x2 = x.reshape(bsz * seq, d)
    w_in16 = _in_weights(w_in)
    w_out16 = w_out.astype(BF16)
    ffn16 = (ffn_w_gate.astype(BF16), ffn_w_up.astype(BF16), ffn_w_down.astype(BF16))
    moe16 = (moe_w_gate.astype(BF16), moe_w_up.astype(BF16), moe_w_down.astype(BF16))
    for i in range(depth):
        shift, scale, gate = mod[i, 0, :, 0], mod[i, 0, :, 1], mod[i, 0, :, 2]
        p = _in_projection(x2, norm_pre[i, 0][None], scale, shift, w_in16, i, seq)
        y_attn = _attention(p, attn_sink[i][None], bsz, seq)
        y_rwkv = _rwkv(p, _rwkv_params(rwkv_mu[i], rwkv_w0[i], rwkv_w_up[i], rwkv_a0[i], rwkv_a_up[i],
                                       rwkv_g_up[i], rwkv_k_k[i], rwkv_k_a[i], rwkv_r_k[i].reshape(-1),
                                       rwkv_ln_w[i], rwkv_ln_b[i]), bsz, seq)
        y_ssd = _ssd(p, _ssd_params(ssd_conv_w[i], ssd_conv_b[i], ssd_dt_bias[i], ssd_a_log[i], ssd_d[i],
                                    ssd_norm_w[i]), bsz, seq)
        y_s5 = _s5(p, _s5_params(s5_lam_re[i], s5_lam_im[i], s5_log_step[i], s5_b_re[i], s5_b_im[i],
                                 s5_c_re[i], s5_c_im[i], s5_d[i], s5_glu_w[i], s5_glu_b[i]), bsz, seq)
        x2 = _out_projection((y_attn, y_rwkv, y_ssd, y_s5), w_out16, i, x2, gate, norm_post[i, 0][None], seq)
        shift, scale, gate = mod[i, 1, :, 0], mod[i, 1, :, 1], mod[i, 1, :, 2]
        j = i // 2
        if i % 2 == 0:
            x2 = _dense_ffn(x2, norm_pre[i, 1][None], scale, shift, *ffn16, j, gate, norm_post[i, 1][None], seq)
        else:
            x2 = _routed_ffn(x2, norm_pre[i, 1][None], scale, shift, moe_router_w[j], moe_router_b[j],
                             *moe16, j, gate, norm_post[i, 1][None], seq)
    return x2.reshape(bsz, seq, d)
```

```python
import functools
import math

import jax
import jax.numpy as jnp
from jax import lax
from jax.experimental import pallas as pl
from jax.experimental.pallas import tpu as pltpu

F32 = jnp.float32
BF16 = jnp.bfloat16
HIGHEST = lax.Precision.HIGHEST

D_MODEL = 2048
DEPTH = 4
GW = 512
NORM_EPS = 1e-6
HEAD = 64
N_HEADS = 8

ATTN_KV_HEADS = 2
ATTN_GROUP = 4
ATTN_BLOCK = 128
ATTN_IN = 768
ATTN_BLOCKS_PER_STEP = 4

RWKV_IN = 1792
RWKV_LN_EPS = 64e-5
RWKV_CHUNK = 64
RWKV_SEQS = 4

SSD_STATE = 128
SSD_CONV = 4
SSD_CHUNK = 128
SSD_SEQS = 4
SSD_CONV_CH = 1024
SSD_IN = 1544
SSD_NORM_EPS = 1e-5

S5_GROUPS = 32
S5_GROUP_CH = 16
S5_STATE = 64
S5_LANES = S5_GROUPS * S5_STATE
S5_CHUNK = 256
S5_TILE = 8
S5_QUADS = 4

D_FF = 5632
N_EXPERTS = 8
TOP_K = 2
D_EXPERT = 2816
MOE_BLOCK = 512

SEC_R = 0
SEC_S = 1792
SEC_F = 3584
SEC_Q = 4096
SEC_KV = 4608
P_WIDTH = 4864
SEC_W = 1792

VMEM_LIMIT = 52 * 1024 * 1024
MOE_VMEM_LIMIT = 58 * 1024 * 1024


def _params(sem, vmem=VMEM_LIMIT):
    return pltpu.CompilerParams(dimension_semantics=sem, vmem_limit_bytes=vmem)


def _nt(a, b, **kw):
    return lax.dot_general(a, b, (((1,), (1,)), ((), ())), preferred_element_type=F32, **kw)


def _tn(a, b, **kw):
    return lax.dot_general(a, b, (((0,), (0,)), ((), ())), preferred_element_type=F32, **kw)


def _mm(a, b, **kw):
    return jnp.dot(a, b, preferred_element_type=F32, **kw)


def _split_mm(a, x, parts):
    acc = None
    for _ in range(parts):
        piece = x.astype(BF16)
        term = _mm(a, piece)
        acc = term if acc is None else acc + term
        x = x - piece.astype(F32)
    return acc


def _split_mm_rhs(x, b, parts):
    acc = None
    for _ in range(parts):
        piece = x.astype(BF16)
        term = _mm(piece, b)
        acc = term if acc is None else acc + term
        x = x - piece.astype(F32)
    return acc


def _softplus(x):
    return jnp.maximum(x, 0.0) + jnp.log1p(jnp.exp(-jnp.abs(x)))


def _silu(x):
    return x * jax.nn.sigmoid(x)


def _rms(x, eps):
    return x * lax.rsqrt(jnp.mean(x * x, axis=-1, keepdims=True) + eps)


def _ada_kernel(c_ref, w_ref, b_ref, o_ref):
    c = c_ref[...]
    o_ref[0] = _mm(_silu(c).astype(BF16), w_ref[0].astype(BF16)) + b_ref[0]


def _ada_modulation(c, ada_w, ada_b):
    bsz = c.shape[0]
    rows = 8 * pl.cdiv(bsz, 8)
    c_pad = jnp.zeros((rows, D_MODEL), F32).at[:bsz].set(c)
    n_mod = ada_w.shape[0] * 2
    w = ada_w.reshape(n_mod, D_MODEL, 3 * D_MODEL)
    b = ada_b.reshape(n_mod, 1, 3 * D_MODEL)
    tn = 768
    out = pl.pallas_call(
        _ada_kernel,
        grid=(n_mod, 3 * D_MODEL // tn),
        in_specs=[
            pl.BlockSpec((rows, D_MODEL), lambda m, j: (0, 0)),
            pl.BlockSpec((1, D_MODEL, tn), lambda m, j: (m, 0, j)),
            pl.BlockSpec((1, 1, tn), lambda m, j: (m, 0, j)),
        ],
        out_specs=pl.BlockSpec((1, rows, tn), lambda m, j: (m, 0, j)),
        out_shape=jax.ShapeDtypeStruct((n_mod, rows, 3 * D_MODEL), F32),
        compiler_params=_params(("parallel", "parallel")),
        name="ada_modulation",
    )(c_pad, w, b)
    return out[:, :bsz]


def _prenorm(x, nw, scale, shift):
    return (_rms(x, NORM_EPS) * nw) * (1.0 + scale) + shift


def _inproj_kernel(x_ref, nw_ref, sc_ref, sh_ref, w_ref, o_ref, h_ref):
    i = pl.program_id(0)
    j = pl.program_id(1)
    last = pl.num_programs(1) - 1
    slot = i % 2

    def normalised():
        return _prenorm(x_ref[...], nw_ref[...], sc_ref[0], sh_ref[0]).astype(BF16)

    @pl.when((i == 0) & (j == 0))
    def _():
        h_ref[0] = normalised()

    @pl.when(j < last)
    def _():
        o_ref[...] = _mm(h_ref[slot], w_ref[0])

    @pl.when(j == last)
    def _():
        o_ref[...] = _mm(h_ref[slot], w_ref[0])
        h_ref[1 - slot] = normalised()


def _in_projection(x2, nw, scale, shift, w, layer, seq):
    t = x2.shape[0]
    tm, tn = 512, P_WIDTH // 2
    per_b = seq // tm
    n_i, n_j = t // tm, P_WIDTH // tn

    def x_tile(i, j):
        return jnp.minimum(jnp.where(j == n_j - 1, i + 1, i), n_i - 1)

    return pl.pallas_call(
        _inproj_kernel,
        grid=(n_i, n_j),
        in_specs=[
            pl.BlockSpec((tm, D_MODEL), lambda i, j: (x_tile(i, j), 0)),
            pl.BlockSpec((1, D_MODEL), lambda i, j: (0, 0)),
            pl.BlockSpec((1, 1, D_MODEL), lambda i, j: (x_tile(i, j) // per_b, 0, 0)),
            pl.BlockSpec((1, 1, D_MODEL), lambda i, j: (x_tile(i, j) // per_b, 0, 0)),
            pl.BlockSpec((1, D_MODEL, tn), lambda i, j: (layer, 0, j)),
        ],
        out_specs=pl.BlockSpec((tm, tn), lambda i, j: (i, j)),
        out_shape=jax.ShapeDtypeStruct((t, P_WIDTH), F32),
        scratch_shapes=[pltpu.VMEM((2, tm, D_MODEL), BF16)],
        compiler_params=_params(("arbitrary", "arbitrary")),
        name="in_projection",
    )(x2, nw, scale, shift, w)


def _alibi_slope(h):
    return 2.0 ** (-8.0 * (h + 1) / N_HEADS)


def _attn_kernel(sink_ref, q_ref, cur_ref, prev_ref, o_ref):
    n = pl.program_id(1)
    blk = ATTN_BLOCK
    nblk = q_ref.shape[0] // blk
    q = q_ref[...]
    kv = jnp.concatenate([prev_ref[...], cur_ref[...]], axis=0).astype(BF16)
    qi = lax.broadcasted_iota(jnp.int32, (blk, 2 * blk), 0) + blk
    kj = lax.broadcasted_iota(jnp.int32, (blk, 2 * blk), 1)
    dist = qi - kj
    window = (dist >= 0) & (dist < blk)
    first = window & ((kj >= blk) | (n > 0))
    distf = dist.astype(F32)
    units = [(i, h) for i in range(nblk) for h in range(N_HEADS)]
    kg = {(i, g): kv[i * blk:(i + 2) * blk, HEAD * g:HEAD * (g + 1)]
          for i in range(nblk) for g in range(ATTN_KV_HEADS)}
    vg = {(i, g): kv[i * blk:(i + 2) * blk, 128 + HEAD * g:128 + HEAD * (g + 1)]
          for i in range(nblk) for g in range(ATTN_KV_HEADS)}
    s = {(i, h): _nt(q[i * blk:(i + 1) * blk, HEAD * h:HEAD * (h + 1)].astype(BF16), kg[i, h // ATTN_GROUP])
         for i, h in units}
    s = {(i, h): jnp.where(first if i == 0 else window,
                           s[i, h] * (HEAD ** -0.5) - _alibi_slope(h) * distf, -jnp.inf) for i, h in units}
    m = {(i, h): jnp.maximum(jnp.max(s[i, h], axis=-1, keepdims=True), sink_ref[0, h]) for i, h in units}
    p = {u: jnp.exp(s[u] - m[u]) for u in units}
    denom = {(i, h): jnp.sum(p[i, h], axis=-1, keepdims=True) + jnp.exp(sink_ref[0, h] - m[i, h]) for i, h in units}
    out = {(i, h): _mm(p[i, h].astype(BF16), vg[i, h // ATTN_GROUP]) / denom[i, h] for i, h in units}
    rows = [jnp.concatenate([out[i, h] for h in range(N_HEADS)], axis=-1) for i in range(nblk)]
    o_ref[...] = jnp.concatenate(rows, axis=0).astype(o_ref.dtype)


def _attention(p, sinks, bsz, seq):
    rows = ATTN_BLOCK * ATTN_BLOCKS_PER_STEP
    steps = seq // rows
    nb = seq // ATTN_BLOCK
    q_col = SEC_Q // GW
    kv_col = SEC_KV // 256
    return pl.pallas_call(
        _attn_kernel,
        grid=(bsz, steps),
        in_specs=[
            pl.BlockSpec(memory_space=pltpu.SMEM),
            pl.BlockSpec((rows, GW), lambda b, n: (b * steps + n, q_col)),
            pl.BlockSpec((rows, 256), lambda b, n: (b * steps + n, kv_col)),
            pl.BlockSpec((ATTN_BLOCK, 256),
                         lambda b, n: (b * nb + jnp.maximum(n * ATTN_BLOCKS_PER_STEP - 1, 0), kv_col)),
        ],
        out_specs=pl.BlockSpec((rows, GW), lambda b, n: (b * steps + n, 0)),
        out_shape=jax.ShapeDtypeStruct((bsz * seq, GW), BF16),
        compiler_params=_params(("parallel", "arbitrary")),
        name="swa_attention",
    )(sinks, p, p, p)


def _rwkv_kernel(p_ref, mu_ref, w0_ref, wup_ref, a0_ref, aup_ref, gup_ref, kk_ref, ka_ref,
                 rk_ref, lnw_ref, lnb_ref, hsum_ref, o_ref, last_ref, state_ref):
    c = RWKV_CHUNK
    nseq = p_ref.shape[0]
    rows = nseq * c

    @pl.when(pl.program_id(1) == 0)
    def _():
        last_ref[...] = jnp.zeros_like(last_ref)
        state_ref[...] = jnp.zeros_like(state_ref)

    p = p_ref[...].reshape(rows, RWKV_IN)
    row = lax.broadcasted_iota(jnp.int32, (rows, 1), 0)
    shifted = pltpu.roll(p, 1, axis=0)
    for j in range(nseq):
        shifted = jnp.where(row == j * c, last_ref[j:j + 1, :], shifted)
        last_ref[j:j + 1, :] = p[(j + 1) * c - 1:(j + 1) * c, :]
    pm = p + mu_ref[...] * (shifted - p)

    r = pm[:, 0:GW]
    k = pm[:, GW:2 * GW]
    v = pm[:, 2 * GW:3 * GW]
    wa = pm[:, 3 * GW:3 * GW + 128]
    gd = pm[:, 3 * GW + 128:3 * GW + 256]

    w = -_softplus(-(w0_ref[...] + _mm(jnp.tanh(wa).astype(BF16), wup_ref[...]))) - 0.5
    logd = -jnp.exp(w)
    a = jax.nn.sigmoid(a0_ref[...] + _mm(wa.astype(BF16), aup_ref[...]))
    g = _mm(jax.nn.sigmoid(gd).astype(BF16), gup_ref[...])

    kk = k * kk_ref[...]
    sumsq = _split_mm_rhs(kk * kk, hsum_ref[...], 2)
    kk = kk / jnp.maximum(jnp.sqrt(sumsq), 1e-12)
    k = k * (1.0 + (a - 1.0) * ka_ref[...])
    b = kk * a

    rr = lax.broadcasted_iota(jnp.int32, (rows, rows), 0)
    cr = lax.broadcasted_iota(jnp.int32, (rows, rows), 1)
    tril_seq = ((rr >= cr) & (rr // c == cr // c)).astype(BF16)
    cum = _split_mm(tril_seq, logd, 3)
    g_inv = jnp.exp(-cum)
    at16 = ((-kk) * jnp.exp(cum - logd)).astype(BF16)
    bt = (b * g_inv).astype(BF16)
    kt = (k * g_inv).astype(BF16)
    rt = (r * jnp.exp(cum)).astype(BF16)
    v16 = v.astype(BF16)
    rk = r * k * rk_ref[...]

    ri = lax.broadcasted_iota(jnp.int32, (c, c), 0)
    ci = lax.broadcasted_iota(jnp.int32, (c, c), 1)
    lower = ri > ci
    eye = (ri == ci).astype(F32)
    ri2 = lax.broadcasted_iota(jnp.int32, (c, 2 * c), 0)
    ci2 = lax.broadcasted_iota(jnp.int32, (c, 2 * c), 1)
    k_half = ci2 >= c
    cj2 = jnp.where(k_half, ci2 - c, ci2)
    strict_k = k_half & (ri2 > cj2)
    lower_eq2 = ri2 >= cj2
    levels = []
    size = 1
    while size < c:
        levels.append((ri // (2 * size) == ci // (2 * size)) & (ri // size > ci // size))
        size *= 2

    units = [(j, h) for j in range(nseq) for h in range(N_HEADS)]
    rs = {u: slice(u[0] * c, (u[0] + 1) * c) for u in units}
    ls = {u: slice(HEAD * u[1], HEAD * (u[1] + 1)) for u in units}
    v_h = {u: v16[rs[u], ls[u]] for u in units}
    ar = {u: jnp.concatenate([at16[rs[u], ls[u]], rt[rs[u], ls[u]]], axis=0) for u in units}
    bk = {u: jnp.concatenate([bt[rs[u], ls[u]], kt[rs[u], ls[u]]], axis=0) for u in units}
    s0 = {u: state_ref[u[0], u[1]] for u in units}
    gram = {u: _nt(ar[u], bk[u]) for u in units}
    l_ab = {u: jnp.where(lower, gram[u][:c, :c], 0.0).astype(BF16) for u in units}
    l_ak = {u: jnp.where(strict_k, gram[u][:c, :], 0.0).astype(BF16) for u in units}
    m_r = {u: jnp.where(lower_eq2, gram[u][c:, :], 0.0).astype(BF16) for u in units}
    inv = {u: eye + jnp.where(levels[0], l_ab[u].astype(F32), 0.0) for u in units}
    for lvl in levels[1:]:
        inv16 = {u: inv[u].astype(BF16) for u in units}
        wl = {u: _mm(jnp.where(lvl, l_ab[u], jnp.zeros_like(l_ab[u])), inv16[u]).astype(BF16) for u in units}
        inv = {u: inv[u] + _mm(inv16[u], wl[u]) for u in units}
    sx = {u: _nt(ar[u], s0[u].astype(BF16)) for u in units}
    x = {u: sx[u][:c] + _mm(l_ak[u], jnp.concatenate([v_h[u], v_h[u]], axis=0)) for u in units}
    us = {u: _mm(inv[u].astype(BF16), x[u].astype(BF16)).astype(BF16) for u in units}
    uv = {u: jnp.concatenate([us[u], v_h[u]], axis=0) for u in units}
    y = {u: sx[u][c:] + _mm(m_r[u], uv[u]) for u in units}
    for u in units:
        g_end = jnp.exp(cum[rs[u].stop - 1:rs[u].stop, ls[u]])
        state_ref[u[0], u[1]] = (s0[u] + _tn(uv[u], bk[u])) * g_end
    outs = []
    for j in range(nseq):
        heads = []
        for h in range(N_HEADS):
            u = (j, h)
            mean = jnp.mean(y[u], axis=-1, keepdims=True)
            yc = y[u] - mean
            var = jnp.mean(yc * yc, axis=-1, keepdims=True)
            bonus = jnp.sum(rk[rs[u], ls[u]], axis=-1, keepdims=True) * v[rs[u], ls[u]]
            heads.append(yc * lax.rsqrt(var + RWKV_LN_EPS) * lnw_ref[:, ls[u]] + lnb_ref[:, ls[u]] + bonus)
        outs.append(jnp.concatenate(heads, axis=-1))
    out = jnp.concatenate(outs, axis=0) * g
    o_ref[...] = out.reshape(nseq, c, GW).astype(o_ref.dtype)


def _rwkv(p, prm, bsz, seq):
    c = RWKV_CHUNK
    nc = seq // c
    nseq = math.gcd(bsz, RWKV_SEQS)
    col = SEC_R // SEC_W
    vec = lambda width: pl.BlockSpec((1, width), lambda b, n: (0, 0))
    mat = lambda rows: pl.BlockSpec((rows, GW), lambda b, n: (0, 0))
    out = pl.pallas_call(
        _rwkv_kernel,
        grid=(bsz // nseq, nc),
        in_specs=[
            pl.BlockSpec((nseq, c, RWKV_IN), lambda b, n: (b, n, col)),
            vec(RWKV_IN), vec(GW), mat(128), vec(GW), mat(128), mat(128),
            vec(GW), vec(GW), vec(GW), vec(GW), vec(GW),
            pl.BlockSpec((GW, GW), lambda b, n: (0, 0)),
        ],
        out_specs=pl.BlockSpec((nseq, c, GW), lambda b, n: (b, n, 0)),
        out_shape=jax.ShapeDtypeStruct((bsz, seq, GW), BF16),
        scratch_shapes=[pltpu.VMEM((8, RWKV_IN), F32), pltpu.VMEM((nseq, N_HEADS, HEAD, HEAD), F32)],
        compiler_params=_params(("parallel", "arbitrary")),
        name="rwkv7_chunked",
    )(p.reshape(bsz, seq, P_WIDTH), *prm)
    return out.reshape(bsz * seq, GW)


def _rwkv_params(mu, w0, w_up, a0, a_up, g_up, k_k, k_a, r_k, ln_w, ln_b):
    zeros = jnp.zeros((64, GW), F32)
    wup = jnp.concatenate([w_up, zeros], axis=0).astype(BF16)
    aup = jnp.concatenate([zeros, a_up], axis=0).astype(BF16)
    head = jnp.arange(GW) // HEAD
    hsum = (head[:, None] == head[None, :]).astype(BF16)
    row = lambda t: t.reshape(1, -1)
    return (row(mu), row(w0), wup, row(a0), aup, g_up.astype(BF16), row(k_k), row(k_a),
            row(r_k), row(ln_w), row(ln_b), hsum)


def _ssd_kernel(p_ref, cw_ref, cb_ref, dtb_ref, a_ref, dsk_ref, nw_ref, exp_ref, o_ref, tail_ref, state_ref):
    @pl.when(pl.program_id(1) == 0)
    def _():
        tail_ref[...] = jnp.zeros_like(tail_ref)
        state_ref[...] = jnp.zeros_like(state_ref)

    for j in range(p_ref.shape[0]):
        _ssd_chunk(p_ref.at[j], cw_ref, cb_ref, dtb_ref, a_ref, dsk_ref, nw_ref, exp_ref, o_ref.at[j],
                   tail_ref.at[j], state_ref.at[j])


def _ssd_chunk(p_ref, cw_ref, cb_ref, dtb_ref, a_ref, dsk_ref, nw_ref, exp_ref, o_ref, tail_ref, state_ref):
    c = SSD_CHUNK
    blk = p_ref[...]
    z = blk[:, :GW]
    raw = blk[:, GW:GW + SSD_CONV_CH]
    ext = jnp.concatenate([tail_ref[...], raw], axis=0)
    tail_ref[...] = raw[c - 8:c, :]
    conv = cb_ref[...]
    for i in range(SSD_CONV):
        off = 8 - (SSD_CONV - 1) + i
        conv = conv + cw_ref[i:i + 1, :] * ext[off:off + c, :]
    xbc = _silu(conv)
    x = xbc[:, :GW]
    bm = xbc[:, GW:GW + 256].astype(BF16)
    cm = xbc[:, GW + 256:GW + 512].astype(BF16)

    dt = _softplus(blk[:, GW + SSD_CONV_CH:GW + SSD_CONV_CH + 128] + dtb_ref[...])
    da = dt * a_ref[...]
    ri = lax.broadcasted_iota(jnp.int32, (c, c), 0)
    ci = lax.broadcasted_iota(jnp.int32, (c, c), 1)
    lower_eq = ri >= ci
    tril = lower_eq.astype(BF16)
    cs = _split_mm(tril, da, 3)
    cs_t = cs.T
    spread = exp_ref[...]
    dt_w = _split_mm_rhs(dt, spread, 3)
    cs_w = _split_mm(tril, _split_mm_rhs(da, spread, 3), 3)
    cs_end = cs_w[c - 1:c, :]
    xdt = x * dt_w
    xdt16 = xdt.astype(BF16)
    xdec16 = (xdt * jnp.exp(cs_end - cs_w)).astype(BF16)
    dec_all = jnp.exp(cs_end)

    heads = range(N_HEADS)
    groups = range(N_HEADS // 4)
    sls = [slice(HEAD * h, HEAD * (h + 1)) for h in heads]
    bg = [bm[:, SSD_STATE * g:SSD_STATE * (g + 1)] for g in groups]
    cg = [cm[:, SSD_STATE * g:SSD_STATE * (g + 1)] for g in groups]
    cb = [_nt(cg[g], bg[g]) for g in groups]
    s0 = [state_ref[h] for h in heads]
    lmat = [jnp.exp(jnp.where(lower_eq, cs[:, h:h + 1] - cs_t[h:h + 1, :], -jnp.inf)) for h in heads]
    y_in = [_mm((cb[h // 4] * lmat[h]).astype(BF16), xdt16[:, sls[h]]) for h in heads]
    y_st = [_nt(cg[h // 4], s0[h].astype(BF16)) for h in heads]
    new = [_tn(xdec16[:, sls[h]], bg[h // 4]) for h in heads]
    for h in heads:
        state_ref[h] = s0[h] * dec_all[:, HEAD * h:HEAD * h + 1] + new[h]
    y = (jnp.concatenate(y_in, axis=-1) + jnp.concatenate(y_st, axis=-1) * jnp.exp(cs_w) + x * dsk_ref[...])
    y = y * _silu(z)
    half = GW // 2
    y = jnp.concatenate([_rms(y[:, :half], SSD_NORM_EPS), _rms(y[:, half:], SSD_NORM_EPS)], axis=-1)
    o_ref[...] = (y * nw_ref[...]).astype(o_ref.dtype)


def _ssd(p, prm, bsz, seq):
    c = SSD_CHUNK
    nc = seq // c
    nseq = math.gcd(bsz, SSD_SEQS)
    col = SEC_S // SEC_W
    vec = lambda width: pl.BlockSpec((1, width), lambda b, n: (0, 0))
    out = pl.pallas_call(
        _ssd_kernel,
        grid=(bsz // nseq, nc),
        in_specs=[
            pl.BlockSpec((nseq, c, SEC_W), lambda b, n: (b, n, col)),
            pl.BlockSpec((SSD_CONV, SSD_CONV_CH), lambda b, n: (0, 0)),
            vec(SSD_CONV_CH), vec(128), vec(128), vec(GW), vec(GW),
            pl.BlockSpec((128, GW), lambda b, n: (0, 0)),
        ],
        out_specs=pl.BlockSpec((nseq, c, GW), lambda b, n: (b, n, 0)),
        out_shape=jax.ShapeDtypeStruct((bsz, seq, GW), BF16),
        scratch_shapes=[pltpu.VMEM((nseq, 8, SSD_CONV_CH), F32),
                        pltpu.VMEM((nseq, N_HEADS, HEAD, SSD_STATE), F32)],
        compiler_params=_params(("parallel", "arbitrary")),
        name="mamba2_ssd",
    )(p.reshape(bsz, seq, P_WIDTH), *prm)
    return out.reshape(bsz * seq, GW)


def _ssd_params(conv_w, conv_b, dt_bias, a_log, d_skip, norm_w):
    pad = lambda t: jnp.zeros((1, 128), F32).at[0, :N_HEADS].set(t)
    spread = (jnp.arange(128)[:, None] == jnp.arange(GW)[None, :] // HEAD).astype(BF16)
    return (conv_w, conv_b.reshape(1, -1), pad(dt_bias), pad(-jnp.exp(a_log)),
            jnp.repeat(d_skip, HEAD).reshape(1, -1), norm_w.reshape(1, -1), spread)


def _gelu_tanh(x):
    return 0.5 * x * (1.0 + jnp.tanh(math.sqrt(2.0 / math.pi) * (x + 0.044715 * (x * x * x))))


def _s5_kernel(u_ref, bbr_ref, bbi_ref, lvr_ref, lvi_ref, pwr_ref, pwi_ref, ccr_ref, cci_ref,
               dsk_ref, gw_ref, gb_ref, o_ref, hr_ref, hi_ref, carry_ref):
    @pl.when(pl.program_id(1) == 0)
    def _():
        carry_ref[...] = jnp.zeros_like(carry_ref)

    u = u_ref[...]
    u16 = u.astype(BF16)
    q_in, q_st = GW // S5_QUADS, S5_LANES // S5_QUADS
    for q in range(S5_QUADS):
        uq = u16[:, q * q_in:(q + 1) * q_in]
        hr_ref[:, q * q_st:(q + 1) * q_st] = _mm(uq, bbr_ref[q])
        hi_ref[:, q * q_st:(q + 1) * q_st] = _mm(uq, bbi_ref[q])

    def tile(t, carry):
        cr, ci = carry
        rows = pl.ds(pl.multiple_of(t * S5_TILE, S5_TILE), S5_TILE)
        xr = hr_ref[rows, :]
        xi = hi_ref[rows, :]
        for lvl, s in enumerate((1, 2, 4)):
            lr = lvr_ref[lvl]
            li = lvi_ref[lvl]
            sr = pltpu.roll(xr, s, axis=0)
            si = pltpu.roll(xi, s, axis=0)
            xr, xi = xr + lr * sr - li * si, xi + lr * si + li * sr
        pr = pwr_ref[...]
        pi = pwi_ref[...]
        xr, xi = xr + pr * cr - pi * ci, xi + pr * ci + pi * cr
        hr_ref[rows, :] = xr
        hi_ref[rows, :] = xi
        last = S5_TILE - 1
        return (jnp.broadcast_to(xr[last:last + 1, :], xr.shape), jnp.broadcast_to(xi[last:last + 1, :], xi.shape))

    cr, ci = lax.fori_loop(0, u.shape[0] // S5_TILE, tile, (carry_ref[0], carry_ref[1]))
    carry_ref[0] = cr
    carry_ref[1] = ci

    y = jnp.concatenate(
        [_mm(hr_ref[:, q * q_st:(q + 1) * q_st].astype(BF16), ccr_ref[q])
         - _mm(hi_ref[:, q * q_st:(q + 1) * q_st].astype(BF16), cci_ref[q]) for q in range(S5_QUADS)], axis=-1)
    y = _gelu_tanh(y + dsk_ref[...] * u)
    gate = jax.nn.sigmoid(_mm(y.astype(BF16), gw_ref[...]) + gb_ref[...])
    o_ref[...] = (y * gate).astype(o_ref.dtype)


def _s5(p, prm, bsz, seq):
    c = S5_CHUNK
    nc = seq // c
    col = SEC_F // GW
    full = lambda shape: pl.BlockSpec(shape, lambda b, n: (0,) * len(shape))
    return pl.pallas_call(
        _s5_kernel,
        grid=(bsz, nc),
        in_specs=[
            pl.BlockSpec((c, GW), lambda b, n: (b * nc + n, col)),
            full((S5_QUADS, GW // S5_QUADS, S5_LANES // S5_QUADS)),
            full((S5_QUADS, GW // S5_QUADS, S5_LANES // S5_QUADS)),
            full((3, S5_TILE, S5_LANES)), full((3, S5_TILE, S5_LANES)),
            full((S5_TILE, S5_LANES)), full((S5_TILE, S5_LANES)),
            full((S5_QUADS, S5_LANES // S5_QUADS, GW // S5_QUADS)),
            full((S5_QUADS, S5_LANES // S5_QUADS, GW // S5_QUADS)),
            full((1, GW)), full((GW, GW)), full((1, GW)),
        ],
        out_specs=pl.BlockSpec((c, GW), lambda b, n: (b * nc + n, 0)),
        out_shape=jax.ShapeDtypeStruct((bsz * seq, GW), BF16),
        scratch_shapes=[pltpu.VMEM((c, S5_LANES), F32), pltpu.VMEM((c, S5_LANES), F32),
                        pltpu.VMEM((2, S5_TILE, S5_LANES), F32)],
        compiler_params=_params(("parallel", "arbitrary")),
        name="s5_scan",
    )(p, *prm)


def _s5_params(lam_re, lam_im, log_step, b_re, b_im, c_re, c_im, d_skip, glu_w, glu_b):
    step = jnp.exp(log_step)[:, None]
    mag = jnp.exp(lam_re * step)
    ab_re, ab_im = mag * jnp.cos(lam_im * step), mag * jnp.sin(lam_im * step)
    den = lam_re * lam_re + lam_im * lam_im
    coef_re = ((ab_re - 1) * lam_re + ab_im * lam_im) / den
    coef_im = (ab_im * lam_re - (ab_re - 1) * lam_im) / den
    bb_re = coef_re[..., None] * b_re - coef_im[..., None] * b_im
    bb_im = coef_re[..., None] * b_im + coef_im[..., None] * b_re
    gq = S5_GROUPS // S5_QUADS
    eye = jnp.eye(gq, dtype=F32)
    in_proj = lambda t: jnp.einsum(
        'qgni,gh->qgihn', t.reshape(S5_QUADS, gq, S5_STATE, S5_GROUP_CH), eye
    ).reshape(S5_QUADS, gq * S5_GROUP_CH, gq * S5_STATE).astype(BF16)
    out_proj = lambda t: jnp.einsum(
        'qgin,gh->qgnhi', t.reshape(S5_QUADS, gq, S5_GROUP_CH, S5_STATE), eye
    ).reshape(S5_QUADS, gq * S5_STATE, gq * S5_GROUP_CH).astype(BF16)

    def cmul(a, b):
        return a[0] * b[0] - a[1] * b[1], a[0] * b[1] + a[1] * b[0]

    lam1 = (ab_re.reshape(-1), ab_im.reshape(-1))
    lam2 = cmul(lam1, lam1)
    lam4 = cmul(lam2, lam2)
    rows = jnp.arange(S5_TILE)[:, None]
    lvl_re = jnp.stack([jnp.where(rows >= s, l[0][None, :], 0.0) for s, l in ((1, lam1), (2, lam2), (4, lam4))])
    lvl_im = jnp.stack([jnp.where(rows >= s, l[1][None, :], 0.0) for s, l in ((1, lam1), (2, lam2), (4, lam4))])
    powers = [lam1]
    for _ in range(S5_TILE - 1):
        powers.append(cmul(powers[-1], lam1))
    pw_re = jnp.stack([q[0] for q in powers])
    pw_im = jnp.stack([q[1] for q in powers])
    return (in_proj(bb_re), in_proj(bb_im), lvl_re, lvl_im, pw_re, pw_im, out_proj(c_re), out_proj(c_im),
            d_skip.reshape(1, -1), glu_w.astype(BF16), glu_b.reshape(1, -1))


def _outproj_kernel(ya_ref, yr_ref, ys_ref, yf_ref, w_ref, x_ref, g_ref, nw_ref, o_ref):
    acc = _mm(ya_ref[...], w_ref[0, 0:GW, :])
    acc = acc + _mm(yr_ref[...], w_ref[0, GW:2 * GW, :])
    acc = acc + _mm(ys_ref[...], w_ref[0, 2 * GW:3 * GW, :])
    acc = acc + _mm(yf_ref[...], w_ref[0, 3 * GW:4 * GW, :])
    o_ref[...] = x_ref[...] + g_ref[0] * (_rms(acc, NORM_EPS) * nw_ref[...])


def _out_projection(ys, w, layer, x2, gate, nw, seq):
    t = x2.shape[0]
    tm = 512
    per_b = seq // tm
    ymap = pl.BlockSpec((tm, GW), lambda i: (i, 0))
    return pl.pallas_call(
        _outproj_kernel,
        grid=(t // tm,),
        in_specs=[
            ymap, ymap, ymap, ymap,
            pl.BlockSpec((1, D_MODEL, D_MODEL), lambda i: (layer, 0, 0)),
            pl.BlockSpec((tm, D_MODEL), lambda i: (i, 0)),
            pl.BlockSpec((1, 1, D_MODEL), lambda i: (i // per_b, 0, 0)),
            pl.BlockSpec((1, D_MODEL), lambda i: (0, 0)),
        ],
        out_specs=pl.BlockSpec((tm, D_MODEL), lambda i: (i, 0)),
        out_shape=jax.ShapeDtypeStruct((t, D_MODEL), F32),
        compiler_params=_params(("parallel",)),
        name="out_projection",
    )(*ys, w, x2, gate, nw)


def _ffn_kernel(x_ref, xn_ref, npre_ref, sc_ref, sh_ref, scn_ref, shn_ref, wg_ref, wu_ref, wd_ref, g_ref,
                npost_ref, o_ref, h_ref, acc_ref):
    i = pl.program_id(0)
    f = pl.program_id(1)
    last = pl.num_programs(1) - 1
    slot = i % 2

    @pl.when((i == 0) & (f == 0))
    def _():
        h_ref[0] = _prenorm(x_ref[...], npre_ref[...], sc_ref[0], sh_ref[0]).astype(BF16)

    def partial():
        h = h_ref[slot]
        act = _silu(_mm(h, wg_ref[0])) * _mm(h, wu_ref[0])
        return _mm(act.astype(BF16), wd_ref[0])

    @pl.when(f == 0)
    def _():
        acc_ref[...] = partial()

    @pl.when((f > 0) & (f < last))
    def _():
        acc_ref[...] += partial()

    @pl.when(f == last)
    def _():
        acc = acc_ref[...] + partial()
        h_ref[1 - slot] = _prenorm(xn_ref[...], npre_ref[...], scn_ref[0], shn_ref[0]).astype(BF16)
        o_ref[...] = x_ref[...] + g_ref[0] * (_rms(acc, NORM_EPS) * npost_ref[...])


def _dense_ffn(x2, npre, scale, shift, wg, wu, wd, layer, gate, npost, seq):
    t = x2.shape[0]
    tm, tf = 512, 512
    per_b = seq // tm
    n_i = t // tm
    assert D_FF // tf >= 3
    nxt = lambda i: jnp.minimum(i + 1, n_i - 1)
    mod = pl.BlockSpec((1, 1, D_MODEL), lambda i, f: (i // per_b, 0, 0))
    mod_next = pl.BlockSpec((1, 1, D_MODEL), lambda i, f: (nxt(i) // per_b, 0, 0))
    vec = pl.BlockSpec((1, D_MODEL), lambda i, f: (0, 0))
    return pl.pallas_call(
        _ffn_kernel,
        grid=(n_i, D_FF // tf),
        in_specs=[
            pl.BlockSpec((tm, D_MODEL), lambda i, f: (i, 0)),
            pl.BlockSpec((tm, D_MODEL), lambda i, f: (nxt(i), 0)),
            vec, mod, mod, mod_next, mod_next,
            pl.BlockSpec((1, D_MODEL, tf), lambda i, f: (layer, 0, f)),
            pl.BlockSpec((1, D_MODEL, tf), lambda i, f: (layer, 0, f)),
            pl.BlockSpec((1, tf, D_MODEL), lambda i, f: (layer, f, 0)),
            mod, vec,
        ],
        out_specs=pl.BlockSpec((tm, D_MODEL), lambda i, f: (i, 0)),
        out_shape=jax.ShapeDtypeStruct((t, D_MODEL), F32),
        scratch_shapes=[pltpu.VMEM((2, tm, D_MODEL), BF16), pltpu.VMEM((tm, D_MODEL), F32)],
        compiler_params=_params(("arbitrary", "arbitrary")),
        name="dense_swiglu",
    )(x2, x2, npre, scale, shift, scale, shift, wg, wu, wd, gate, npost)


def _router_kernel(x_ref, npre_ref, sc_ref, sh_ref, rw_ref, rb_ref, h_ref, lg_ref):
    h = _prenorm(x_ref[...], npre_ref[...], sc_ref[0], sh_ref[0])
    h_ref[...] = h.astype(BF16)
    lg_ref[...] = _mm(h, rw_ref[...], precision=HIGHEST) + rb_ref[...]


def _router(x2, npre, scale, shift, rw, rb, seq):
    t = x2.shape[0]
    tm = 512
    per_b = seq // tm
    mod = pl.BlockSpec((1, 1, D_MODEL), lambda i: (i // per_b, 0, 0))
    return pl.pallas_call(
        _router_kernel,
        grid=(t // tm,),
        in_specs=[
            pl.BlockSpec((tm, D_MODEL), lambda i: (i, 0)),
            pl.BlockSpec((1, D_MODEL), lambda i: (0, 0)),
            mod, mod,
            pl.BlockSpec((D_MODEL, 128), lambda i: (0, 0)),
            pl.BlockSpec((1, 128), lambda i: (0, 0)),
        ],
        out_specs=[pl.BlockSpec((tm, D_MODEL), lambda i: (i, 0)), pl.BlockSpec((tm, 128), lambda i: (i, 0))],
        out_shape=[jax.ShapeDtypeStruct((t, D_MODEL), BF16), jax.ShapeDtypeStruct((t, 128), F32)],
        compiler_params=_params(("parallel",)),
        name="moe_router",
    )(x2, npre, scale, shift, rw, rb)


def _moe_ffn_kernel(be_ref, used_ref, xs_ref, wg_ref, wu_ref, wd_ref, o_ref, acc_ref, *, nf):
    i = pl.program_id(0)
    f = pl.program_id(1)
    last = nf - 1
    live = i < used_ref[0]

    def partial():
        xs = xs_ref[...]
        act = _silu(_mm(xs, wg_ref[0, 0])) * _mm(xs, wu_ref[0, 0])
        return _mm(act.astype(BF16), wd_ref[0, 0])

    @pl.when(live & (f == 0))
    def _():
        acc_ref[...] = partial()

    if nf > 2:
        @pl.when(live & (f > 0) & (f < last))
        def _():
            acc_ref[...] += partial()

    @pl.when(live & (f == last))
    def _():
        o_ref[...] = (acc_ref[...] + partial()).astype(o_ref.dtype)

    @pl.when(jnp.logical_not(live) & (f == last))
    def _():
        o_ref[...] = jnp.zeros_like(o_ref)


def _moe_ffn(block_e, n_used, xs, wg, wu, wd, layer):
    rows = xs.shape[0]
    tf = D_EXPERT // 2
    nf = D_EXPERT // tf

    def tile(i, f, used):
        return jnp.where(i < used[0], f, nf - 1)

    grid_spec = pltpu.PrefetchScalarGridSpec(
        num_scalar_prefetch=2,
        grid=(rows // MOE_BLOCK, nf),
        in_specs=[
            pl.BlockSpec((MOE_BLOCK, D_MODEL), lambda i, f, be, used: (i, 0)),
            pl.BlockSpec((1, 1, D_MODEL, tf), lambda i, f, be, used: (layer, be[i], 0, tile(i, f, used))),
            pl.BlockSpec((1, 1, D_MODEL, tf), lambda i, f, be, used: (layer, be[i], 0, tile(i, f, used))),
            pl.BlockSpec((1, 1, tf, D_MODEL), lambda i, f, be, used: (layer, be[i], tile(i, f, used), 0)),
        ],
        out_specs=pl.BlockSpec((MOE_BLOCK, D_MODEL), lambda i, f, be, used: (i, 0)),
        scratch_shapes=[pltpu.VMEM((MOE_BLOCK, D_MODEL), F32)],
    )
    assert nf >= 2
    return pl.pallas_call(
        functools.partial(_moe_ffn_kernel, nf=nf),
        grid_spec=grid_spec,
        out_shape=jax.ShapeDtypeStruct((rows, D_MODEL), BF16),
        compiler_params=_params(("parallel", "arbitrary"), MOE_VMEM_LIMIT),
        name="moe_swiglu",
    )(block_e, n_used, xs, wg, wu, wd)


def _combine_kernel(y0_ref, y1_ref, p_ref, x_ref, g_ref, npost_ref, o_ref):
    p = p_ref[...]
    y = y0_ref[...].astype(F32) * p[:, 0:1] + y1_ref[...].astype(F32) * p[:, 1:2]
    o_ref[...] = x_ref[...] + g_ref[0] * (_rms(y, NORM_EPS) * npost_ref[...])


def _moe_combine(y0, y1, top_p, x2, gate, npost, seq):
    t = x2.shape[0]
    tm = 512
    per_b = seq // tm
    row = pl.BlockSpec((tm, D_MODEL), lambda i: (i, 0))
    return pl.pallas_call(
        _combine_kernel,
        grid=(t // tm,),
        in_specs=[row, row, pl.BlockSpec((tm, 128), lambda i: (i, 0)), row,
                  pl.BlockSpec((1, 1, D_MODEL), lambda i: (i // per_b, 0, 0)),
                  pl.BlockSpec((1, D_MODEL), lambda i: (0, 0))],
        out_specs=row,
        out_shape=jax.ShapeDtypeStruct((t, D_MODEL), F32),
        compiler_params=_params(("parallel",)),
        name="moe_combine",
    )(y0, y1, top_p, x2, gate, npost)


def _routed_ffn(x2, npre, scale, shift, rw, rb, wg, wu, wd, layer, gate, npost, seq):
    t = x2.shape[0]
    rw_pad = jnp.zeros((D_MODEL, 128), F32).at[:, :N_EXPERTS].set(rw)
    rb_pad = jnp.zeros((1, 128), F32).at[0, :N_EXPERTS].set(rb)
    h, logits = _router(x2, npre, scale, shift, rw_pad, rb_pad, seq)
    logits = logits[:, :N_EXPERTS]
    top_logit, top_idx = lax.top_k(logits, TOP_K)
    top_p = jax.nn.softmax(top_logit, axis=-1)
    n_assign = t * TOP_K
    flat_e = top_idx.reshape(-1).astype(jnp.int32)
    order = jnp.argsort(flat_e).astype(jnp.int32)
    rank = jnp.argsort(order).astype(jnp.int32)
    counts = jnp.sum((flat_e[:, None] == jnp.arange(N_EXPERTS, dtype=jnp.int32)[None, :]).astype(jnp.int32), axis=0)
    padded = (counts + MOE_BLOCK - 1) // MOE_BLOCK * MOE_BLOCK
    pad_end = jnp.cumsum(padded)
    pad_start = pad_end - padded
    start = jnp.cumsum(counts) - counts
    n_blocks = -(-n_assign // MOE_BLOCK) + N_EXPERTS
    rows = n_blocks * MOE_BLOCK
    block_e = jnp.minimum(jnp.searchsorted(pad_end, jnp.arange(n_blocks) * MOE_BLOCK, side='right'),
                          N_EXPERTS - 1).astype(jnp.int32)
    e_row = jnp.repeat(block_e, MOE_BLOCK)
    off = jnp.arange(rows, dtype=jnp.int32) - pad_start[e_row]
    src = order[jnp.clip(start[e_row] + off, 0, n_assign - 1)]
    row_tok = jnp.where(off < counts[e_row], src // TOP_K, t)
    h_pad = jnp.concatenate([h, jnp.zeros((1, D_MODEL), h.dtype)], axis=0)
    xs = h_pad[row_tok]
    n_used = (pad_end[-1:] // MOE_BLOCK).astype(jnp.int32)
    yb = _moe_ffn(block_e, n_used, xs, wg, wu, wd, layer)
    pos = (pad_start[flat_e] + rank - start[flat_e]).reshape(t, TOP_K)
    p_pad = jnp.zeros((t, 128), F32).at[:, :TOP_K].set(top_p)
    return _moe_combine(yb[pos[:, 0]], yb[pos[:, 1]], p_pad, x2, gate, npost, seq)


def _in_weights(w_in):
    a0, r0, s0, f0 = 0, ATTN_IN, ATTN_IN + RWKV_IN, ATTN_IN + RWKV_IN + SSD_IN
    zeros = jnp.zeros(w_in.shape[:-1] + (SEC_W - SSD_IN,), w_in.dtype)
    return jnp.concatenate([
        w_in[..., r0:s0],
        w_in[..., s0:f0], zeros,
        w_in[..., f0:f0 + GW],
        w_in[..., a0:r0],
    ], axis=-1).astype(BF16)


def kernel(x, c, ada_w, ada_b, norm_pre, norm_post, w_in, w_out, attn_sink, rwkv_mu, rwkv_w0, rwkv_w_up, rwkv_a0, rwkv_a_up, rwkv_g_up, rwkv_k_k, rwkv_k_a, rwkv_r_k, rwkv_ln_w, rwkv_ln_b, ssd_conv_w, ssd_conv_b, ssd_dt_bias, ssd_a_log, ssd_d, ssd_norm_w, s5_lam_re, s5_lam_im, s5_log_step, s5_b_re, s5_b_im, s5_c_re, s5_c_im, s5_d, s5_glu_w, s5_glu_b, ffn_w_gate, ffn_w_up, ffn_w_down, moe_router_w, moe_router_b, moe_w_gate, moe_w_up, moe_w_down):
    bsz, seq, d = x.shape
    depth = ada_w.shape[0]
    mod = _ada_modulation(c, ada_w, ada_b)
    mod = mod.reshape(depth, 2, bsz, 3, 1, d)
    x2 = x.reshape(bsz * seq, d)
    w_in16 = _in_weights(w_in)
    w_out16 = w_out.astype(BF16)
    ffn16 = (ffn_w_gate.astype(BF16), ffn_w_up.astype(BF16), ffn_w_down.astype(BF16))
    moe16 = (moe_w_gate.astype(BF16), moe_w_up.astype(BF16), moe_w_down.astype(BF16))
    for i in range(depth):
        shift, scale, gate = mod[i, 0, :, 0], mod[i, 0, :, 1], mod[i, 0, :, 2]
        p = _in_projection(x2, norm_pre[i, 0][None], scale, shift, w_in16, i, seq)
        y_attn = _attention(p, attn_sink[i][None], bsz, seq)
        y_rwkv = _rwkv(p, _rwkv_params(rwkv_mu[i], rwkv_w0[i], rwkv_w_up[i], rwkv_a0[i], rwkv_a_up[i],
                                       rwkv_g_up[i], rwkv_k_k[i], rwkv_k_a[i], rwkv_r_k[i].reshape(-1),
                                       rwkv_ln_w[i], rwkv_ln_b[i]), bsz, seq)
        y_ssd = _ssd(p, _ssd_params(ssd_conv_w[i], ssd_conv_b[i], ssd_dt_bias[i], ssd_a_log[i], ssd_d[i],
                                    ssd_norm_w[i]), bsz, seq)
        y_s5 = _s5(p, _s5_params(s5_lam_re[i], s5_lam_im[i], s5_log_step[i], s5_b_re[i], s5_b_im[i],
                                 s5_c_re[i], s5_c_im[i], s5_d[i], s5_glu_w[i], s5_glu_b[i]), bsz, seq)
        x2 = _out_projection((y_attn, y_rwkv, y_ssd, y_s5), w_out16, i, x2, gate, norm_post[i, 0][None], seq)
        shift, scale, gate = mod[i, 1, :, 0], mod[i, 1, :, 1], mod[i, 1, :, 2]
        j = i // 2
        if i % 2 == 0:
            x2 = _dense_ffn(x2, norm_pre[i, 1][None], scale, shift, *ffn16, j, gate, norm_post[i, 1][None], seq)
        else:
            x2 = _routed_ffn(x2, norm_pre[i, 1][None], scale, shift, moe_router_w[j], moe_router_b[j],
                             *moe16, j, gate, norm_post[i, 1][None], seq)
    return x2.reshape(bsz, seq, d)
```

```python
import functools
import math

import jax
import jax.numpy as jnp
from jax import lax
from jax.experimental import pallas as pl
from jax.experimental.pallas import tpu as pltpu

F32 = jnp.float32
BF16 = jnp.bfloat16
HIGHEST = lax.Precision.HIGHEST

D_MODEL = 2048
DEPTH = 4
GW = 512
NORM_EPS = 1e-6
HEAD = 64
N_HEADS = 8

ATTN_KV_HEADS = 2
ATTN_GROUP = 4
ATTN_BLOCK = 128
ATTN_IN = 768
ATTN_BLOCKS_PER_STEP = 4

RWKV_IN = 1792
RWKV_LN_EPS = 64e-5
RWKV_CHUNK = 64
RWKV_SEQS = 4

SSD_STATE = 128
SSD_CONV = 4
SSD_CHUNK = 128
SSD_SEQS = 4
SSD_CONV_CH = 1024
SSD_IN = 1544
SSD_NORM_EPS = 1e-5

S5_GROUPS = 32
S5_GROUP_CH = 16
S5_STATE = 64
S5_LANES = S5_GROUPS * S5_STATE
S5_CHUNK = 512
S5_TILE = 8
S5_QUADS = 4

D_FF = 5632
N_EXPERTS = 8
TOP_K = 2
D_EXPERT = 2816
MOE_BLOCK = 512

SEC_R = 0
SEC_S = 1792
SEC_F = 3584
SEC_Q = 4096
SEC_KV = 4608
P_WIDTH = 4864
SEC_W = 1792

VMEM_LIMIT = 52 * 1024 * 1024
MOE_VMEM_LIMIT = 58 * 1024 * 1024


def _params(sem, vmem=VMEM_LIMIT):
    return pltpu.CompilerParams(dimension_semantics=sem, vmem_limit_bytes=vmem)


def _nt(a, b, **kw):
    return lax.dot_general(a, b, (((1,), (1,)), ((), ())), preferred_element_type=F32, **kw)


def _tn(a, b, **kw):
    return lax.dot_general(a, b, (((0,), (0,)), ((), ())), preferred_element_type=F32, **kw)


def _mm(a, b, **kw):
    return jnp.dot(a, b, preferred_element_type=F32, **kw)


def _split_mm(a, x, parts):
    acc = None
    for _ in range(parts):
        piece = x.astype(BF16)
        term = _mm(a, piece)
        acc = term if acc is None else acc + term
        x = x - piece.astype(F32)
    return acc


def _split_mm_rhs(x, b, parts):
    acc = None
    for _ in range(parts):
        piece = x.astype(BF16)
        term = _mm(piece, b)
        acc = term if acc is None else acc + term
        x = x - piece.astype(F32)
    return acc


def _softplus(x):
    return jnp.maximum(x, 0.0) + jnp.log1p(jnp.exp(-jnp.abs(x)))


def _silu(x):
    return x * jax.nn.sigmoid(x)


def _rms(x, eps):
    return x * lax.rsqrt(jnp.mean(x * x, axis=-1, keepdims=True) + eps)


def _ada_kernel(c_ref, w_ref, b_ref, o_ref):
    c = c_ref[...]
    o_ref[0] = _mm(_silu(c).astype(BF16), w_ref[0].astype(BF16)) + b_ref[0]


def _ada_modulation(c, ada_w, ada_b):
    bsz = c.shape[0]
    rows = 8 * pl.cdiv(bsz, 8)
    c_pad = jnp.zeros((rows, D_MODEL), F32).at[:bsz].set(c)
    n_mod = ada_w.shape[0] * 2
    w = ada_w.reshape(n_mod, D_MODEL, 3 * D_MODEL)
    b = ada_b.reshape(n_mod, 1, 3 * D_MODEL)
    tn = 768
    out = pl.pallas_call(
        _ada_kernel,
        grid=(n_mod, 3 * D_MODEL // tn),
        in_specs=[
            pl.BlockSpec((rows, D_MODEL), lambda m, j: (0, 0)),
            pl.BlockSpec((1, D_MODEL, tn), lambda m, j: (m, 0, j)),
            pl.BlockSpec((1, 1, tn), lambda m, j: (m, 0, j)),
        ],
        out_specs=pl.BlockSpec((1, rows, tn), lambda m, j: (m, 0, j)),
        out_shape=jax.ShapeDtypeStruct((n_mod, rows, 3 * D_MODEL), F32),
        compiler_params=_params(("parallel", "parallel")),
        name="ada_modulation",
    )(c_pad, w, b)
    return out[:, :bsz]


def _prenorm(x, nw, scale, shift):
    return (_rms(x, NORM_EPS) * nw) * (1.0 + scale) + shift


def _inproj_kernel(x_ref, nw_ref, sc_ref, sh_ref, w_ref, o_ref, h_ref):
    i = pl.program_id(0)
    j = pl.program_id(1)
    last = pl.num_programs(1) - 1
    slot = i % 2

    def normalised():
        return _prenorm(x_ref[...], nw_ref[...], sc_ref[0], sh_ref[0]).astype(BF16)

    @pl.when((i == 0) & (j == 0))
    def _():
        h_ref[0] = normalised()

    @pl.when(j < last)
    def _():
        o_ref[...] = _mm(h_ref[slot], w_ref[0]).astype(o_ref.dtype)

    @pl.when(j == last)
    def _():
        o_ref[...] = _mm(h_ref[slot], w_ref[0]).astype(o_ref.dtype)
        h_ref[1 - slot] = normalised()


def _in_projection(x2, nw, scale, shift, w, layer, seq):
    t = x2.shape[0]
    tm, tn = 512, P_WIDTH // 2
    per_b = seq // tm
    n_i, n_j = t // tm, P_WIDTH // tn

    def x_tile(i, j):
        return jnp.minimum(jnp.where(j == n_j - 1, i + 1, i), n_i - 1)

    return pl.pallas_call(
        _inproj_kernel,
        grid=(n_i, n_j),
        in_specs=[
            pl.BlockSpec((tm, D_MODEL), lambda i, j: (x_tile(i, j), 0)),
            pl.BlockSpec((1, D_MODEL), lambda i, j: (0, 0)),
            pl.BlockSpec((1, 1, D_MODEL), lambda i, j: (x_tile(i, j) // per_b, 0, 0)),
            pl.BlockSpec((1, 1, D_MODEL), lambda i, j: (x_tile(i, j) // per_b, 0, 0)),
            pl.BlockSpec((1, D_MODEL, tn), lambda i, j: (layer, 0, j)),
        ],
        out_specs=pl.BlockSpec((tm, tn), lambda i, j: (i, j)),
        out_shape=jax.ShapeDtypeStruct((t, P_WIDTH), BF16),
        scratch_shapes=[pltpu.VMEM((2, tm, D_MODEL), BF16)],
        compiler_params=_params(("arbitrary", "arbitrary")),
        name="in_projection",
    )(x2, nw, scale, shift, w)


def _alibi_slope(h):
    return 2.0 ** (-8.0 * (h + 1) / N_HEADS)


def _attn_kernel(sink_ref, q_ref, cur_ref, prev_ref, o_ref):
    n = pl.program_id(1)
    blk = ATTN_BLOCK
    nblk = q_ref.shape[0] // blk
    q = q_ref[...]
    kv = jnp.concatenate([prev_ref[...], cur_ref[...]], axis=0).astype(BF16)
    qi = lax.broadcasted_iota(jnp.int32, (blk, 2 * blk), 0) + blk
    kj = lax.broadcasted_iota(jnp.int32, (blk, 2 * blk), 1)
    dist = qi - kj
    window = (dist >= 0) & (dist < blk)
    first = window & ((kj >= blk) | (n > 0))
    distf = dist.astype(F32)
    units = [(i, h) for i in range(nblk) for h in range(N_HEADS)]
    kg = {(i, g): kv[i * blk:(i + 2) * blk, HEAD * g:HEAD * (g + 1)]
          for i in range(nblk) for g in range(ATTN_KV_HEADS)}
    vg = {(i, g): kv[i * blk:(i + 2) * blk, 128 + HEAD * g:128 + HEAD * (g + 1)]
          for i in range(nblk) for g in range(ATTN_KV_HEADS)}
    s = {(i, h): _nt(q[i * blk:(i + 1) * blk, HEAD * h:HEAD * (h + 1)].astype(BF16), kg[i, h // ATTN_GROUP])
         for i, h in units}
    s = {(i, h): jnp.where(first if i == 0 else window,
                           s[i, h] * (HEAD ** -0.5) - _alibi_slope(h) * distf, -jnp.inf) for i, h in units}
    m = {(i, h): jnp.maximum(jnp.max(s[i, h], axis=-1, keepdims=True), sink_ref[0, h]) for i, h in units}
    p = {u: jnp.exp(s[u] - m[u]) for u in units}
    denom = {(i, h): jnp.sum(p[i, h], axis=-1, keepdims=True) + jnp.exp(sink_ref[0, h] - m[i, h]) for i, h in units}
    out = {(i, h): _mm(p[i, h].astype(BF16), vg[i, h // ATTN_GROUP]) / denom[i, h] for i, h in units}
    rows = [jnp.concatenate([out[i, h] for h in range(N_HEADS)], axis=-1) for i in range(nblk)]
    o_ref[...] = jnp.concatenate(rows, axis=0).astype(o_ref.dtype)


def _attention(p, sinks, bsz, seq):
    rows = ATTN_BLOCK * ATTN_BLOCKS_PER_STEP
    steps = seq // rows
    nb = seq // ATTN_BLOCK
    q_col = SEC_Q // GW
    kv_col = SEC_KV // 256
    return pl.pallas_call(
        _attn_kernel,
        grid=(bsz, steps),
        in_specs=[
            pl.BlockSpec(memory_space=pltpu.SMEM),
            pl.BlockSpec((rows, GW), lambda b, n: (b * steps + n, q_col)),
            pl.BlockSpec((rows, 256), lambda b, n: (b * steps + n, kv_col)),
            pl.BlockSpec((ATTN_BLOCK, 256),
                         lambda b, n: (b * nb + jnp.maximum(n * ATTN_BLOCKS_PER_STEP - 1, 0), kv_col)),
        ],
        out_specs=pl.BlockSpec((rows, GW), lambda b, n: (b * steps + n, 0)),
        out_shape=jax.ShapeDtypeStruct((bsz * seq, GW), BF16),
        compiler_params=_params(("parallel", "arbitrary")),
        name="swa_attention",
    )(sinks, p, p, p)


def _rwkv_kernel(p_ref, mu_ref, w0_ref, wup_ref, a0_ref, aup_ref, gup_ref, kk_ref, ka_ref,
                 rk_ref, lnw_ref, lnb_ref, hsum_ref, o_ref, last_ref, state_ref):
    c = RWKV_CHUNK
    nseq = p_ref.shape[0]
    rows = nseq * c

    @pl.when(pl.program_id(1) == 0)
    def _():
        last_ref[...] = jnp.zeros_like(last_ref)
        state_ref[...] = jnp.zeros_like(state_ref)

    p = p_ref[...].astype(F32).reshape(rows, RWKV_IN)
    row = lax.broadcasted_iota(jnp.int32, (rows, 1), 0)
    shifted = pltpu.roll(p, 1, axis=0)
    for j in range(nseq):
        shifted = jnp.where(row == j * c, last_ref[j:j + 1, :], shifted)
        last_ref[j:j + 1, :] = p[(j + 1) * c - 1:(j + 1) * c, :]
    pm = p + mu_ref[...] * (shifted - p)

    r = pm[:, 0:GW]
    k = pm[:, GW:2 * GW]
    v = pm[:, 2 * GW:3 * GW]
    wa = pm[:, 3 * GW:3 * GW + 128]
    gd = pm[:, 3 * GW + 128:3 * GW + 256]

    w = -_softplus(-(w0_ref[...] + _mm(jnp.tanh(wa).astype(BF16), wup_ref[...]))) - 0.5
    logd = -jnp.exp(w)
    a = jax.nn.sigmoid(a0_ref[...] + _mm(wa.astype(BF16), aup_ref[...]))
    g = _mm(jax.nn.sigmoid(gd).astype(BF16), gup_ref[...])

    kk = k * kk_ref[...]
    sumsq = _split_mm_rhs(kk * kk, hsum_ref[...], 2)
    kk = kk / jnp.maximum(jnp.sqrt(sumsq), 1e-12)
    k = k * (1.0 + (a - 1.0) * ka_ref[...])
    b = kk * a

    rr = lax.broadcasted_iota(jnp.int32, (rows, rows), 0)
    cr = lax.broadcasted_iota(jnp.int32, (rows, rows), 1)
    tril_seq = ((rr >= cr) & (rr // c == cr // c)).astype(BF16)
    cum = _split_mm(tril_seq, logd, 3)
    g_inv = jnp.exp(-cum)
    at16 = ((-kk) * jnp.exp(cum - logd)).astype(BF16)
    bt = (b * g_inv).astype(BF16)
    kt = (k * g_inv).astype(BF16)
    rt = (r * jnp.exp(cum)).astype(BF16)
    v16 = v.astype(BF16)
    rk = r * k * rk_ref[...]

    ri = lax.broadcasted_iota(jnp.int32, (c, c), 0)
    ci = lax.broadcasted_iota(jnp.int32, (c, c), 1)
    lower = ri > ci
    eye = (ri == ci).astype(F32)
    ri2 = lax.broadcasted_iota(jnp.int32, (c, 2 * c), 0)
    ci2 = lax.broadcasted_iota(jnp.int32, (c, 2 * c), 1)
    k_half = ci2 >= c
    cj2 = jnp.where(k_half, ci2 - c, ci2)
    strict_k = k_half & (ri2 > cj2)
    lower_eq2 = ri2 >= cj2
    levels = []
    size = 1
    while size < c:
        levels.append((ri // (2 * size) == ci // (2 * size)) & (ri // size > ci // size))
        size *= 2

    units = [(j, h) for j in range(nseq) for h in range(N_HEADS)]
    rs = {u: slice(u[0] * c, (u[0] + 1) * c) for u in units}
    ls = {u: slice(HEAD * u[1], HEAD * (u[1] + 1)) for u in units}
    v_h = {u: v16[rs[u], ls[u]] for u in units}
    ar = {u: jnp.concatenate([at16[rs[u], ls[u]], rt[rs[u], ls[u]]], axis=0) for u in units}
    bk = {u: jnp.concatenate([bt[rs[u], ls[u]], kt[rs[u], ls[u]]], axis=0) for u in units}
    s0 = {u: state_ref[u[0], u[1]] for u in units}
    gram = {u: _nt(ar[u], bk[u]) for u in units}
    l_ab = {u: jnp.where(lower, gram[u][:c, :c], 0.0).astype(BF16) for u in units}
    l_ak = {u: jnp.where(strict_k, gram[u][:c, :], 0.0).astype(BF16) for u in units}
    m_r = {u: jnp.where(lower_eq2, gram[u][c:, :], 0.0).astype(BF16) for u in units}
    inv = {u: eye + jnp.where(levels[0], l_ab[u].astype(F32), 0.0) for u in units}
    for lvl in levels[1:]:
        inv16 = {u: inv[u].astype(BF16) for u in units}
        wl = {u: _mm(jnp.where(lvl, l_ab[u], jnp.zeros_like(l_ab[u])), inv16[u]).astype(BF16) for u in units}
        inv = {u: inv[u] + _mm(inv16[u], wl[u]) for u in units}
    sx = {u: _nt(ar[u], s0[u].astype(BF16)) for u in units}
    x = {u: sx[u][:c] + _mm(l_ak[u], jnp.concatenate([v_h[u], v_h[u]], axis=0)) for u in units}
    us = {u: _mm(inv[u].astype(BF16), x[u].astype(BF16)).astype(BF16) for u in units}
    uv = {u: jnp.concatenate([us[u], v_h[u]], axis=0) for u in units}
    y = {u: sx[u][c:] + _mm(m_r[u], uv[u]) for u in units}
    for u in units:
        g_end = jnp.exp(cum[rs[u].stop - 1:rs[u].stop, ls[u]])
        state_ref[u[0], u[1]] = (s0[u] + _tn(uv[u], bk[u])) * g_end
    outs = []
    for j in range(nseq):
        heads = []
        for h in range(N_HEADS):
            u = (j, h)
            mean = jnp.mean(y[u], axis=-1, keepdims=True)
            yc = y[u] - mean
            var = jnp.mean(yc * yc, axis=-1, keepdims=True)
            bonus = jnp.sum(rk[rs[u], ls[u]], axis=-1, keepdims=True) * v[rs[u], ls[u]]
            heads.append(yc * lax.rsqrt(var + RWKV_LN_EPS) * lnw_ref[:, ls[u]] + lnb_ref[:, ls[u]] + bonus)
        outs.append(jnp.concatenate(heads, axis=-1))
    out = jnp.concatenate(outs, axis=0) * g
    o_ref[...] = out.reshape(nseq, c, GW).astype(o_ref.dtype)


def _rwkv(p, prm, bsz, seq):
    c = RWKV_CHUNK
    nc = seq // c
    nseq = math.gcd(bsz, RWKV_SEQS)
    col = SEC_R // SEC_W
    vec = lambda width: pl.BlockSpec((1, width), lambda b, n: (0, 0))
    mat = lambda rows: pl.BlockSpec((rows, GW), lambda b, n: (0, 0))
    out = pl.pallas_call(
        _rwkv_kernel,
        grid=(bsz // nseq, nc),
        in_specs=[
            pl.BlockSpec((nseq, c, RWKV_IN), lambda b, n: (b, n, col)),
            vec(RWKV_IN), vec(GW), mat(128), vec(GW), mat(128), mat(128),
            vec(GW), vec(GW), vec(GW), vec(GW), vec(GW),
            pl.BlockSpec((GW, GW), lambda b, n: (0, 0)),
        ],
        out_specs=pl.BlockSpec((nseq, c, GW), lambda b, n: (b, n, 0)),
        out_shape=jax.ShapeDtypeStruct((bsz, seq, GW), BF16),
        scratch_shapes=[pltpu.VMEM((8, RWKV_IN), F32), pltpu.VMEM((nseq, N_HEADS, HEAD, HEAD), F32)],
        compiler_params=_params(("parallel", "arbitrary")),
        name="rwkv7_chunked",
    )(p.reshape(bsz, seq, P_WIDTH), *prm)
    return out.reshape(bsz * seq, GW)


def _rwkv_params(mu, w0, w_up, a0, a_up, g_up, k_k, k_a, r_k, ln_w, ln_b):
    zeros = jnp.zeros((64, GW), F32)
    wup = jnp.concatenate([w_up, zeros], axis=0).astype(BF16)
    aup = jnp.concatenate([zeros, a_up], axis=0).astype(BF16)
    head = jnp.arange(GW) // HEAD
    hsum = (head[:, None] == head[None, :]).astype(BF16)
    row = lambda t: t.reshape(1, -1)
    return (row(mu), row(w0), wup, row(a0), aup, g_up.astype(BF16), row(k_k), row(k_a),
            row(r_k), row(ln_w), row(ln_b), hsum)


def _ssd_kernel(p_ref, cw_ref, cb_ref, dtb_ref, a_ref, dsk_ref, nw_ref, exp_ref, o_ref, tail_ref, state_ref):
    @pl.when(pl.program_id(1) == 0)
    def _():
        tail_ref[...] = jnp.zeros_like(tail_ref)
        state_ref[...] = jnp.zeros_like(state_ref)

    for j in range(p_ref.shape[0]):
        _ssd_chunk(p_ref.at[j], cw_ref, cb_ref, dtb_ref, a_ref, dsk_ref, nw_ref, exp_ref, o_ref.at[j],
                   tail_ref.at[j], state_ref.at[j])


def _ssd_chunk(p_ref, cw_ref, cb_ref, dtb_ref, a_ref, dsk_ref, nw_ref, exp_ref, o_ref, tail_ref, state_ref):
    c = SSD_CHUNK
    blk = p_ref[...].astype(F32)
    z = blk[:, :GW]
    raw = blk[:, GW:GW + SSD_CONV_CH]
    ext = jnp.concatenate([tail_ref[...], raw], axis=0)
    tail_ref[...] = raw[c - 8:c, :]
    conv = cb_ref[...]
    for i in range(SSD_CONV):
        off = 8 - (SSD_CONV - 1) + i
        conv = conv + cw_ref[i:i + 1, :] * ext[off:off + c, :]
    xbc = _silu(conv)
    x = xbc[:, :GW]
    bm = xbc[:, GW:GW + 256].astype(BF16)
    cm = xbc[:, GW + 256:GW + 512].astype(BF16)

    dt = _softplus(blk[:, GW + SSD_CONV_CH:GW + SSD_CONV_CH + 128] + dtb_ref[...])
    da = dt * a_ref[...]
    ri = lax.broadcasted_iota(jnp.int32, (c, c), 0)
    ci = lax.broadcasted_iota(jnp.int32, (c, c), 1)
    lower_eq = ri >= ci
    tril = lower_eq.astype(BF16)
    cs = _split_mm(tril, da, 3)
    cs_t = cs.T
    spread = exp_ref[...]
    dt_w = _split_mm_rhs(dt, spread, 3)
    cs_w = _split_mm(tril, _split_mm_rhs(da, spread, 3), 3)
    cs_end = cs_w[c - 1:c, :]
    xdt = x * dt_w
    xdt16 = xdt.astype(BF16)
    xdec16 = (xdt * jnp.exp(cs_end - cs_w)).astype(BF16)
    dec_all = jnp.exp(cs_end)

    heads = range(N_HEADS)
    groups = range(N_HEADS // 4)
    sls = [slice(HEAD * h, HEAD * (h + 1)) for h in heads]
    bg = [bm[:, SSD_STATE * g:SSD_STATE * (g + 1)] for g in groups]
    cg = [cm[:, SSD_STATE * g:SSD_STATE * (g + 1)] for g in groups]
    cb = [_nt(cg[g], bg[g]) for g in groups]
    s0 = [state_ref[h] for h in heads]
    lmat = [jnp.exp(jnp.where(lower_eq, cs[:, h:h + 1] - cs_t[h:h + 1, :], -jnp.inf)) for h in heads]
    y_in = [_mm((cb[h // 4] * lmat[h]).astype(BF16), xdt16[:, sls[h]]) for h in heads]
    y_st = [_nt(cg[h // 4], s0[h].astype(BF16)) for h in heads]
    new = [_tn(xdec16[:, sls[h]], bg[h // 4]) for h in heads]
    for h in heads:
        state_ref[h] = s0[h] * dec_all[:, HEAD * h:HEAD * h + 1] + new[h]
    y = (jnp.concatenate(y_in, axis=-1) + jnp.concatenate(y_st, axis=-1) * jnp.exp(cs_w) + x * dsk_ref[...])
    y = y * _silu(z)
    half = GW // 2
    y = jnp.concatenate([_rms(y[:, :half], SSD_NORM_EPS), _rms(y[:, half:], SSD_NORM_EPS)], axis=-1)
    o_ref[...] = (y * nw_ref[...]).astype(o_ref.dtype)


def _ssd(p, prm, bsz, seq):
    c = SSD_CHUNK
    nc = seq // c
    nseq = math.gcd(bsz, SSD_SEQS)
    col = SEC_S // SEC_W
    vec = lambda width: pl.BlockSpec((1, width), lambda b, n: (0, 0))
    out = pl.pallas_call(
        _ssd_kernel,
        grid=(bsz // nseq, nc),
        in_specs=[
            pl.BlockSpec((nseq, c, SEC_W), lambda b, n: (b, n, col)),
            pl.BlockSpec((SSD_CONV, SSD_CONV_CH), lambda b, n: (0, 0)),
            vec(SSD_CONV_CH), vec(128), vec(128), vec(GW), vec(GW),
            pl.BlockSpec((128, GW), lambda b, n: (0, 0)),
        ],
        out_specs=pl.BlockSpec((nseq, c, GW), lambda b, n: (b, n, 0)),
        out_shape=jax.ShapeDtypeStruct((bsz, seq, GW), BF16),
        scratch_shapes=[pltpu.VMEM((nseq, 8, SSD_CONV_CH), F32),
                        pltpu.VMEM((nseq, N_HEADS, HEAD, SSD_STATE), F32)],
        compiler_params=_params(("parallel", "arbitrary")),
        name="mamba2_ssd",
    )(p.reshape(bsz, seq, P_WIDTH), *prm)
    return out.reshape(bsz * seq, GW)


def _ssd_params(conv_w, conv_b, dt_bias, a_log, d_skip, norm_w):
    pad = lambda t: jnp.zeros((1, 128), F32).at[0, :N_HEADS].set(t)
    spread = (jnp.arange(128)[:, None] == jnp.arange(GW)[None, :] // HEAD).astype(BF16)
    return (conv_w, conv_b.reshape(1, -1), pad(dt_bias), pad(-jnp.exp(a_log)),
            jnp.repeat(d_skip, HEAD).reshape(1, -1), norm_w.reshape(1, -1), spread)


def _gelu_tanh(x):
    return 0.5 * x * (1.0 + jnp.tanh(math.sqrt(2.0 / math.pi) * (x + 0.044715 * (x * x * x))))


def _s5_kernel(u_ref, bbr_ref, bbi_ref, lvr_ref, lvi_ref, pwr_ref, pwi_ref, ccr_ref, cci_ref,
               dsk_ref, gw_ref, gb_ref, o_ref, hr_ref, hi_ref, carry_ref):
    @pl.when(pl.program_id(1) == 0)
    def _():
        carry_ref[...] = jnp.zeros_like(carry_ref)

    u = u_ref[...].astype(F32)
    u16 = u.astype(BF16)
    q_in, q_st = GW // S5_QUADS, S5_LANES // S5_QUADS
    for q in range(S5_QUADS):
        uq = u16[:, q * q_in:(q + 1) * q_in]
        hr_ref[:, q * q_st:(q + 1) * q_st] = _mm(uq, bbr_ref[q])
        hi_ref[:, q * q_st:(q + 1) * q_st] = _mm(uq, bbi_ref[q])

    def tile(t, carry):
        cr, ci = carry
        rows = pl.ds(pl.multiple_of(t * S5_TILE, S5_TILE), S5_TILE)
        xr = hr_ref[rows, :]
        xi = hi_ref[rows, :]
        for lvl, s in enumerate((1, 2, 4)):
            lr = lvr_ref[lvl]
            li = lvi_ref[lvl]
            sr = pltpu.roll(xr, s, axis=0)
            si = pltpu.roll(xi, s, axis=0)
            xr, xi = xr + lr * sr - li * si, xi + lr * si + li * sr
        pr = pwr_ref[...]
        pi = pwi_ref[...]
        xr, xi = xr + pr * cr - pi * ci, xi + pr * ci + pi * cr
        hr_ref[rows, :] = xr
        hi_ref[rows, :] = xi
        last = S5_TILE - 1
        return (jnp.broadcast_to(xr[last:last + 1, :], xr.shape), jnp.broadcast_to(xi[last:last + 1, :], xi.shape))

    cr, ci = lax.fori_loop(0, u.shape[0] // S5_TILE, tile, (carry_ref[0], carry_ref[1]))
    carry_ref[0] = cr
    carry_ref[1] = ci

    y = jnp.concatenate(
        [_mm(hr_ref[:, q * q_st:(q + 1) * q_st].astype(BF16), ccr_ref[q])
         - _mm(hi_ref[:, q * q_st:(q + 1) * q_st].astype(BF16), cci_ref[q]) for q in range(S5_QUADS)], axis=-1)
    y = _gelu_tanh(y + dsk_ref[...] * u)
    gate = jax.nn.sigmoid(_mm(y.astype(BF16), gw_ref[...]) + gb_ref[...])
    o_ref[...] = (y * gate).astype(o_ref.dtype)


def _s5(p, prm, bsz, seq):
    c = S5_CHUNK
    nc = seq // c
    col = SEC_F // GW
    full = lambda shape: pl.BlockSpec(shape, lambda b, n: (0,) * len(shape))
    return pl.pallas_call(
        _s5_kernel,
        grid=(bsz, nc),
        in_specs=[
            pl.BlockSpec((c, GW), lambda b, n: (b * nc + n, col)),
            full((S5_QUADS, GW // S5_QUADS, S5_LANES // S5_QUADS)),
            full((S5_QUADS, GW // S5_QUADS, S5_LANES // S5_QUADS)),
            full((3, S5_TILE, S5_LANES)), full((3, S5_TILE, S5_LANES)),
            full((S5_TILE, S5_LANES)), full((S5_TILE, S5_LANES)),
            full((S5_QUADS, S5_LANES // S5_QUADS, GW // S5_QUADS)),
            full((S5_QUADS, S5_LANES // S5_QUADS, GW // S5_QUADS)),
            full((1, GW)), full((GW, GW)), full((1, GW)),
        ],
        out_specs=pl.BlockSpec((c, GW), lambda b, n: (b * nc + n, 0)),
        out_shape=jax.ShapeDtypeStruct((bsz * seq, GW), BF16),
        scratch_shapes=[pltpu.VMEM((c, S5_LANES), F32), pltpu.VMEM((c, S5_LANES), F32),
                        pltpu.VMEM((2, S5_TILE, S5_LANES), F32)],
        compiler_params=_params(("parallel", "arbitrary")),
        name="s5_scan",
    )(p, *prm)


def _s5_params(lam_re, lam_im, log_step, b_re, b_im, c_re, c_im, d_skip, glu_w, glu_b):
    step = jnp.exp(log_step)[:, None]
    mag = jnp.exp(lam_re * step)
    ab_re, ab_im = mag * jnp.cos(lam_im * step), mag * jnp.sin(lam_im * step)
    den = lam_re * lam_re + lam_im * lam_im
    coef_re = ((ab_re - 1) * lam_re + ab_im * lam_im) / den
    coef_im = (ab_im * lam_re - (ab_re - 1) * lam_im) / den
    bb_re = coef_re[..., None] * b_re - coef_im[..., None] * b_im
    bb_im = coef_re[..., None] * b_im + coef_im[..., None] * b_re
    gq = S5_GROUPS // S5_QUADS
    eye = jnp.eye(gq, dtype=F32)
    in_proj = lambda t: jnp.einsum(
        'qgni,gh->qgihn', t.reshape(S5_QUADS, gq, S5_STATE, S5_GROUP_CH), eye
    ).reshape(S5_QUADS, gq * S5_GROUP_CH, gq * S5_STATE).astype(BF16)
    out_proj = lambda t: jnp.einsum(
        'qgin,gh->qgnhi', t.reshape(S5_QUADS, gq, S5_GROUP_CH, S5_STATE), eye
    ).reshape(S5_QUADS, gq * S5_STATE, gq * S5_GROUP_CH).astype(BF16)

    def cmul(a, b):
        return a[0] * b[0] - a[1] * b[1], a[0] * b[1] + a[1] * b[0]

    lam1 = (ab_re.reshape(-1), ab_im.reshape(-1))
    lam2 = cmul(lam1, lam1)
    lam4 = cmul(lam2, lam2)
    rows = jnp.arange(S5_TILE)[:, None]
    lvl_re = jnp.stack([jnp.where(rows >= s, l[0][None, :], 0.0) for s, l in ((1, lam1), (2, lam2), (4, lam4))])
    lvl_im = jnp.stack([jnp.where(rows >= s, l[1][None, :], 0.0) for s, l in ((1, lam1), (2, lam2), (4, lam4))])
    powers = [lam1]
    for _ in range(S5_TILE - 1):
        powers.append(cmul(powers[-1], lam1))
    pw_re = jnp.stack([q[0] for q in powers])
    pw_im = jnp.stack([q[1] for q in powers])
    return (in_proj(bb_re), in_proj(bb_im), lvl_re, lvl_im, pw_re, pw_im, out_proj(c_re), out_proj(c_im),
            d_skip.reshape(1, -1), glu_w.astype(BF16), glu_b.reshape(1, -1))


def _outproj_kernel(ya_ref, yr_ref, ys_ref, yf_ref, w_ref, x_ref, g_ref, nw_ref, o_ref):
    acc = _mm(ya_ref[...], w_ref[0, 0:GW, :])
    acc = acc + _mm(yr_ref[...], w_ref[0, GW:2 * GW, :])
    acc = acc + _mm(ys_ref[...], w_ref[0, 2 * GW:3 * GW, :])
    acc = acc + _mm(yf_ref[...], w_ref[0, 3 * GW:4 * GW, :])
    o_ref[...] = x_ref[...] + g_ref[0] * (_rms(acc, NORM_EPS) * nw_ref[...])


def _out_projection(ys, w, layer, x2, gate, nw, seq):
    t = x2.shape[0]
    tm = 512
    per_b = seq // tm
    ymap = pl.BlockSpec((tm, GW), lambda i: (i, 0))
    return pl.pallas_call(
        _outproj_kernel,
        grid=(t // tm,),
        in_specs=[
            ymap, ymap, ymap, ymap,
            pl.BlockSpec((1, D_MODEL, D_MODEL), lambda i: (layer, 0, 0)),
            pl.BlockSpec((tm, D_MODEL), lambda i: (i, 0)),
            pl.BlockSpec((1, 1, D_MODEL), lambda i: (i // per_b, 0, 0)),
            pl.BlockSpec((1, D_MODEL), lambda i: (0, 0)),
        ],
        out_specs=pl.BlockSpec((tm, D_MODEL), lambda i: (i, 0)),
        out_shape=jax.ShapeDtypeStruct((t, D_MODEL), F32),
        compiler_params=_params(("parallel",)),
        name="out_projection",
    )(*ys, w, x2, gate, nw)


def _ffn_kernel(x_ref, xn_ref, npre_ref, sc_ref, sh_ref, scn_ref, shn_ref, wg_ref, wu_ref, wd_ref, g_ref,
                npost_ref, o_ref, h_ref, acc_ref):
    i = pl.program_id(0)
    f = pl.program_id(1)
    last = pl.num_programs(1) - 1
    slot = i % 2

    @pl.when((i == 0) & (f == 0))
    def _():
        h_ref[0] = _prenorm(x_ref[...], npre_ref[...], sc_ref[0], sh_ref[0]).astype(BF16)

    def partial():
        h = h_ref[slot]
        act = _silu(_mm(h, wg_ref[0])) * _mm(h, wu_ref[0])
        return _mm(act.astype(BF16), wd_ref[0])

    @pl.when(f == 0)
    def _():
        acc_ref[...] = partial()

    @pl.when((f > 0) & (f < last))
    def _():
        acc_ref[...] += partial()

    @pl.when(f == last)
    def _():
        acc = acc_ref[...] + partial()
        h_ref[1 - slot] = _prenorm(xn_ref[...], npre_ref[...], scn_ref[0], shn_ref[0]).astype(BF16)
        o_ref[...] = x_ref[...] + g_ref[0] * (_rms(acc, NORM_EPS) * npost_ref[...])


def _dense_ffn(x2, npre, scale, shift, wg, wu, wd, layer, gate, npost, seq):
    t = x2.shape[0]
    tm, tf = 512, 512
    per_b = seq // tm
    n_i = t // tm
    assert D_FF // tf >= 3
    nxt = lambda i: jnp.minimum(i + 1, n_i - 1)
    mod = pl.BlockSpec((1, 1, D_MODEL), lambda i, f: (i // per_b, 0, 0))
    mod_next = pl.BlockSpec((1, 1, D_MODEL), lambda i, f: (nxt(i) // per_b, 0, 0))
    vec = pl.BlockSpec((1, D_MODEL), lambda i, f: (0, 0))
    return pl.pallas_call(
        _ffn_kernel,
        grid=(n_i, D_FF // tf),
        in_specs=[
            pl.BlockSpec((tm, D_MODEL), lambda i, f: (i, 0)),
            pl.BlockSpec((tm, D_MODEL), lambda i, f: (nxt(i), 0)),
            vec, mod, mod, mod_next, mod_next,
            pl.BlockSpec((1, D_MODEL, tf), lambda i, f: (layer, 0, f)),
            pl.BlockSpec((1, D_MODEL, tf), lambda i, f: (layer, 0, f)),
            pl.BlockSpec((1, tf, D_MODEL), lambda i, f: (layer, f, 0)),
            mod, vec,
        ],
        out_specs=pl.BlockSpec((tm, D_MODEL), lambda i, f: (i, 0)),
        out_shape=jax.ShapeDtypeStruct((t, D_MODEL), F32),
        scratch_shapes=[pltpu.VMEM((2, tm, D_MODEL), BF16), pltpu.VMEM((tm, D_MODEL), F32)],
        compiler_params=_params(("arbitrary", "arbitrary")),
        name="dense_swiglu",
    )(x2, x2, npre, scale, shift, scale, shift, wg, wu, wd, gate, npost)


def _router_kernel(x_ref, npre_ref, sc_ref, sh_ref, rw_ref, rb_ref, h_ref, lg_ref):
    h = _prenorm(x_ref[...], npre_ref[...], sc_ref[0], sh_ref[0])
    h_ref[...] = h.astype(BF16)
    lg_ref[...] = _mm(h, rw_ref[...], precision=HIGHEST) + rb_ref[...]


def _router(x2, npre, scale, shift, rw, rb, seq):
    t = x2.shape[0]
    tm = 512
    per_b = seq // tm
    mod = pl.BlockSpec((1, 1, D_MODEL), lambda i: (i // per_b, 0, 0))
    return pl.pallas_call(
        _router_kernel,
        grid=(t // tm,),
        in_specs=[
            pl.BlockSpec((tm, D_MODEL), lambda i: (i, 0)),
            pl.BlockSpec((1, D_MODEL), lambda i: (0, 0)),
            mod, mod,
            pl.BlockSpec((D_MODEL, 128), lambda i: (0, 0)),
            pl.BlockSpec((1, 128), lambda i: (0, 0)),
        ],
        out_specs=[pl.BlockSpec((tm, D_MODEL), lambda i: (i, 0)), pl.BlockSpec((tm, 128), lambda i: (i, 0))],
        out_shape=[jax.ShapeDtypeStruct((t, D_MODEL), BF16), jax.ShapeDtypeStruct((t, 128), F32)],
        compiler_params=_params(("parallel",)),
        name="moe_router",
    )(x2, npre, scale, shift, rw, rb)


def _moe_ffn_kernel(be_ref, used_ref, xs_ref, wg_ref, wu_ref, wd_ref, o_ref, acc_ref, *, nf):
    i = pl.program_id(0)
    f = pl.program_id(1)
    last = nf - 1
    live = i < used_ref[0]

    def partial():
        xs = xs_ref[...]
        act = _silu(_mm(xs, wg_ref[0, 0])) * _mm(xs, wu_ref[0, 0])
        return _mm(act.astype(BF16), wd_ref[0, 0])

    @pl.when(live & (f == 0))
    def _():
        acc_ref[...] = partial()

    if nf > 2:
        @pl.when(live & (f > 0) & (f < last))
        def _():
            acc_ref[...] += partial()

    @pl.when(live & (f == last))
    def _():
        o_ref[...] = (acc_ref[...] + partial()).astype(o_ref.dtype)

    @pl.when(jnp.logical_not(live) & (f == last))
    def _():
        o_ref[...] = jnp.zeros_like(o_ref)


def _moe_ffn(block_e, n_used, xs, wg, wu, wd, layer):
    rows = xs.shape[0]
    tf = D_EXPERT // 2
    nf = D_EXPERT // tf

    def tile(i, f, used):
        return jnp.where(i < used[0], f, nf - 1)

    grid_spec = pltpu.PrefetchScalarGridSpec(
        num_scalar_prefetch=2,
        grid=(rows // MOE_BLOCK, nf),
        in_specs=[
            pl.BlockSpec((MOE_BLOCK, D_MODEL), lambda i, f, be, used: (i, 0)),
            pl.BlockSpec((1, 1, D_MODEL, tf), lambda i, f, be, used: (layer, be[i], 0, tile(i, f, used))),
            pl.BlockSpec((1, 1, D_MODEL, tf), lambda i, f, be, used: (layer, be[i], 0, tile(i, f, used))),
            pl.BlockSpec((1, 1, tf, D_MODEL), lambda i, f, be, used: (layer, be[i], tile(i, f, used), 0)),
        ],
        out_specs=pl.BlockSpec((MOE_BLOCK, D_MODEL), lambda i, f, be, used: (i, 0)),
        scratch_shapes=[pltpu.VMEM((MOE_BLOCK, D_MODEL), F32)],
    )
    assert nf >= 2
    return pl.pallas_call(
        functools.partial(_moe_ffn_kernel, nf=nf),
        grid_spec=grid_spec,
        out_shape=jax.ShapeDtypeStruct((rows, D_MODEL), BF16),
        compiler_params=_params(("parallel", "arbitrary"), MOE_VMEM_LIMIT),
        name="moe_swiglu",
    )(block_e, n_used, xs, wg, wu, wd)


def _combine_kernel(y0_ref, y1_ref, p_ref, x_ref, g_ref, npost_ref, o_ref):
    p = p_ref[...]
    y = y0_ref[...].astype(F32) * p[:, 0:1] + y1_ref[...].astype(F32) * p[:, 1:2]
    o_ref[...] = x_ref[...] + g_ref[0] * (_rms(y, NORM_EPS) * npost_ref[...])


def _moe_combine(y0, y1, top_p, x2, gate, npost, seq):
    t = x2.shape[0]
    tm = 512
    per_b = seq // tm
    row = pl.BlockSpec((tm, D_MODEL), lambda i: (i, 0))
    return pl.pallas_call(
        _combine_kernel,
        grid=(t // tm,),
        in_specs=[row, row, pl.BlockSpec((tm, 128), lambda i: (i, 0)), row,
                  pl.BlockSpec((1, 1, D_MODEL), lambda i: (i // per_b, 0, 0)),
                  pl.BlockSpec((1, D_MODEL), lambda i: (0, 0))],
        out_specs=row,
        out_shape=jax.ShapeDtypeStruct((t, D_MODEL), F32),
        compiler_params=_params(("parallel",)),
        name="moe_combine",
    )(y0, y1, top_p, x2, gate, npost)


def _routed_ffn(x2, npre, scale, shift, rw, rb, wg, wu, wd, layer, gate, npost, seq):
    t = x2.shape[0]
    rw_pad = jnp.zeros((D_MODEL, 128), F32).at[:, :N_EXPERTS].set(rw)
    rb_pad = jnp.zeros((1, 128), F32).at[0, :N_EXPERTS].set(rb)
    h, logits = _router(x2, npre, scale, shift, rw_pad, rb_pad, seq)
    logits = logits[:, :N_EXPERTS]
    top_logit, top_idx = lax.top_k(logits, TOP_K)
    top_p = jax.nn.softmax(top_logit, axis=-1)
    n_assign = t * TOP_K
    flat_e = top_idx.reshape(-1).astype(jnp.int32)
    order = jnp.argsort(flat_e).astype(jnp.int32)
    rank = jnp.argsort(order).astype(jnp.int32)
    counts = jnp.sum((flat_e[:, None] == jnp.arange(N_EXPERTS, dtype=jnp.int32)[None, :]).astype(jnp.int32), axis=0)
    padded = (counts + MOE_BLOCK - 1) // MOE_BLOCK * MOE_BLOCK
    pad_end = jnp.cumsum(padded)
    pad_start = pad_end - padded
    start = jnp.cumsum(counts) - counts
    n_blocks = -(-n_assign // MOE_BLOCK) + N_EXPERTS
    rows = n_blocks * MOE_BLOCK
    block_e = jnp.minimum(jnp.searchsorted(pad_end, jnp.arange(n_blocks) * MOE_BLOCK, side='right'),
                          N_EXPERTS - 1).astype(jnp.int32)
    e_row = jnp.repeat(block_e, MOE_BLOCK)
    off = jnp.arange(rows, dtype=jnp.int32) - pad_start[e_row]
    src = order[jnp.clip(start[e_row] + off, 0, n_assign - 1)]
    row_tok = jnp.where(off < counts[e_row], src // TOP_K, t)
    h_pad = jnp.concatenate([h, jnp.zeros((1, D_MODEL), h.dtype)], axis=0)
    xs = h_pad[row_tok]
    n_used = (pad_end[-1:] // MOE_BLOCK).astype(jnp.int32)
    yb = _moe_ffn(block_e, n_used, xs, wg, wu, wd, layer)
    pos = (pad_start[flat_e] + rank - start[flat_e]).reshape(t, TOP_K)
    p_pad = jnp.zeros((t, 128), F32).at[:, :TOP_K].set(top_p)
    return _moe_combine(yb[pos[:, 0]], yb[pos[:, 1]], p_pad, x2, gate, npost, seq)


def _in_weights(w_in):
    a0, r0, s0, f0 = 0, ATTN_IN, ATTN_IN + RWKV_IN, ATTN_IN + RWKV_IN + SSD_IN
    zeros = jnp.zeros(w_in.shape[:-1] + (SEC_W - SSD_IN,), w_in.dtype)
    return jnp.concatenate([
        w_in[..., r0:s0],
        w_in[..., s0:f0], zeros,
        w_in[..., f0:f0 + GW],
        w_in[..., a0:r0],
    ], axis=-1).astype(BF16)


def kernel(x, c, ada_w, ada_b, norm_pre, norm_post, w_in, w_out, attn_sink, rwkv_mu, rwkv_w0, rwkv_w_up, rwkv_a0, rwkv_a_up, rwkv_g_up, rwkv_k_k, rwkv_k_a, rwkv_r_k, rwkv_ln_w, rwkv_ln_b, ssd_conv_w, ssd_conv_b, ssd_dt_bias, ssd_a_log, ssd_d, ssd_norm_w, s5_lam_re, s5_lam_im, s5_log_step, s5_b_re, s5_b_im, s5_c_re, s5_c_im, s5_d, s5_glu_w, s5_glu_b, ffn_w_gate, ffn_w_up, ffn_w_down, moe_router_w, moe_router_b, moe_w_gate, moe_w_up, moe_w_down):
    bsz, seq, d = x.shape
    depth = ada_w.shape[0]
    mod = _ada_modulation(c, ada_w, ada_b)
    mod = mod.reshape(depth, 2, bsz, 3, 1, d)
    x2 = x.reshape(bsz * seq, d)
    w_in16 = _in_weights(w_in)
    w_out16 = w_out.astype(BF16)
    ffn16 = (ffn_w_gate.astype(BF16), ffn_w_up.astype(BF16), ffn_w_down.astype(BF16))
    moe16 = (moe_w_gate.astype(BF16), moe_w_up.astype(BF16), moe_w_down.astype(BF16))
    for i in range(depth):
        shift, scale, gate = mod[i, 0, :, 0], mod[i, 0, :, 1], mod[i, 0, :, 2]
        p = _in_projection(x2, norm_pre[i, 0][None], scale, shift, w_in16, i, seq)
        y_attn = _attention(p, attn_sink[i][None], bsz, seq)
        y_rwkv = _rwkv(p, _rwkv_params(rwkv_mu[i], rwkv_w0[i], rwkv_w_up[i], rwkv_a0[i], rwkv_a_up[i],
                                       rwkv_g_up[i], rwkv_k_k[i], rwkv_k_a[i], rwkv_r_k[i].reshape(-1),
                                       rwkv_ln_w[i], rwkv_ln_b[i]), bsz, seq)
        y_ssd = _ssd(p, _ssd_params(ssd_conv_w[i], ssd_conv_b[i], ssd_dt_bias[i], ssd_a_log[i], ssd_d[i],
                                    ssd_norm_w[i]), bsz, seq)
        y_s5 = _s5(p, _s5_params(s5_lam_re[i], s5_lam_im[i], s5_log_step[i], s5_b_re[i], s5_b_im[i],
                                 s5_c_re[i], s5_c_im[i], s5_d[i], s5_glu_w[i], s5_glu_b[i]), bsz, seq)
        x2 = _out_projection((y_attn, y_rwkv, y_ssd, y_s5), w_out16, i, x2, gate, norm_post[i, 0][None], seq)
        shift, scale, gate = mod[i, 1, :, 0], mod[i, 1, :, 1], mod[i, 1, :, 2]
        j = i // 2
        if i % 2 == 0:
            x2 = _dense_ffn(x2, norm_pre[i, 1][None], scale, shift, *ffn16, j, gate, norm_post[i, 1][None], seq)
        else:
            x2 = _routed_ffn(x2, norm_pre[i, 1][None], scale, shift, moe_router_w[j], moe_router_b[j],
                             *moe16, j, gate, norm_post[i, 1][None], seq)
    return x2.reshape(bsz, seq, d)
```

```python
import functools
import math

import jax
import jax.numpy as jnp
from jax import lax
from jax.experimental import pallas as pl
from jax.experimental.pallas import tpu as pltpu

F32 = jnp.float32
BF16 = jnp.bfloat16
HIGHEST = lax.Precision.HIGHEST

D_MODEL = 2048
DEPTH = 4
GW = 512
NORM_EPS = 1e-6
HEAD = 64
N_HEADS = 8

ATTN_KV_HEADS = 2
ATTN_GROUP = 4
ATTN_BLOCK = 128
ATTN_IN = 768
ATTN_BLOCKS_PER_STEP = 4

RWKV_IN = 1792
RWKV_LN_EPS = 64e-5
RWKV_CHUNK = 64
RWKV_SEQS = 4

SSD_STATE = 128
SSD_CONV = 4
SSD_CHUNK = 128
SSD_SEQS = 4
SSD_CONV_CH = 1024
SSD_IN = 1544
SSD_NORM_EPS = 1e-5

S5_GROUPS = 32
S5_GROUP_CH = 16
S5_STATE = 64
S5_LANES = S5_GROUPS * S5_STATE
S5_CHUNK = 512
S5_TILE = 8
S5_QUADS = 4

D_FF = 5632
N_EXPERTS = 8
TOP_K = 2
D_EXPERT = 2816
MOE_BLOCK = 512

SEC_R = 0
SEC_S = 1792
SEC_F = 3584
SEC_Q = 4096
SEC_KV = 4608
P_WIDTH = 4864
SEC_W = 1792

VMEM_LIMIT = 52 * 1024 * 1024
MOE_VMEM_LIMIT = 58 * 1024 * 1024


def _params(sem, vmem=VMEM_LIMIT):
    return pltpu.CompilerParams(dimension_semantics=sem, vmem_limit_bytes=vmem)


def _nt(a, b, **kw):
    return lax.dot_general(a, b, (((1,), (1,)), ((), ())), preferred_element_type=F32, **kw)


def _tn(a, b, **kw):
    return lax.dot_general(a, b, (((0,), (0,)), ((), ())), preferred_element_type=F32, **kw)


def _mm(a, b, **kw):
    return jnp.dot(a, b, preferred_element_type=F32, **kw)


def _split_mm(a, x, parts):
    acc = None
    for _ in range(parts):
        piece = x.astype(BF16)
        term = _mm(a, piece)
        acc = term if acc is None else acc + term
        x = x - piece.astype(F32)
    return acc


def _split_mm_rhs(x, b, parts):
    acc = None
    for _ in range(parts):
        piece = x.astype(BF16)
        term = _mm(piece, b)
        acc = term if acc is None else acc + term
        x = x - piece.astype(F32)
    return acc


def _softplus(x):
    return jnp.maximum(x, 0.0) + jnp.log1p(jnp.exp(-jnp.abs(x)))


def _silu(x):
    return x * jax.nn.sigmoid(x)


def _rms(x, eps):
    return x * lax.rsqrt(jnp.mean(x * x, axis=-1, keepdims=True) + eps)


def _ada_kernel(c_ref, w_ref, b_ref, o_ref):
    c = c_ref[...]
    o_ref[0] = _mm(_silu(c).astype(BF16), w_ref[0].astype(BF16)) + b_ref[0]


def _ada_modulation(c, ada_w, ada_b):
    bsz = c.shape[0]
    rows = 8 * pl.cdiv(bsz, 8)
    c_pad = jnp.zeros((rows, D_MODEL), F32).at[:bsz].set(c)
    n_mod = ada_w.shape[0] * 2
    w = ada_w.reshape(n_mod, D_MODEL, 3 * D_MODEL)
    b = ada_b.reshape(n_mod, 1, 3 * D_MODEL)
    tn = 768
    out = pl.pallas_call(
        _ada_kernel,
        grid=(n_mod, 3 * D_MODEL // tn),
        in_specs=[
            pl.BlockSpec((rows, D_MODEL), lambda m, j: (0, 0)),
            pl.BlockSpec((1, D_MODEL, tn), lambda m, j: (m, 0, j)),
            pl.BlockSpec((1, 1, tn), lambda m, j: (m, 0, j)),
        ],
        out_specs=pl.BlockSpec((1, rows, tn), lambda m, j: (m, 0, j)),
        out_shape=jax.ShapeDtypeStruct((n_mod, rows, 3 * D_MODEL), F32),
        compiler_params=_params(("parallel", "parallel")),
        name="ada_modulation",
    )(c_pad, w, b)
    return out[:, :bsz]


def _prenorm(x, nw, scale, shift):
    return (_rms(x, NORM_EPS) * nw) * (1.0 + scale) + shift


def _inproj_kernel(x0_ref, xn_ref, nw_ref, sc0_ref, sh0_ref, scn_ref, shn_ref, w_ref, o_ref, h_ref):
    i = pl.program_id(0)
    slot = i % 2

    @pl.when(i == 0)
    def _():
        h_ref[0] = _prenorm(x0_ref[...], nw_ref[...], sc0_ref[0], sh0_ref[0]).astype(BF16)

    o_ref[...] = _mm(h_ref[slot], w_ref[0]).astype(o_ref.dtype)
    h_ref[1 - slot] = _prenorm(xn_ref[...], nw_ref[...], scn_ref[0], shn_ref[0]).astype(BF16)


def _in_projection(x2, nw, scale, shift, w, layer, seq):
    t = x2.shape[0]
    tm = 512
    per_b = seq // tm
    n_i = t // tm
    nxt = lambda i: jnp.minimum(i + 1, n_i - 1)
    first_mod = pl.BlockSpec((1, 1, D_MODEL), lambda i: (0, 0, 0))
    next_mod = pl.BlockSpec((1, 1, D_MODEL), lambda i: (nxt(i) // per_b, 0, 0))
    return pl.pallas_call(
        _inproj_kernel,
        grid=(n_i,),
        in_specs=[
            pl.BlockSpec((tm, D_MODEL), lambda i: (0, 0)),
            pl.BlockSpec((tm, D_MODEL), lambda i: (nxt(i), 0)),
            pl.BlockSpec((1, D_MODEL), lambda i: (0, 0)),
            first_mod, first_mod, next_mod, next_mod,
            pl.BlockSpec((1, D_MODEL, P_WIDTH), lambda i: (layer, 0, 0), pipeline_mode=pl.Buffered(1)),
        ],
        out_specs=pl.BlockSpec((tm, P_WIDTH), lambda i: (i, 0)),
        out_shape=jax.ShapeDtypeStruct((t, P_WIDTH), BF16),
        scratch_shapes=[pltpu.VMEM((2, tm, D_MODEL), BF16)],
        compiler_params=_params(("arbitrary",)),
        name="in_projection",
    )(x2, x2, nw, scale, shift, scale, shift, w)


def _alibi_slope(h):
    return 2.0 ** (-8.0 * (h + 1) / N_HEADS)


def _attn_kernel(sink_ref, q_ref, cur_ref, prev_ref, o_ref):
    n = pl.program_id(1)
    blk = ATTN_BLOCK
    nblk = q_ref.shape[0] // blk
    q = q_ref[...]
    kv = jnp.concatenate([prev_ref[...], cur_ref[...]], axis=0).astype(BF16)
    qi = lax.broadcasted_iota(jnp.int32, (blk, 2 * blk), 0) + blk
    kj = lax.broadcasted_iota(jnp.int32, (blk, 2 * blk), 1)
    dist = qi - kj
    window = (dist >= 0) & (dist < blk)
    first = window & ((kj >= blk) | (n > 0))
    distf = dist.astype(F32)
    units = [(i, h) for i in range(nblk) for h in range(N_HEADS)]
    kg = {(i, g): kv[i * blk:(i + 2) * blk, HEAD * g:HEAD * (g + 1)]
          for i in range(nblk) for g in range(ATTN_KV_HEADS)}
    vg = {(i, g): kv[i * blk:(i + 2) * blk, 128 + HEAD * g:128 + HEAD * (g + 1)]
          for i in range(nblk) for g in range(ATTN_KV_HEADS)}
    s = {(i, h): _nt(q[i * blk:(i + 1) * blk, HEAD * h:HEAD * (h + 1)].astype(BF16), kg[i, h // ATTN_GROUP])
         for i, h in units}
    s = {(i, h): jnp.where(first if i == 0 else window,
                           s[i, h] * (HEAD ** -0.5) - _alibi_slope(h) * distf, -jnp.inf) for i, h in units}
    m = {(i, h): jnp.maximum(jnp.max(s[i, h], axis=-1, keepdims=True), sink_ref[0, h]) for i, h in units}
    p = {u: jnp.exp(s[u] - m[u]) for u in units}
    denom = {(i, h): jnp.sum(p[i, h], axis=-1, keepdims=True) + jnp.exp(sink_ref[0, h] - m[i, h]) for i, h in units}
    out = {(i, h): _mm(p[i, h].astype(BF16), vg[i, h // ATTN_GROUP]) / denom[i, h] for i, h in units}
    rows = [jnp.concatenate([out[i, h] for h in range(N_HEADS)], axis=-1) for i in range(nblk)]
    o_ref[...] = jnp.concatenate(rows, axis=0).astype(o_ref.dtype)


def _attention(p, sinks, bsz, seq):
    rows = ATTN_BLOCK * ATTN_BLOCKS_PER_STEP
    steps = seq // rows
    nb = seq // ATTN_BLOCK
    q_col = SEC_Q // GW
    kv_col = SEC_KV // 256
    return pl.pallas_call(
        _attn_kernel,
        grid=(bsz, steps),
        in_specs=[
            pl.BlockSpec(memory_space=pltpu.SMEM),
            pl.BlockSpec((rows, GW), lambda b, n: (b * steps + n, q_col)),
            pl.BlockSpec((rows, 256), lambda b, n: (b * steps + n, kv_col)),
            pl.BlockSpec((ATTN_BLOCK, 256),
                         lambda b, n: (b * nb + jnp.maximum(n * ATTN_BLOCKS_PER_STEP - 1, 0), kv_col)),
        ],
        out_specs=pl.BlockSpec((rows, GW), lambda b, n: (b * steps + n, 0)),
        out_shape=jax.ShapeDtypeStruct((bsz * seq, GW), BF16),
        compiler_params=_params(("parallel", "arbitrary")),
        name="swa_attention",
    )(sinks, p, p, p)


def _rwkv_kernel(p_ref, mu_ref, w0_ref, wup_ref, a0_ref, aup_ref, gup_ref, kk_ref, ka_ref,
                 rk_ref, lnw_ref, lnb_ref, hsum_ref, o_ref, last_ref, state_ref):
    c = RWKV_CHUNK
    nseq = p_ref.shape[0]
    rows = nseq * c

    @pl.when(pl.program_id(1) == 0)
    def _():
        last_ref[...] = jnp.zeros_like(last_ref)
        state_ref[...] = jnp.zeros_like(state_ref)

    p = p_ref[...].astype(F32).reshape(rows, RWKV_IN)
    row = lax.broadcasted_iota(jnp.int32, (rows, 1), 0)
    shifted = pltpu.roll(p, 1, axis=0)
    for j in range(nseq):
        shifted = jnp.where(row == j * c, last_ref[j:j + 1, :], shifted)
        last_ref[j:j + 1, :] = p[(j + 1) * c - 1:(j + 1) * c, :]
    pm = p + mu_ref[...] * (shifted - p)

    r = pm[:, 0:GW]
    k = pm[:, GW:2 * GW]
    v = pm[:, 2 * GW:3 * GW]
    wa = pm[:, 3 * GW:3 * GW + 128]
    gd = pm[:, 3 * GW + 128:3 * GW + 256]

    w = -_softplus(-(w0_ref[...] + _mm(jnp.tanh(wa).astype(BF16), wup_ref[...]))) - 0.5
    logd = -jnp.exp(w)
    a = jax.nn.sigmoid(a0_ref[...] + _mm(wa.astype(BF16), aup_ref[...]))
    g = _mm(jax.nn.sigmoid(gd).astype(BF16), gup_ref[...])

    kk = k * kk_ref[...]
    sumsq = _split_mm_rhs(kk * kk, hsum_ref[...], 2)
    kk = kk / jnp.maximum(jnp.sqrt(sumsq), 1e-12)
    k = k * (1.0 + (a - 1.0) * ka_ref[...])
    b = kk * a

    rr = lax.broadcasted_iota(jnp.int32, (rows, rows), 0)
    cr = lax.broadcasted_iota(jnp.int32, (rows, rows), 1)
    tril_seq = ((rr >= cr) & (rr // c == cr // c)).astype(BF16)
    cum = _split_mm(tril_seq, logd, 3)
    g_inv = jnp.exp(-cum)
    at16 = ((-kk) * jnp.exp(cum - logd)).astype(BF16)
    bt = (b * g_inv).astype(BF16)
    kt = (k * g_inv).astype(BF16)
    rt = (r * jnp.exp(cum)).astype(BF16)
    v16 = v.astype(BF16)
    rk = r * k * rk_ref[...]

    ri = lax.broadcasted_iota(jnp.int32, (c, c), 0)
    ci = lax.broadcasted_iota(jnp.int32, (c, c), 1)
    lower = ri > ci
    eye = (ri == ci).astype(F32)
    ri2 = lax.broadcasted_iota(jnp.int32, (c, 2 * c), 0)
    ci2 = lax.broadcasted_iota(jnp.int32, (c, 2 * c), 1)
    k_half = ci2 >= c
    cj2 = jnp.where(k_half, ci2 - c, ci2)
    strict_k = k_half & (ri2 > cj2)
    lower_eq2 = ri2 >= cj2
    levels = []
    size = 1
    while size < c:
        levels.append((ri // (2 * size) == ci // (2 * size)) & (ri // size > ci // size))
        size *= 2

    units = [(j, h) for j in range(nseq) for h in range(N_HEADS)]
    rs = {u: slice(u[0] * c, (u[0] + 1) * c) for u in units}
    ls = {u: slice(HEAD * u[1], HEAD * (u[1] + 1)) for u in units}
    v_h = {u: v16[rs[u], ls[u]] for u in units}
    ar = {u: jnp.concatenate([at16[rs[u], ls[u]], rt[rs[u], ls[u]]], axis=0) for u in units}
    bk = {u: jnp.concatenate([bt[rs[u], ls[u]], kt[rs[u], ls[u]]], axis=0) for u in units}
    s0 = {u: state_ref[u[0], u[1]] for u in units}
    gram = {u: _nt(ar[u], bk[u]) for u in units}
    l_ab = {u: jnp.where(lower, gram[u][:c, :c], 0.0).astype(BF16) for u in units}
    l_ak = {u: jnp.where(strict_k, gram[u][:c, :], 0.0).astype(BF16) for u in units}
    m_r = {u: jnp.where(lower_eq2, gram[u][c:, :], 0.0).astype(BF16) for u in units}
    inv = {u: eye + jnp.where(levels[0], l_ab[u].astype(F32), 0.0) for u in units}
    for lvl in levels[1:]:
        inv16 = {u: inv[u].astype(BF16) for u in units}
        wl = {u: _mm(jnp.where(lvl, l_ab[u], jnp.zeros_like(l_ab[u])), inv16[u]).astype(BF16) for u in units}
        inv = {u: inv[u] + _mm(inv16[u], wl[u]) for u in units}
    sx = {u: _nt(ar[u], s0[u].astype(BF16)) for u in units}
    x = {u: sx[u][:c] + _mm(l_ak[u], jnp.concatenate([v_h[u], v_h[u]], axis=0)) for u in units}
    us = {u: _mm(inv[u].astype(BF16), x[u].astype(BF16)).astype(BF16) for u in units}
    uv = {u: jnp.concatenate([us[u], v_h[u]], axis=0) for u in units}
    y = {u: sx[u][c:] + _mm(m_r[u], uv[u]) for u in units}
    for u in units:
        g_end = jnp.exp(cum[rs[u].stop - 1:rs[u].stop, ls[u]])
        state_ref[u[0], u[1]] = (s0[u] + _tn(uv[u], bk[u])) * g_end
    outs = []
    for j in range(nseq):
        heads = []
        for h in range(N_HEADS):
            u = (j, h)
            mean = jnp.mean(y[u], axis=-1, keepdims=True)
            yc = y[u] - mean
            var = jnp.mean(yc * yc, axis=-1, keepdims=True)
            bonus = jnp.sum(rk[rs[u], ls[u]], axis=-1, keepdims=True) * v[rs[u], ls[u]]
            heads.append(yc * lax.rsqrt(var + RWKV_LN_EPS) * lnw_ref[:, ls[u]] + lnb_ref[:, ls[u]] + bonus)
        outs.append(jnp.concatenate(heads, axis=-1))
    out = jnp.concatenate(outs, axis=0) * g
    o_ref[...] = out.reshape(nseq, c, GW).astype(o_ref.dtype)


def _rwkv(p, prm, bsz, seq):
    c = RWKV_CHUNK
    nc = seq // c
    nseq = math.gcd(bsz, RWKV_SEQS)
    col = SEC_R // SEC_W
    vec = lambda width: pl.BlockSpec((1, width), lambda b, n: (0, 0))
    mat = lambda rows: pl.BlockSpec((rows, GW), lambda b, n: (0, 0))
    out = pl.pallas_call(
        _rwkv_kernel,
        grid=(bsz // nseq, nc),
        in_specs=[
            pl.BlockSpec((nseq, c, RWKV_IN), lambda b, n: (b, n, col)),
            vec(RWKV_IN), vec(GW), mat(128), vec(GW), mat(128), mat(128),
            vec(GW), vec(GW), vec(GW), vec(GW), vec(GW),
            pl.BlockSpec((GW, GW), lambda b, n: (0, 0)),
        ],
        out_specs=pl.BlockSpec((nseq, c, GW), lambda b, n: (b, n, 0)),
        out_shape=jax.ShapeDtypeStruct((bsz, seq, GW), BF16),
        scratch_shapes=[pltpu.VMEM((8, RWKV_IN), F32), pltpu.VMEM((nseq, N_HEADS, HEAD, HEAD), F32)],
        compiler_params=_params(("parallel", "arbitrary")),
        name="rwkv7_chunked",
    )(p.reshape(bsz, seq, P_WIDTH), *prm)
    return out.reshape(bsz * seq, GW)


def _rwkv_params(mu, w0, w_up, a0, a_up, g_up, k_k, k_a, r_k, ln_w, ln_b):
    zeros = jnp.zeros((64, GW), F32)
    wup = jnp.concatenate([w_up, zeros], axis=0).astype(BF16)
    aup = jnp.concatenate([zeros, a_up], axis=0).astype(BF16)
    head = jnp.arange(GW) // HEAD
    hsum = (head[:, None] == head[None, :]).astype(BF16)
    row = lambda t: t.reshape(1, -1)
    return (row(mu), row(w0), wup, row(a0), aup, g_up.astype(BF16), row(k_k), row(k_a),
            row(r_k), row(ln_w), row(ln_b), hsum)


def _ssd_kernel(p_ref, cw_ref, cb_ref, dtb_ref, a_ref, dsk_ref, nw_ref, exp_ref, o_ref, tail_ref, state_ref):
    @pl.when(pl.program_id(1) == 0)
    def _():
        tail_ref[...] = jnp.zeros_like(tail_ref)
        state_ref[...] = jnp.zeros_like(state_ref)

    for j in range(p_ref.shape[0]):
        _ssd_chunk(p_ref.at[j], cw_ref, cb_ref, dtb_ref, a_ref, dsk_ref, nw_ref, exp_ref, o_ref.at[j],
                   tail_ref.at[j], state_ref.at[j])


def _ssd_chunk(p_ref, cw_ref, cb_ref, dtb_ref, a_ref, dsk_ref, nw_ref, exp_ref, o_ref, tail_ref, state_ref):
    c = SSD_CHUNK
    blk = p_ref[...].astype(F32)
    z = blk[:, :GW]
    raw = blk[:, GW:GW + SSD_CONV_CH]
    ext = jnp.concatenate([tail_ref[...], raw], axis=0)
    tail_ref[...] = raw[c - 8:c, :]
    conv = cb_ref[...]
    for i in range(SSD_CONV):
        off = 8 - (SSD_CONV - 1) + i
        conv = conv + cw_ref[i:i + 1, :] * ext[off:off + c, :]
    xbc = _silu(conv)
    x = xbc[:, :GW]
    bm = xbc[:, GW:GW + 256].astype(BF16)
    cm = xbc[:, GW + 256:GW + 512].astype(BF16)

    dt = _softplus(blk[:, GW + SSD_CONV_CH:GW + SSD_CONV_CH + 128] + dtb_ref[...])
    da = dt * a_ref[...]
    ri = lax.broadcasted_iota(jnp.int32, (c, c), 0)
    ci = lax.broadcasted_iota(jnp.int32, (c, c), 1)
    lower_eq = ri >= ci
    tril = lower_eq.astype(BF16)
    cs = _split_mm(tril, da, 3)
    cs_t = cs.T
    spread = exp_ref[...]
    dt_w = _split_mm_rhs(dt, spread, 3)
    cs_w = _split_mm(tril, _split_mm_rhs(da, spread, 3), 3)
    cs_end = cs_w[c - 1:c, :]
    xdt = x * dt_w
    xdt16 = xdt.astype(BF16)
    xdec16 = (xdt * jnp.exp(cs_end - cs_w)).astype(BF16)
    dec_all = jnp.exp(cs_end)

    heads = range(N_HEADS)
    groups = range(N_HEADS // 4)
    sls = [slice(HEAD * h, HEAD * (h + 1)) for h in heads]
    bg = [bm[:, SSD_STATE * g:SSD_STATE * (g + 1)] for g in groups]
    cg = [cm[:, SSD_STATE * g:SSD_STATE * (g + 1)] for g in groups]
    cb = [_nt(cg[g], bg[g]) for g in groups]
    s0 = [state_ref[h] for h in heads]
    lmat = [jnp.exp(jnp.where(lower_eq, cs[:, h:h + 1] - cs_t[h:h + 1, :], -jnp.inf)) for h in heads]
    y_in = [_mm((cb[h // 4] * lmat[h]).astype(BF16), xdt16[:, sls[h]]) for h in heads]
    y_st = [_nt(cg[h // 4], s0[h].astype(BF16)) for h in heads]
    new = [_tn(xdec16[:, sls[h]], bg[h // 4]) for h in heads]
    for h in heads:
        state_ref[h] = s0[h] * dec_all[:, HEAD * h:HEAD * h + 1] + new[h]
    y = (jnp.concatenate(y_in, axis=-1) + jnp.concatenate(y_st, axis=-1) * jnp.exp(cs_w) + x * dsk_ref[...])
    y = y * _silu(z)
    half = GW // 2
    y = jnp.concatenate([_rms(y[:, :half], SSD_NORM_EPS), _rms(y[:, half:], SSD_NORM_EPS)], axis=-1)
    o_ref[...] = (y * nw_ref[...]).astype(o_ref.dtype)


def _ssd(p, prm, bsz, seq):
    c = SSD_CHUNK
    nc = seq // c
    nseq = math.gcd(bsz, SSD_SEQS)
    col = SEC_S // SEC_W
    vec = lambda width: pl.BlockSpec((1, width), lambda b, n: (0, 0))
    out = pl.pallas_call(
        _ssd_kernel,
        grid=(bsz // nseq, nc),
        in_specs=[
            pl.BlockSpec((nseq, c, SEC_W), lambda b, n: (b, n, col)),
            pl.BlockSpec((SSD_CONV, SSD_CONV_CH), lambda b, n: (0, 0)),
            vec(SSD_CONV_CH), vec(128), vec(128), vec(GW), vec(GW),
            pl.BlockSpec((128, GW), lambda b, n: (0, 0)),
        ],
        out_specs=pl.BlockSpec((nseq, c, GW), lambda b, n: (b, n, 0)),
        out_shape=jax.ShapeDtypeStruct((bsz, seq, GW), BF16),
        scratch_shapes=[pltpu.VMEM((nseq, 8, SSD_CONV_CH), F32),
                        pltpu.VMEM((nseq, N_HEADS, HEAD, SSD_STATE), F32)],
        compiler_params=_params(("parallel", "arbitrary")),
        name="mamba2_ssd",
    )(p.reshape(bsz, seq, P_WIDTH), *prm)
    return out.reshape(bsz * seq, GW)


def _ssd_params(conv_w, conv_b, dt_bias, a_log, d_skip, norm_w):
    pad = lambda t: jnp.zeros((1, 128), F32).at[0, :N_HEADS].set(t)
    spread = (jnp.arange(128)[:, None] == jnp.arange(GW)[None, :] // HEAD).astype(BF16)
    return (conv_w, conv_b.reshape(1, -1), pad(dt_bias), pad(-jnp.exp(a_log)),
            jnp.repeat(d_skip, HEAD).reshape(1, -1), norm_w.reshape(1, -1), spread)


def _gelu_tanh(x):
    return 0.5 * x * (1.0 + jnp.tanh(math.sqrt(2.0 / math.pi) * (x + 0.044715 * (x * x * x))))


def _s5_kernel(u_ref, bbr_ref, bbi_ref, lvr_ref, lvi_ref, pwr_ref, pwi_ref, ccr_ref, cci_ref,
               dsk_ref, gw_ref, gb_ref, o_ref, hr_ref, hi_ref, carry_ref):
    @pl.when(pl.program_id(1) == 0)
    def _():
        carry_ref[...] = jnp.zeros_like(carry_ref)

    u = u_ref[...].astype(F32)
    u16 = u.astype(BF16)
    q_in, q_st = GW // S5_QUADS, S5_LANES // S5_QUADS
    for q in range(S5_QUADS):
        uq = u16[:, q * q_in:(q + 1) * q_in]
        hr_ref[:, q * q_st:(q + 1) * q_st] = _mm(uq, bbr_ref[q])
        hi_ref[:, q * q_st:(q + 1) * q_st] = _mm(uq, bbi_ref[q])

    def tile(t, carry):
        cr, ci = carry
        rows = pl.ds(pl.multiple_of(t * S5_TILE, S5_TILE), S5_TILE)
        xr = hr_ref[rows, :]
        xi = hi_ref[rows, :]
        for lvl, s in enumerate((1, 2, 4)):
            lr = lvr_ref[lvl]
            li = lvi_ref[lvl]
            sr = pltpu.roll(xr, s, axis=0)
            si = pltpu.roll(xi, s, axis=0)
            xr, xi = xr + lr * sr - li * si, xi + lr * si + li * sr
        pr = pwr_ref[...]
        pi = pwi_ref[...]
        xr, xi = xr + pr * cr - pi * ci, xi + pr * ci + pi * cr
        hr_ref[rows, :] = xr
        hi_ref[rows, :] = xi
        last = S5_TILE - 1
        return (jnp.broadcast_to(xr[last:last + 1, :], xr.shape), jnp.broadcast_to(xi[last:last + 1, :], xi.shape))

    cr, ci = lax.fori_loop(0, u.shape[0] // S5_TILE, tile, (carry_ref[0], carry_ref[1]))
    carry_ref[0] = cr
    carry_ref[1] = ci

    y = jnp.concatenate(
        [_mm(hr_ref[:, q * q_st:(q + 1) * q_st].astype(BF16), ccr_ref[q])
         - _mm(hi_ref[:, q * q_st:(q + 1) * q_st].astype(BF16), cci_ref[q]) for q in range(S5_QUADS)], axis=-1)
    y = _gelu_tanh(y + dsk_ref[...] * u)
    gate = jax.nn.sigmoid(_mm(y.astype(BF16), gw_ref[...]) + gb_ref[...])
    o_ref[...] = (y * gate).astype(o_ref.dtype)


def _s5(p, prm, bsz, seq):
    c = S5_CHUNK
    nc = seq // c
    col = SEC_F // GW
    full = lambda shape: pl.BlockSpec(shape, lambda b, n: (0,) * len(shape))
    return pl.pallas_call(
        _s5_kernel,
        grid=(bsz, nc),
        in_specs=[
            pl.BlockSpec((c, GW), lambda b, n: (b * nc + n, col)),
            full((S5_QUADS, GW // S5_QUADS, S5_LANES // S5_QUADS)),
            full((S5_QUADS, GW // S5_QUADS, S5_LANES // S5_QUADS)),
            full((3, S5_TILE, S5_LANES)), full((3, S5_TILE, S5_LANES)),
            full((S5_TILE, S5_LANES)), full((S5_TILE, S5_LANES)),
            full((S5_QUADS, S5_LANES // S5_QUADS, GW // S5_QUADS)),
            full((S5_QUADS, S5_LANES // S5_QUADS, GW // S5_QUADS)),
            full((1, GW)), full((GW, GW)), full((1, GW)),
        ],
        out_specs=pl.BlockSpec((c, GW), lambda b, n: (b * nc + n, 0)),
        out_shape=jax.ShapeDtypeStruct((bsz * seq, GW), BF16),
        scratch_shapes=[pltpu.VMEM((c, S5_LANES), F32), pltpu.VMEM((c, S5_LANES), F32),
                        pltpu.VMEM((2, S5_TILE, S5_LANES), F32)],
        compiler_params=_params(("parallel", "arbitrary")),
        name="s5_scan",
    )(p, *prm)


def _s5_params(lam_re, lam_im, log_step, b_re, b_im, c_re, c_im, d_skip, glu_w, glu_b):
    step = jnp.exp(log_step)[:, None]
    mag = jnp.exp(lam_re * step)
    ab_re, ab_im = mag * jnp.cos(lam_im * step), mag * jnp.sin(lam_im * step)
    den = lam_re * lam_re + lam_im * lam_im
    coef_re = ((ab_re - 1) * lam_re + ab_im * lam_im) / den
    coef_im = (ab_im * lam_re - (ab_re - 1) * lam_im) / den
    bb_re = coef_re[..., None] * b_re - coef_im[..., None] * b_im
    bb_im = coef_re[..., None] * b_im + coef_im[..., None] * b_re
    gq = S5_GROUPS // S5_QUADS
    eye = jnp.eye(gq, dtype=F32)
    in_proj = lambda t: jnp.einsum(
        'qgni,gh->qgihn', t.reshape(S5_QUADS, gq, S5_STATE, S5_GROUP_CH), eye
    ).reshape(S5_QUADS, gq * S5_GROUP_CH, gq * S5_STATE).astype(BF16)
    out_proj = lambda t: jnp.einsum(
        'qgin,gh->qgnhi', t.reshape(S5_QUADS, gq, S5_GROUP_CH, S5_STATE), eye
    ).reshape(S5_QUADS, gq * S5_STATE, gq * S5_GROUP_CH).astype(BF16)

    def cmul(a, b):
        return a[0] * b[0] - a[1] * b[1], a[0] * b[1] + a[1] * b[0]

    lam1 = (ab_re.reshape(-1), ab_im.reshape(-1))
    lam2 = cmul(lam1, lam1)
    lam4 = cmul(lam2, lam2)
    rows = jnp.arange(S5_TILE)[:, None]
    lvl_re = jnp.stack([jnp.where(rows >= s, l[0][None, :], 0.0) for s, l in ((1, lam1), (2, lam2), (4, lam4))])
    lvl_im = jnp.stack([jnp.where(rows >= s, l[1][None, :], 0.0) for s, l in ((1, lam1), (2, lam2), (4, lam4))])
    powers = [lam1]
    for _ in range(S5_TILE - 1):
        powers.append(cmul(powers[-1], lam1))
    pw_re = jnp.stack([q[0] for q in powers])
    pw_im = jnp.stack([q[1] for q in powers])
    return (in_proj(bb_re), in_proj(bb_im), lvl_re, lvl_im, pw_re, pw_im, out_proj(c_re), out_proj(c_im),
            d_skip.reshape(1, -1), glu_w.astype(BF16), glu_b.reshape(1, -1))


def _outproj_kernel(ya_ref, yr_ref, ys_ref, yf_ref, w_ref, x_ref, g_ref, nw_ref, o_ref):
    acc = _mm(ya_ref[...], w_ref[0, 0:GW, :])
    acc = acc + _mm(yr_ref[...], w_ref[0, GW:2 * GW, :])
    acc = acc + _mm(ys_ref[...], w_ref[0, 2 * GW:3 * GW, :])
    acc = acc + _mm(yf_ref[...], w_ref[0, 3 * GW:4 * GW, :])
    o_ref[...] = x_ref[...] + g_ref[0] * (_rms(acc, NORM_EPS) * nw_ref[...])


def _out_projection(ys, w, layer, x2, gate, nw, seq):
    t = x2.shape[0]
    tm = 512
    per_b = seq // tm
    ymap = pl.BlockSpec((tm, GW), lambda i: (i, 0))
    return pl.pallas_call(
        _outproj_kernel,
        grid=(t // tm,),
        in_specs=[
            ymap, ymap, ymap, ymap,
            pl.BlockSpec((1, D_MODEL, D_MODEL), lambda i: (layer, 0, 0)),
            pl.BlockSpec((tm, D_MODEL), lambda i: (i, 0)),
            pl.BlockSpec((1, 1, D_MODEL), lambda i: (i // per_b, 0, 0)),
            pl.BlockSpec((1, D_MODEL), lambda i: (0, 0)),
        ],
        out_specs=pl.BlockSpec((tm, D_MODEL), lambda i: (i, 0)),
        out_shape=jax.ShapeDtypeStruct((t, D_MODEL), F32),
        compiler_params=_params(("parallel",)),
        name="out_projection",
    )(*ys, w, x2, gate, nw)


def _ffn_kernel(x_ref, xn_ref, npre_ref, sc_ref, sh_ref, scn_ref, shn_ref, wg_ref, wu_ref, wd_ref, g_ref,
                npost_ref, o_ref, h_ref, acc_ref):
    i = pl.program_id(0)
    f = pl.program_id(1)
    last = pl.num_programs(1) - 1
    slot = i % 2

    @pl.when((i == 0) & (f == 0))
    def _():
        h_ref[0] = _prenorm(x_ref[...], npre_ref[...], sc_ref[0], sh_ref[0]).astype(BF16)

    def partial():
        h = h_ref[slot]
        act = _silu(_mm(h, wg_ref[0])) * _mm(h, wu_ref[0])
        return _mm(act.astype(BF16), wd_ref[0])

    @pl.when(f == 0)
    def _():
        acc_ref[...] = partial()

    @pl.when((f > 0) & (f < last))
    def _():
        acc_ref[...] += partial()

    @pl.when(f == last)
    def _():
        acc = acc_ref[...] + partial()
        h_ref[1 - slot] = _prenorm(xn_ref[...], npre_ref[...], scn_ref[0], shn_ref[0]).astype(BF16)
        o_ref[...] = x_ref[...] + g_ref[0] * (_rms(acc, NORM_EPS) * npost_ref[...])


def _dense_ffn(x2, npre, scale, shift, wg, wu, wd, layer, gate, npost, seq):
    t = x2.shape[0]
    tm, tf = 512, 512
    per_b = seq // tm
    n_i = t // tm
    assert D_FF // tf >= 3
    nxt = lambda i: jnp.minimum(i + 1, n_i - 1)
    mod = pl.BlockSpec((1, 1, D_MODEL), lambda i, f: (i // per_b, 0, 0))
    mod_next = pl.BlockSpec((1, 1, D_MODEL), lambda i, f: (nxt(i) // per_b, 0, 0))
    vec = pl.BlockSpec((1, D_MODEL), lambda i, f: (0, 0))
    return pl.pallas_call(
        _ffn_kernel,
        grid=(n_i, D_FF // tf),
        in_specs=[
            pl.BlockSpec((tm, D_MODEL), lambda i, f: (i, 0)),
            pl.BlockSpec((tm, D_MODEL), lambda i, f: (nxt(i), 0)),
            vec, mod, mod, mod_next, mod_next,
            pl.BlockSpec((1, D_MODEL, tf), lambda i, f: (layer, 0, f)),
            pl.BlockSpec((1, D_MODEL, tf), lambda i, f: (layer, 0, f)),
            pl.BlockSpec((1, tf, D_MODEL), lambda i, f: (layer, f, 0)),
            mod, vec,
        ],
        out_specs=pl.BlockSpec((tm, D_MODEL), lambda i, f: (i, 0)),
        out_shape=jax.ShapeDtypeStruct((t, D_MODEL), F32),
        scratch_shapes=[pltpu.VMEM((2, tm, D_MODEL), BF16), pltpu.VMEM((tm, D_MODEL), F32)],
        compiler_params=_params(("arbitrary", "arbitrary")),
        name="dense_swiglu",
    )(x2, x2, npre, scale, shift, scale, shift, wg, wu, wd, gate, npost)


def _router_kernel(x_ref, npre_ref, sc_ref, sh_ref, rw_ref, rb_ref, h_ref, lg_ref):
    h = _prenorm(x_ref[...], npre_ref[...], sc_ref[0], sh_ref[0])
    h_ref[...] = h.astype(BF16)
    lg_ref[...] = _mm(h, rw_ref[...], precision=HIGHEST) + rb_ref[...]


def _router(x2, npre, scale, shift, rw, rb, seq):
    t = x2.shape[0]
    tm = 512
    per_b = seq // tm
    mod = pl.BlockSpec((1, 1, D_MODEL), lambda i: (i // per_b, 0, 0))
    return pl.pallas_call(
        _router_kernel,
        grid=(t // tm,),
        in_specs=[
            pl.BlockSpec((tm, D_MODEL), lambda i: (i, 0)),
            pl.BlockSpec((1, D_MODEL), lambda i: (0, 0)),
            mod, mod,
            pl.BlockSpec((D_MODEL, 128), lambda i: (0, 0)),
            pl.BlockSpec((1, 128), lambda i: (0, 0)),
        ],
        out_specs=[pl.BlockSpec((tm, D_MODEL), lambda i: (i, 0)), pl.BlockSpec((tm, 128), lambda i: (i, 0))],
        out_shape=[jax.ShapeDtypeStruct((t, D_MODEL), BF16), jax.ShapeDtypeStruct((t, 128), F32)],
        compiler_params=_params(("parallel",)),
        name="moe_router",
    )(x2, npre, scale, shift, rw, rb)


def _moe_ffn_kernel(be_ref, used_ref, xs_ref, wg_ref, wu_ref, wd_ref, o_ref, acc_ref, *, nf):
    i = pl.program_id(0)
    f = pl.program_id(1)
    last = nf - 1
    live = i < used_ref[0]

    def partial():
        xs = xs_ref[...]
        act = _silu(_mm(xs, wg_ref[0, 0])) * _mm(xs, wu_ref[0, 0])
        return _mm(act.astype(BF16), wd_ref[0, 0])

    @pl.when(live & (f == 0))
    def _():
        acc_ref[...] = partial()

    if nf > 2:
        @pl.when(live & (f > 0) & (f < last))
        def _():
            acc_ref[...] += partial()

    @pl.when(live & (f == last))
    def _():
        o_ref[...] = (acc_ref[...] + partial()).astype(o_ref.dtype)

    @pl.when(jnp.logical_not(live) & (f == last))
    def _():
        o_ref[...] = jnp.zeros_like(o_ref)


def _moe_ffn(block_e, n_used, xs, wg, wu, wd, layer):
    rows = xs.shape[0]
    tf = D_EXPERT // 2
    nf = D_EXPERT // tf

    def tile(i, f, used):
        return jnp.where(i < used[0], f, nf - 1)

    grid_spec = pltpu.PrefetchScalarGridSpec(
        num_scalar_prefetch=2,
        grid=(rows // MOE_BLOCK, nf),
        in_specs=[
            pl.BlockSpec((MOE_BLOCK, D_MODEL), lambda i, f, be, used: (i, 0)),
            pl.BlockSpec((1, 1, D_MODEL, tf), lambda i, f, be, used: (layer, be[i], 0, tile(i, f, used))),
            pl.BlockSpec((1, 1, D_MODEL, tf), lambda i, f, be, used: (layer, be[i], 0, tile(i, f, used))),
            pl.BlockSpec((1, 1, tf, D_MODEL), lambda i, f, be, used: (layer, be[i], tile(i, f, used), 0)),
        ],
        out_specs=pl.BlockSpec((MOE_BLOCK, D_MODEL), lambda i, f, be, used: (i, 0)),
        scratch_shapes=[pltpu.VMEM((MOE_BLOCK, D_MODEL), F32)],
    )
    assert nf >= 2
    return pl.pallas_call(
        functools.partial(_moe_ffn_kernel, nf=nf),
        grid_spec=grid_spec,
        out_shape=jax.ShapeDtypeStruct((rows, D_MODEL), BF16),
        compiler_params=_params(("parallel", "arbitrary"), MOE_VMEM_LIMIT),
        name="moe_swiglu",
    )(block_e, n_used, xs, wg, wu, wd)


def _combine_kernel(y0_ref, y1_ref, p_ref, x_ref, g_ref, npost_ref, o_ref):
    p = p_ref[...]
    y = y0_ref[...].astype(F32) * p[:, 0:1] + y1_ref[...].astype(F32) * p[:, 1:2]
    o_ref[...] = x_ref[...] + g_ref[0] * (_rms(y, NORM_EPS) * npost_ref[...])


def _moe_combine(y0, y1, top_p, x2, gate, npost, seq):
    t = x2.shape[0]
    tm = 512
    per_b = seq // tm
    row = pl.BlockSpec((tm, D_MODEL), lambda i: (i, 0))
    return pl.pallas_call(
        _combine_kernel,
        grid=(t // tm,),
        in_specs=[row, row, pl.BlockSpec((tm, 128), lambda i: (i, 0)), row,
                  pl.BlockSpec((1, 1, D_MODEL), lambda i: (i // per_b, 0, 0)),
                  pl.BlockSpec((1, D_MODEL), lambda i: (0, 0))],
        out_specs=row,
        out_shape=jax.ShapeDtypeStruct((t, D_MODEL), F32),
        compiler_params=_params(("parallel",)),
        name="moe_combine",
    )(y0, y1, top_p, x2, gate, npost)


def _routed_ffn(x2, npre, scale, shift, rw, rb, wg, wu, wd, layer, gate, npost, seq):
    t = x2.shape[0]
    rw_pad = jnp.zeros((D_MODEL, 128), F32).at[:, :N_EXPERTS].set(rw)
    rb_pad = jnp.zeros((1, 128), F32).at[0, :N_EXPERTS].set(rb)
    h, logits = _router(x2, npre, scale, shift, rw_pad, rb_pad, seq)
    logits = logits[:, :N_EXPERTS]
    top_logit, top_idx = lax.top_k(logits, TOP_K)
    top_p = jax.nn.softmax(top_logit, axis=-1)
    n_assign = t * TOP_K
    flat_e = top_idx.reshape(-1).astype(jnp.int32)
    order = jnp.argsort(flat_e).astype(jnp.int32)
    rank = jnp.argsort(order).astype(jnp.int32)
    counts = jnp.sum((flat_e[:, None] == jnp.arange(N_EXPERTS, dtype=jnp.int32)[None, :]).astype(jnp.int32), axis=0)
    padded = (counts + MOE_BLOCK - 1) // MOE_BLOCK * MOE_BLOCK
    pad_end = jnp.cumsum(padded)
    pad_start = pad_end - padded
    start = jnp.cumsum(counts) - counts
    n_blocks = -(-n_assign // MOE_BLOCK) + N_EXPERTS
    rows = n_blocks * MOE_BLOCK
    block_e = jnp.minimum(jnp.searchsorted(pad_end, jnp.arange(n_blocks) * MOE_BLOCK, side='right'),
                          N_EXPERTS - 1).astype(jnp.int32)
    e_row = jnp.repeat(block_e, MOE_BLOCK)
    off = jnp.arange(rows, dtype=jnp.int32) - pad_start[e_row]
    src = order[jnp.clip(start[e_row] + off, 0, n_assign - 1)]
    row_tok = jnp.where(off < counts[e_row], src // TOP_K, t)
    h_pad = jnp.concatenate([h, jnp.zeros((1, D_MODEL), h.dtype)], axis=0)
    xs = h_pad[row_tok]
    n_used = (pad_end[-1:] // MOE_BLOCK).astype(jnp.int32)
    yb = _moe_ffn(block_e, n_used, xs, wg, wu, wd, layer)
    pos = (pad_start[flat_e] + rank - start[flat_e]).reshape(t, TOP_K)
    p_pad = jnp.zeros((t, 128), F32).at[:, :TOP_K].set(top_p)
    return _moe_combine(yb[pos[:, 0]], yb[pos[:, 1]], p_pad, x2, gate, npost, seq)


def _in_weights(w_in):
    a0, r0, s0, f0 = 0, ATTN_IN, ATTN_IN + RWKV_IN, ATTN_IN + RWKV_IN + SSD_IN
    zeros = jnp.zeros(w_in.shape[:-1] + (SEC_W - SSD_IN,), w_in.dtype)
    return jnp.concatenate([
        w_in[..., r0:s0],
        w_in[..., s0:f0], zeros,
        w_in[..., f0:f0 + GW],
        w_in[..., a0:r0],
    ], axis=-1).astype(BF16)


def kernel(x, c, ada_w, ada_b, norm_pre, norm_post, w_in, w_out, attn_sink, rwkv_mu, rwkv_w0, rwkv_w_up, rwkv_a0, rwkv_a_up, rwkv_g_up, rwkv_k_k, rwkv_k_a, rwkv_r_k, rwkv_ln_w, rwkv_ln_b, ssd_conv_w, ssd_conv_b, ssd_dt_bias, ssd_a_log, ssd_d, ssd_norm_w, s5_lam_re, s5_lam_im, s5_log_step, s5_b_re, s5_b_im, s5_c_re, s5_c_im, s5_d, s5_glu_w, s5_glu_b, ffn_w_gate, ffn_w_up, ffn_w_down, moe_router_w, moe_router_b, moe_w_gate, moe_w_up, moe_w_down):
    bsz, seq, d = x.shape
    depth = ada_w.shape[0]
    mod = _ada_modulation(c, ada_w, ada_b)
    mod = mod.reshape(depth, 2, bsz, 3, 1, d)
    x2 = x.reshape(bsz * seq, d)
    w_in16 = _in_weights(w_in)
    w_out16 = w_out.astype(BF16)
    ffn16 = (ffn_w_gate.astype(BF16), ffn_w_up.astype(BF16), ffn_w_down.astype(BF16))
    moe16 = (moe_w_gate.astype(BF16), moe_w_up.astype(BF16), moe_w_down.astype(BF16))
    for i in range(depth):
        shift, scale, gate = mod[i, 0, :, 0], mod[i, 0, :, 1], mod[i, 0, :, 2]
        p = _in_projection(x2, norm_pre[i, 0][None], scale, shift, w_in16, i, seq)
        y_attn = _attention(p, attn_sink[i][None], bsz, seq)
        y_rwkv = _rwkv(p, _rwkv_params(rwkv_mu[i], rwkv_w0[i], rwkv_w_up[i], rwkv_a0[i], rwkv_a_up[i],
                                       rwkv_g_up[i], rwkv_k_k[i], rwkv_k_a[i], rwkv_r_k[i].reshape(-1),
                                       rwkv_ln_w[i], rwkv_ln_b[i]), bsz, seq)
        y_ssd = _ssd(p, _ssd_params(ssd_conv_w[i], ssd_conv_b[i], ssd_dt_bias[i], ssd_a_log[i], ssd_d[i],
                                    ssd_norm_w[i]), bsz, seq)
        y_s5 = _s5(p, _s5_params(s5_lam_re[i], s5_lam_im[i], s5_log_step[i], s5_b_re[i], s5_b_im[i],
                                 s5_c_re[i], s5_c_im[i], s5_d[i], s5_glu_w[i], s5_glu_b[i]), bsz, seq)
        x2 = _out_projection((y_attn, y_rwkv, y_ssd, y_s5), w_out16, i, x2, gate, norm_post[i, 0][None], seq)
        shift, scale, gate = mod[i, 1, :, 0], mod[i, 1, :, 1], mod[i, 1, :, 2]
        j = i // 2
        if i % 2 == 0:
            x2 = _dense_ffn(x2, norm_pre[i, 1][None], scale, shift, *ffn16, j, gate, norm_post[i, 1][None], seq)
        else:
            x2 = _routed_ffn(x2, norm_pre[i, 1][None], scale, shift, moe_router_w[j], moe_router_b[j],
                             *moe16, j, gate, norm_post[i, 1][None], seq)
    return x2.reshape(bsz, seq, d)
```

```python
import functools
import math

import jax
import jax.numpy as jnp
from jax import lax
from jax.experimental import pallas as pl
from jax.experimental.pallas import tpu as pltpu

F32 = jnp.float32
BF16 = jnp.bfloat16
HIGHEST = lax.Precision.HIGHEST

D_MODEL = 2048
DEPTH = 4
GW = 512
NORM_EPS = 1e-6
HEAD = 64
N_HEADS = 8

ATTN_KV_HEADS = 2
ATTN_GROUP = 4
ATTN_BLOCK = 128
ATTN_IN = 768
ATTN_BLOCKS_PER_STEP = 8

RWKV_IN = 1792
RWKV_LN_EPS = 64e-5
RWKV_CHUNK = 64
RWKV_SEQS = 4

SSD_STATE = 128
SSD_CONV = 4
SSD_CHUNK = 128
SSD_SEQS = 4
SSD_CONV_CH = 1024
SSD_IN = 1544
SSD_NORM_EPS = 1e-5

S5_GROUPS = 32
S5_GROUP_CH = 16
S5_STATE = 64
S5_LANES = S5_GROUPS * S5_STATE
S5_CHUNK = 1024
S5_TILE = 8
S5_QUADS = 4

D_FF = 5632
N_EXPERTS = 8
TOP_K = 2
D_EXPERT = 2816
MOE_BLOCK = 512

SEC_R = 0
SEC_S = 1792
SEC_F = 3584
SEC_Q = 4096
SEC_KV = 4608
P_WIDTH = 4864
SEC_W = 1792

VMEM_LIMIT = 52 * 1024 * 1024
MOE_VMEM_LIMIT = 58 * 1024 * 1024


def _params(sem, vmem=VMEM_LIMIT):
    return pltpu.CompilerParams(dimension_semantics=sem, vmem_limit_bytes=vmem)


def _nt(a, b, **kw):
    return lax.dot_general(a, b, (((1,), (1,)), ((), ())), preferred_element_type=F32, **kw)


def _tn(a, b, **kw):
    return lax.dot_general(a, b, (((0,), (0,)), ((), ())), preferred_element_type=F32, **kw)


def _mm(a, b, **kw):
    return jnp.dot(a, b, preferred_element_type=F32, **kw)


def _split_mm(a, x, parts):
    acc = None
    for _ in range(parts):
        piece = x.astype(BF16)
        term = _mm(a, piece)
        acc = term if acc is None else acc + term
        x = x - piece.astype(F32)
    return acc


def _split_mm_rhs(x, b, parts):
    acc = None
    for _ in range(parts):
        piece = x.astype(BF16)
        term = _mm(piece, b)
        acc = term if acc is None else acc + term
        x = x - piece.astype(F32)
    return acc


def _softplus(x):
    return jnp.maximum(x, 0.0) + jnp.log1p(jnp.exp(-jnp.abs(x)))


def _silu(x):
    return x * jax.nn.sigmoid(x)


def _rms(x, eps):
    return x * lax.rsqrt(jnp.mean(x * x, axis=-1, keepdims=True) + eps)


def _ada_kernel(c_ref, w_ref, b_ref, o_ref):
    c = c_ref[...]
    o_ref[0] = _mm(_silu(c).astype(BF16), w_ref[0].astype(BF16)) + b_ref[0]


def _ada_modulation(c, ada_w, ada_b):
    bsz = c.shape[0]
    rows = 8 * pl.cdiv(bsz, 8)
    c_pad = jnp.zeros((rows, D_MODEL), F32).at[:bsz].set(c)
    n_mod = ada_w.shape[0] * 2
    w = ada_w.reshape(n_mod, D_MODEL, 3 * D_MODEL)
    b = ada_b.reshape(n_mod, 1, 3 * D_MODEL)
    tn = 768
    out = pl.pallas_call(
        _ada_kernel,
        grid=(n_mod, 3 * D_MODEL // tn),
        in_specs=[
            pl.BlockSpec((rows, D_MODEL), lambda m, j: (0, 0)),
            pl.BlockSpec((1, D_MODEL, tn), lambda m, j: (m, 0, j)),
            pl.BlockSpec((1, 1, tn), lambda m, j: (m, 0, j)),
        ],
        out_specs=pl.BlockSpec((1, rows, tn), lambda m, j: (m, 0, j)),
        out_shape=jax.ShapeDtypeStruct((n_mod, rows, 3 * D_MODEL), F32),
        compiler_params=_params(("parallel", "parallel")),
        name="ada_modulation",
    )(c_pad, w, b)
    return out[:, :bsz]


def _prenorm(x, nw, scale, shift):
    return (_rms(x, NORM_EPS) * nw) * (1.0 + scale) + shift


def _inproj_kernel(x0_ref, xn_ref, nw_ref, sc0_ref, sh0_ref, scn_ref, shn_ref, w_ref, o_ref, h_ref):
    i = pl.program_id(0)
    slot = i % 2

    @pl.when(i == 0)
    def _():
        h_ref[0] = _prenorm(x0_ref[...], nw_ref[...], sc0_ref[0], sh0_ref[0]).astype(BF16)

    o_ref[...] = _mm(h_ref[slot], w_ref[0]).astype(o_ref.dtype)
    h_ref[1 - slot] = _prenorm(xn_ref[...], nw_ref[...], scn_ref[0], shn_ref[0]).astype(BF16)


def _in_projection(x2, nw, scale, shift, w, layer, seq):
    t = x2.shape[0]
    tm = 512
    per_b = seq // tm
    n_i = t // tm
    nxt = lambda i: jnp.minimum(i + 1, n_i - 1)
    first_mod = pl.BlockSpec((1, 1, D_MODEL), lambda i: (0, 0, 0))
    next_mod = pl.BlockSpec((1, 1, D_MODEL), lambda i: (nxt(i) // per_b, 0, 0))
    return pl.pallas_call(
        _inproj_kernel,
        grid=(n_i,),
        in_specs=[
            pl.BlockSpec((tm, D_MODEL), lambda i: (0, 0)),
            pl.BlockSpec((tm, D_MODEL), lambda i: (nxt(i), 0)),
            pl.BlockSpec((1, D_MODEL), lambda i: (0, 0)),
            first_mod, first_mod, next_mod, next_mod,
            pl.BlockSpec((1, D_MODEL, P_WIDTH), lambda i: (layer, 0, 0), pipeline_mode=pl.Buffered(1)),
        ],
        out_specs=pl.BlockSpec((tm, P_WIDTH), lambda i: (i, 0)),
        out_shape=jax.ShapeDtypeStruct((t, P_WIDTH), BF16),
        scratch_shapes=[pltpu.VMEM((2, tm, D_MODEL), BF16)],
        compiler_params=_params(("arbitrary",)),
        name="in_projection",
    )(x2, x2, nw, scale, shift, scale, shift, w)


def _alibi_slope(h):
    return 2.0 ** (-8.0 * (h + 1) / N_HEADS)


def _attn_kernel(sink_ref, q_ref, cur_ref, prev_ref, o_ref):
    n = pl.program_id(1)
    blk = ATTN_BLOCK
    nblk = q_ref.shape[0] // blk
    q = q_ref[...]
    kv = jnp.concatenate([prev_ref[...], cur_ref[...]], axis=0).astype(BF16)
    qi = lax.broadcasted_iota(jnp.int32, (blk, 2 * blk), 0) + blk
    kj = lax.broadcasted_iota(jnp.int32, (blk, 2 * blk), 1)
    dist = qi - kj
    window = (dist >= 0) & (dist < blk)
    first = window & ((kj >= blk) | (n > 0))
    distf = dist.astype(F32)
    units = [(i, h) for i in range(nblk) for h in range(N_HEADS)]
    kg = {(i, g): kv[i * blk:(i + 2) * blk, HEAD * g:HEAD * (g + 1)]
          for i in range(nblk) for g in range(ATTN_KV_HEADS)}
    vg = {(i, g): kv[i * blk:(i + 2) * blk, 128 + HEAD * g:128 + HEAD * (g + 1)]
          for i in range(nblk) for g in range(ATTN_KV_HEADS)}
    s = {(i, h): _nt(q[i * blk:(i + 1) * blk, HEAD * h:HEAD * (h + 1)].astype(BF16), kg[i, h // ATTN_GROUP])
         for i, h in units}
    s = {(i, h): jnp.where(first if i == 0 else window,
                           s[i, h] * (HEAD ** -0.5) - _alibi_slope(h) * distf, -jnp.inf) for i, h in units}
    m = {(i, h): jnp.maximum(jnp.max(s[i, h], axis=-1, keepdims=True), sink_ref[0, h]) for i, h in units}
    p = {u: jnp.exp(s[u] - m[u]) for u in units}
    denom = {(i, h): jnp.sum(p[i, h], axis=-1, keepdims=True) + jnp.exp(sink_ref[0, h] - m[i, h]) for i, h in units}
    out = {(i, h): _mm(p[i, h].astype(BF16), vg[i, h // ATTN_GROUP]) / denom[i, h] for i, h in units}
    rows = [jnp.concatenate([out[i, h] for h in range(N_HEADS)], axis=-1) for i in range(nblk)]
    o_ref[...] = jnp.concatenate(rows, axis=0).astype(o_ref.dtype)


def _attention(p, sinks, bsz, seq):
    rows = ATTN_BLOCK * ATTN_BLOCKS_PER_STEP
    steps = seq // rows
    nb = seq // ATTN_BLOCK
    q_col = SEC_Q // GW
    kv_col = SEC_KV // 256
    return pl.pallas_call(
        _attn_kernel,
        grid=(bsz, steps),
        in_specs=[
            pl.BlockSpec(memory_space=pltpu.SMEM),
            pl.BlockSpec((rows, GW), lambda b, n: (b * steps + n, q_col)),
            pl.BlockSpec((rows, 256), lambda b, n: (b * steps + n, kv_col)),
            pl.BlockSpec((ATTN_BLOCK, 256),
                         lambda b, n: (b * nb + jnp.maximum(n * ATTN_BLOCKS_PER_STEP - 1, 0), kv_col)),
        ],
        out_specs=pl.BlockSpec((rows, GW), lambda b, n: (b * steps + n, 0)),
        out_shape=jax.ShapeDtypeStruct((bsz * seq, GW), BF16),
        compiler_params=_params(("parallel", "arbitrary")),
        name="swa_attention",
    )(sinks, p, p, p)


def _rwkv_kernel(p_ref, mu_ref, w0_ref, wup_ref, a0_ref, aup_ref, gup_ref, kk_ref, ka_ref,
                 rk_ref, lnw_ref, lnb_ref, hsum_ref, o_ref, last_ref, state_ref):
    c = RWKV_CHUNK
    nseq = p_ref.shape[0]
    rows = nseq * c

    @pl.when(pl.program_id(1) == 0)
    def _():
        last_ref[...] = jnp.zeros_like(last_ref)
        state_ref[...] = jnp.zeros_like(state_ref)

    p = p_ref[...].astype(F32).reshape(rows, RWKV_IN)
    row = lax.broadcasted_iota(jnp.int32, (rows, 1), 0)
    shifted = pltpu.roll(p, 1, axis=0)
    for j in range(nseq):
        shifted = jnp.where(row == j * c, last_ref[j:j + 1, :], shifted)
        last_ref[j:j + 1, :] = p[(j + 1) * c - 1:(j + 1) * c, :]
    pm = p + mu_ref[...] * (shifted - p)

    r = pm[:, 0:GW]
    k = pm[:, GW:2 * GW]
    v = pm[:, 2 * GW:3 * GW]
    wa = pm[:, 3 * GW:3 * GW + 128]
    gd = pm[:, 3 * GW + 128:3 * GW + 256]

    w = -_softplus(-(w0_ref[...] + _mm(jnp.tanh(wa).astype(BF16), wup_ref[...]))) - 0.5
    logd = -jnp.exp(w)
    a = jax.nn.sigmoid(a0_ref[...] + _mm(wa.astype(BF16), aup_ref[...]))
    g = _mm(jax.nn.sigmoid(gd).astype(BF16), gup_ref[...])

    kk = k * kk_ref[...]
    sumsq = _split_mm_rhs(kk * kk, hsum_ref[...], 2)
    kk = kk / jnp.maximum(jnp.sqrt(sumsq), 1e-12)
    k = k * (1.0 + (a - 1.0) * ka_ref[...])
    b = kk * a

    rr = lax.broadcasted_iota(jnp.int32, (rows, rows), 0)
    cr = lax.broadcasted_iota(jnp.int32, (rows, rows), 1)
    tril_seq = ((rr >= cr) & (rr // c == cr // c)).astype(BF16)
    cum = _split_mm(tril_seq, logd, 3)
    g_inv = jnp.exp(-cum)
    at16 = ((-kk) * jnp.exp(cum - logd)).astype(BF16)
    bt = (b * g_inv).astype(BF16)
    kt = (k * g_inv).astype(BF16)
    rt = (r * jnp.exp(cum)).astype(BF16)
    v16 = v.astype(BF16)
    rk = r * k * rk_ref[...]

    ri = lax.broadcasted_iota(jnp.int32, (c, c), 0)
    ci = lax.broadcasted_iota(jnp.int32, (c, c), 1)
    lower = ri > ci
    eye = (ri == ci).astype(F32)
    ri2 = lax.broadcasted_iota(jnp.int32, (c, 2 * c), 0)
    ci2 = lax.broadcasted_iota(jnp.int32, (c, 2 * c), 1)
    k_half = ci2 >= c
    cj2 = jnp.where(k_half, ci2 - c, ci2)
    strict_k = k_half & (ri2 > cj2)
    lower_eq2 = ri2 >= cj2
    levels = []
    size = 1
    while size < c:
        levels.append((ri // (2 * size) == ci // (2 * size)) & (ri // size > ci // size))
        size *= 2

    units = [(j, h) for j in range(nseq) for h in range(N_HEADS)]
    rs = {u: slice(u[0] * c, (u[0] + 1) * c) for u in units}
    ls = {u: slice(HEAD * u[1], HEAD * (u[1] + 1)) for u in units}
    v_h = {u: v16[rs[u], ls[u]] for u in units}
    ar = {u: jnp.concatenate([at16[rs[u], ls[u]], rt[rs[u], ls[u]]], axis=0) for u in units}
    bk = {u: jnp.concatenate([bt[rs[u], ls[u]], kt[rs[u], ls[u]]], axis=0) for u in units}
    s0 = {u: state_ref[u[0], u[1]] for u in units}
    gram = {u: _nt(ar[u], bk[u]) for u in units}
    l_ab = {u: jnp.where(lower, gram[u][:c, :c], 0.0).astype(BF16) for u in units}
    l_ak = {u: jnp.where(strict_k, gram[u][:c, :], 0.0).astype(BF16) for u in units}
    m_r = {u: jnp.where(lower_eq2, gram[u][c:, :], 0.0).astype(BF16) for u in units}
    inv = {u: eye + jnp.where(levels[0], l_ab[u].astype(F32), 0.0) for u in units}
    for lvl in levels[1:]:
        inv16 = {u: inv[u].astype(BF16) for u in units}
        wl = {u: _mm(jnp.where(lvl, l_ab[u], jnp.zeros_like(l_ab[u])), inv16[u]).astype(BF16) for u in units}
        inv = {u: inv[u] + _mm(inv16[u], wl[u]) for u in units}
    sx = {u: _nt(ar[u], s0[u].astype(BF16)) for u in units}
    x = {u: sx[u][:c] + _mm(l_ak[u], jnp.concatenate([v_h[u], v_h[u]], axis=0)) for u in units}
    us = {u: _mm(inv[u].astype(BF16), x[u].astype(BF16)).astype(BF16) for u in units}
    uv = {u: jnp.concatenate([us[u], v_h[u]], axis=0) for u in units}
    y = {u: sx[u][c:] + _mm(m_r[u], uv[u]) for u in units}
    for u in units:
        g_end = jnp.exp(cum[rs[u].stop - 1:rs[u].stop, ls[u]])
        state_ref[u[0], u[1]] = (s0[u] + _tn(uv[u], bk[u])) * g_end
    outs = []
    for j in range(nseq):
        heads = []
        for h in range(N_HEADS):
            u = (j, h)
            mean = jnp.mean(y[u], axis=-1, keepdims=True)
            yc = y[u] - mean
            var = jnp.mean(yc * yc, axis=-1, keepdims=True)
            bonus = jnp.sum(rk[rs[u], ls[u]], axis=-1, keepdims=True) * v[rs[u], ls[u]]
            heads.append(yc * lax.rsqrt(var + RWKV_LN_EPS) * lnw_ref[:, ls[u]] + lnb_ref[:, ls[u]] + bonus)
        outs.append(jnp.concatenate(heads, axis=-1))
    out = jnp.concatenate(outs, axis=0) * g
    o_ref[...] = out.reshape(nseq, c, GW).astype(o_ref.dtype)


def _rwkv(p, prm, bsz, seq):
    c = RWKV_CHUNK
    nc = seq // c
    nseq = math.gcd(bsz, RWKV_SEQS)
    col = SEC_R // SEC_W
    vec = lambda width: pl.BlockSpec((1, width), lambda b, n: (0, 0))
    mat = lambda rows: pl.BlockSpec((rows, GW), lambda b, n: (0, 0))
    out = pl.pallas_call(
        _rwkv_kernel,
        grid=(bsz // nseq, nc),
        in_specs=[
            pl.BlockSpec((nseq, c, RWKV_IN), lambda b, n: (b, n, col)),
            vec(RWKV_IN), vec(GW), mat(128), vec(GW), mat(128), mat(128),
            vec(GW), vec(GW), vec(GW), vec(GW), vec(GW),
            pl.BlockSpec((GW, GW), lambda b, n: (0, 0)),
        ],
        out_specs=pl.BlockSpec((nseq, c, GW), lambda b, n: (b, n, 0)),
        out_shape=jax.ShapeDtypeStruct((bsz, seq, GW), BF16),
        scratch_shapes=[pltpu.VMEM((8, RWKV_IN), F32), pltpu.VMEM((nseq, N_HEADS, HEAD, HEAD), F32)],
        compiler_params=_params(("parallel", "arbitrary")),
        name="rwkv7_chunked",
    )(p.reshape(bsz, seq, P_WIDTH), *prm)
    return out.reshape(bsz * seq, GW)


def _rwkv_params(mu, w0, w_up, a0, a_up, g_up, k_k, k_a, r_k, ln_w, ln_b):
    zeros = jnp.zeros((64, GW), F32)
    wup = jnp.concatenate([w_up, zeros], axis=0).astype(BF16)
    aup = jnp.concatenate([zeros, a_up], axis=0).astype(BF16)
    head = jnp.arange(GW) // HEAD
    hsum = (head[:, None] == head[None, :]).astype(BF16)
    row = lambda t: t.reshape(1, -1)
    return (row(mu), row(w0), wup, row(a0), aup, g_up.astype(BF16), row(k_k), row(k_a),
            row(r_k), row(ln_w), row(ln_b), hsum)


def _ssd_kernel(p_ref, cw_ref, cb_ref, dtb_ref, a_ref, dsk_ref, nw_ref, exp_ref, o_ref, tail_ref, state_ref):
    @pl.when(pl.program_id(1) == 0)
    def _():
        tail_ref[...] = jnp.zeros_like(tail_ref)
        state_ref[...] = jnp.zeros_like(state_ref)

    for j in range(p_ref.shape[0]):
        _ssd_chunk(p_ref.at[j], cw_ref, cb_ref, dtb_ref, a_ref, dsk_ref, nw_ref, exp_ref, o_ref.at[j],
                   tail_ref.at[j], state_ref.at[j])


def _ssd_chunk(p_ref, cw_ref, cb_ref, dtb_ref, a_ref, dsk_ref, nw_ref, exp_ref, o_ref, tail_ref, state_ref):
    c = SSD_CHUNK
    blk = p_ref[...].astype(F32)
    z = blk[:, :GW]
    raw = blk[:, GW:GW + SSD_CONV_CH]
    ext = jnp.concatenate([tail_ref[...], raw], axis=0)
    tail_ref[...] = raw[c - 8:c, :]
    conv = cb_ref[...]
    for i in range(SSD_CONV):
        off = 8 - (SSD_CONV - 1) + i
        conv = conv + cw_ref[i:i + 1, :] * ext[off:off + c, :]
    xbc = _silu(conv)
    x = xbc[:, :GW]
    bm = xbc[:, GW:GW + 256].astype(BF16)
    cm = xbc[:, GW + 256:GW + 512].astype(BF16)

    dt = _softplus(blk[:, GW + SSD_CONV_CH:GW + SSD_CONV_CH + 128] + dtb_ref[...])
    da = dt * a_ref[...]
    ri = lax.broadcasted_iota(jnp.int32, (c, c), 0)
    ci = lax.broadcasted_iota(jnp.int32, (c, c), 1)
    lower_eq = ri >= ci
    tril = lower_eq.astype(BF16)
    cs = _split_mm(tril, da, 3)
    cs_t = cs.T
    spread = exp_ref[...]
    dt_w = _split_mm_rhs(dt, spread, 3)
    cs_w = _split_mm(tril, _split_mm_rhs(da, spread, 3), 3)
    cs_end = cs_w[c - 1:c, :]
    xdt = x * dt_w
    xdt16 = xdt.astype(BF16)
    xdec16 = (xdt * jnp.exp(cs_end - cs_w)).astype(BF16)
    dec_all = jnp.exp(cs_end)

    heads = range(N_HEADS)
    groups = range(N_HEADS // 4)
    sls = [slice(HEAD * h, HEAD * (h + 1)) for h in heads]
    bg = [bm[:, SSD_STATE * g:SSD_STATE * (g + 1)] for g in groups]
    cg = [cm[:, SSD_STATE * g:SSD_STATE * (g + 1)] for g in groups]
    cb = [_nt(cg[g], bg[g]) for g in groups]
    s0 = [state_ref[h] for h in heads]
    lmat = [jnp.exp(jnp.where(lower_eq, cs[:, h:h + 1] - cs_t[h:h + 1, :], -jnp.inf)) for h in heads]
    y_in = [_mm((cb[h // 4] * lmat[h]).astype(BF16), xdt16[:, sls[h]]) for h in heads]
    y_st = [_nt(cg[h // 4], s0[h].astype(BF16)) for h in heads]
    new = [_tn(xdec16[:, sls[h]], bg[h // 4]) for h in heads]
    for h in heads:
        state_ref[h] = s0[h] * dec_all[:, HEAD * h:HEAD * h + 1] + new[h]
    y = (jnp.concatenate(y_in, axis=-1) + jnp.concatenate(y_st, axis=-1) * jnp.exp(cs_w) + x * dsk_ref[...])
    y = y * _silu(z)
    half = GW // 2
    y = jnp.concatenate([_rms(y[:, :half], SSD_NORM_EPS), _rms(y[:, half:], SSD_NORM_EPS)], axis=-1)
    o_ref[...] = (y * nw_ref[...]).astype(o_ref.dtype)


def _ssd(p, prm, bsz, seq):
    c = SSD_CHUNK
    nc = seq // c
    nseq = math.gcd(bsz, SSD_SEQS)
    col = SEC_S // SEC_W
    vec = lambda width: pl.BlockSpec((1, width), lambda b, n: (0, 0))
    out = pl.pallas_call(
        _ssd_kernel,
        grid=(bsz // nseq, nc),
        in_specs=[
            pl.BlockSpec((nseq, c, SEC_W), lambda b, n: (b, n, col)),
            pl.BlockSpec((SSD_CONV, SSD_CONV_CH), lambda b, n: (0, 0)),
            vec(SSD_CONV_CH), vec(128), vec(128), vec(GW), vec(GW),
            pl.BlockSpec((128, GW), lambda b, n: (0, 0)),
        ],
        out_specs=pl.BlockSpec((nseq, c, GW), lambda b, n: (b, n, 0)),
        out_shape=jax.ShapeDtypeStruct((bsz, seq, GW), BF16),
        scratch_shapes=[pltpu.VMEM((nseq, 8, SSD_CONV_CH), F32),
                        pltpu.VMEM((nseq, N_HEADS, HEAD, SSD_STATE), F32)],
        compiler_params=_params(("parallel", "arbitrary")),
        name="mamba2_ssd",
    )(p.reshape(bsz, seq, P_WIDTH), *prm)
    return out.reshape(bsz * seq, GW)


def _ssd_params(conv_w, conv_b, dt_bias, a_log, d_skip, norm_w):
    pad = lambda t: jnp.zeros((1, 128), F32).at[0, :N_HEADS].set(t)
    spread = (jnp.arange(128)[:, None] == jnp.arange(GW)[None, :] // HEAD).astype(BF16)
    return (conv_w, conv_b.reshape(1, -1), pad(dt_bias), pad(-jnp.exp(a_log)),
            jnp.repeat(d_skip, HEAD).reshape(1, -1), norm_w.reshape(1, -1), spread)


def _gelu_tanh(x):
    return 0.5 * x * (1.0 + jnp.tanh(math.sqrt(2.0 / math.pi) * (x + 0.044715 * (x * x * x))))


def _s5_kernel(u_ref, bbr_ref, bbi_ref, lvr_ref, lvi_ref, pwr_ref, pwi_ref, ccr_ref, cci_ref,
               dsk_ref, gw_ref, gb_ref, o_ref, hr_ref, hi_ref, carry_ref):
    @pl.when(pl.program_id(1) == 0)
    def _():
        carry_ref[...] = jnp.zeros_like(carry_ref)

    u = u_ref[...].astype(F32)
    u16 = u.astype(BF16)
    q_in, q_st = GW // S5_QUADS, S5_LANES // S5_QUADS
    for q in range(S5_QUADS):
        uq = u16[:, q * q_in:(q + 1) * q_in]
        hr_ref[:, q * q_st:(q + 1) * q_st] = _mm(uq, bbr_ref[q])
        hi_ref[:, q * q_st:(q + 1) * q_st] = _mm(uq, bbi_ref[q])

    def tile(t, carry):
        cr, ci = carry
        rows = pl.ds(pl.multiple_of(t * S5_TILE, S5_TILE), S5_TILE)
        xr = hr_ref[rows, :]
        xi = hi_ref[rows, :]
        for lvl, s in enumerate((1, 2, 4)):
            lr = lvr_ref[lvl]
            li = lvi_ref[lvl]
            sr = pltpu.roll(xr, s, axis=0)
            si = pltpu.roll(xi, s, axis=0)
            xr, xi = xr + lr * sr - li * si, xi + lr * si + li * sr
        pr = pwr_ref[...]
        pi = pwi_ref[...]
        xr, xi = xr + pr * cr - pi * ci, xi + pr * ci + pi * cr
        hr_ref[rows, :] = xr
        hi_ref[rows, :] = xi
        last = S5_TILE - 1
        return (jnp.broadcast_to(xr[last:last + 1, :], xr.shape), jnp.broadcast_to(xi[last:last + 1, :], xi.shape))

    cr, ci = lax.fori_loop(0, u.shape[0] // S5_TILE, tile, (carry_ref[0], carry_ref[1]))
    carry_ref[0] = cr
    carry_ref[1] = ci

    y = jnp.concatenate(
        [_mm(hr_ref[:, q * q_st:(q + 1) * q_st].astype(BF16), ccr_ref[q])
         - _mm(hi_ref[:, q * q_st:(q + 1) * q_st].astype(BF16), cci_ref[q]) for q in range(S5_QUADS)], axis=-1)
    y = _gelu_tanh(y + dsk_ref[...] * u)
    gate = jax.nn.sigmoid(_mm(y.astype(BF16), gw_ref[...]) + gb_ref[...])
    o_ref[...] = (y * gate).astype(o_ref.dtype)


def _s5(p, prm, bsz, seq):
    c = S5_CHUNK
    nc = seq // c
    col = SEC_F // GW
    full = lambda shape: pl.BlockSpec(shape, lambda b, n: (0,) * len(shape))
    return pl.pallas_call(
        _s5_kernel,
        grid=(bsz, nc),
        in_specs=[
            pl.BlockSpec((c, GW), lambda b, n: (b * nc + n, col)),
            full((S5_QUADS, GW // S5_QUADS, S5_LANES // S5_QUADS)),
            full((S5_QUADS, GW // S5_QUADS, S5_LANES // S5_QUADS)),
            full((3, S5_TILE, S5_LANES)), full((3, S5_TILE, S5_LANES)),
            full((S5_TILE, S5_LANES)), full((S5_TILE, S5_LANES)),
            full((S5_QUADS, S5_LANES // S5_QUADS, GW // S5_QUADS)),
            full((S5_QUADS, S5_LANES // S5_QUADS, GW // S5_QUADS)),
            full((1, GW)), full((GW, GW)), full((1, GW)),
        ],
        out_specs=pl.BlockSpec((c, GW), lambda b, n: (b * nc + n, 0)),
        out_shape=jax.ShapeDtypeStruct((bsz * seq, GW), BF16),
        scratch_shapes=[pltpu.VMEM((c, S5_LANES), F32), pltpu.VMEM((c, S5_LANES), F32),
                        pltpu.VMEM((2, S5_TILE, S5_LANES), F32)],
        compiler_params=_params(("parallel", "arbitrary")),
        name="s5_scan",
    )(p, *prm)


def _s5_params(lam_re, lam_im, log_step, b_re, b_im, c_re, c_im, d_skip, glu_w, glu_b):
    step = jnp.exp(log_step)[:, None]
    mag = jnp.exp(lam_re * step)
    ab_re, ab_im = mag * jnp.cos(lam_im * step), mag * jnp.sin(lam_im * step)
    den = lam_re * lam_re + lam_im * lam_im
    coef_re = ((ab_re - 1) * lam_re + ab_im * lam_im) / den
    coef_im = (ab_im * lam_re - (ab_re - 1) * lam_im) / den
    bb_re = coef_re[..., None] * b_re - coef_im[..., None] * b_im
    bb_im = coef_re[..., None] * b_im + coef_im[..., None] * b_re
    gq = S5_GROUPS // S5_QUADS
    eye = jnp.eye(gq, dtype=F32)
    in_proj = lambda t: jnp.einsum(
        'qgni,gh->qgihn', t.reshape(S5_QUADS, gq, S5_STATE, S5_GROUP_CH), eye
    ).reshape(S5_QUADS, gq * S5_GROUP_CH, gq * S5_STATE).astype(BF16)
    out_proj = lambda t: jnp.einsum(
        'qgin,gh->qgnhi', t.reshape(S5_QUADS, gq, S5_GROUP_CH, S5_STATE), eye
    ).reshape(S5_QUADS, gq * S5_STATE, gq * S5_GROUP_CH).astype(BF16)

    def cmul(a, b):
        return a[0] * b[0] - a[1] * b[1], a[0] * b[1] + a[1] * b[0]

    lam1 = (ab_re.reshape(-1), ab_im.reshape(-1))
    lam2 = cmul(lam1, lam1)
    lam4 = cmul(lam2, lam2)
    rows = jnp.arange(S5_TILE)[:, None]
    lvl_re = jnp.stack([jnp.where(rows >= s, l[0][None, :], 0.0) for s, l in ((1, lam1), (2, lam2), (4, lam4))])
    lvl_im = jnp.stack([jnp.where(rows >= s, l[1][None, :], 0.0) for s, l in ((1, lam1), (2, lam2), (4, lam4))])
    powers = [lam1]
    for _ in range(S5_TILE - 1):
        powers.append(cmul(powers[-1], lam1))
    pw_re = jnp.stack([q[0] for q in powers])
    pw_im = jnp.stack([q[1] for q in powers])
    return (in_proj(bb_re), in_proj(bb_im), lvl_re, lvl_im, pw_re, pw_im, out_proj(c_re), out_proj(c_im),
            d_skip.reshape(1, -1), glu_w.astype(BF16), glu_b.reshape(1, -1))


def _outproj_kernel(ya_ref, yr_ref, ys_ref, yf_ref, w_ref, x_ref, g_ref, nw_ref, o_ref):
    acc = _mm(ya_ref[...], w_ref[0, 0:GW, :])
    acc = acc + _mm(yr_ref[...], w_ref[0, GW:2 * GW, :])
    acc = acc + _mm(ys_ref[...], w_ref[0, 2 * GW:3 * GW, :])
    acc = acc + _mm(yf_ref[...], w_ref[0, 3 * GW:4 * GW, :])
    o_ref[...] = x_ref[...] + g_ref[0] * (_rms(acc, NORM_EPS) * nw_ref[...])


def _out_projection(ys, w, layer, x2, gate, nw, seq):
    t = x2.shape[0]
    tm = 512
    per_b = seq // tm
    ymap = pl.BlockSpec((tm, GW), lambda i: (i, 0))
    return pl.pallas_call(
        _outproj_kernel,
        grid=(t // tm,),
        in_specs=[
            ymap, ymap, ymap, ymap,
            pl.BlockSpec((1, D_MODEL, D_MODEL), lambda i: (layer, 0, 0)),
            pl.BlockSpec((tm, D_MODEL), lambda i: (i, 0)),
            pl.BlockSpec((1, 1, D_MODEL), lambda i: (i // per_b, 0, 0)),
            pl.BlockSpec((1, D_MODEL), lambda i: (0, 0)),
        ],
        out_specs=pl.BlockSpec((tm, D_MODEL), lambda i: (i, 0)),
        out_shape=jax.ShapeDtypeStruct((t, D_MODEL), F32),
        compiler_params=_params(("parallel",)),
        name="out_projection",
    )(*ys, w, x2, gate, nw)


def _ffn_kernel(x_ref, xn_ref, npre_ref, sc_ref, sh_ref, scn_ref, shn_ref, wg_ref, wu_ref, wd_ref, g_ref,
                npost_ref, o_ref, h_ref, acc_ref):
    i = pl.program_id(0)
    f = pl.program_id(1)
    last = pl.num_programs(1) - 1
    slot = i % 2

    @pl.when((i == 0) & (f == 0))
    def _():
        h_ref[0] = _prenorm(x_ref[...], npre_ref[...], sc_ref[0], sh_ref[0]).astype(BF16)

    def partial():
        h = h_ref[slot]
        act = _silu(_mm(h, wg_ref[0])) * _mm(h, wu_ref[0])
        return _mm(act.astype(BF16), wd_ref[0])

    @pl.when(f == 0)
    def _():
        acc_ref[...] = partial()

    @pl.when((f > 0) & (f < last))
    def _():
        acc_ref[...] += partial()

    @pl.when(f == last)
    def _():
        acc = acc_ref[...] + partial()
        h_ref[1 - slot] = _prenorm(xn_ref[...], npre_ref[...], scn_ref[0], shn_ref[0]).astype(BF16)
        o_ref[...] = x_ref[...] + g_ref[0] * (_rms(acc, NORM_EPS) * npost_ref[...])


def _dense_ffn(x2, npre, scale, shift, wg, wu, wd, layer, gate, npost, seq):
    t = x2.shape[0]
    tm, tf = 512, 512
    per_b = seq // tm
    n_i = t // tm
    assert D_FF // tf >= 3
    nxt = lambda i: jnp.minimum(i + 1, n_i - 1)
    mod = pl.BlockSpec((1, 1, D_MODEL), lambda i, f: (i // per_b, 0, 0))
    mod_next = pl.BlockSpec((1, 1, D_MODEL), lambda i, f: (nxt(i) // per_b, 0, 0))
    vec = pl.BlockSpec((1, D_MODEL), lambda i, f: (0, 0))
    return pl.pallas_call(
        _ffn_kernel,
        grid=(n_i, D_FF // tf),
        in_specs=[
            pl.BlockSpec((tm, D_MODEL), lambda i, f: (i, 0)),
            pl.BlockSpec((tm, D_MODEL), lambda i, f: (nxt(i), 0)),
            vec, mod, mod, mod_next, mod_next,
            pl.BlockSpec((1, D_MODEL, tf), lambda i, f: (layer, 0, f)),
            pl.BlockSpec((1, D_MODEL, tf), lambda i, f: (layer, 0, f)),
            pl.BlockSpec((1, tf, D_MODEL), lambda i, f: (layer, f, 0)),
            mod, vec,
        ],
        out_specs=pl.BlockSpec((tm, D_MODEL), lambda i, f: (i, 0)),
        out_shape=jax.ShapeDtypeStruct((t, D_MODEL), F32),
        scratch_shapes=[pltpu.VMEM((2, tm, D_MODEL), BF16), pltpu.VMEM((tm, D_MODEL), F32)],
        compiler_params=_params(("arbitrary", "arbitrary")),
        name="dense_swiglu",
    )(x2, x2, npre, scale, shift, scale, shift, wg, wu, wd, gate, npost)


def _router_kernel(x_ref, npre_ref, sc_ref, sh_ref, rw_ref, rb_ref, h_ref, lg_ref):
    h = _prenorm(x_ref[...], npre_ref[...], sc_ref[0], sh_ref[0])
    h_ref[...] = h.astype(BF16)
    lg_ref[...] = _mm(h, rw_ref[...], precision=HIGHEST) + rb_ref[...]


def _router(x2, npre, scale, shift, rw, rb, seq):
    t = x2.shape[0]
    tm = 512
    per_b = seq // tm
    mod = pl.BlockSpec((1, 1, D_MODEL), lambda i: (i // per_b, 0, 0))
    return pl.pallas_call(
        _router_kernel,
        grid=(t // tm,),
        in_specs=[
            pl.BlockSpec((tm, D_MODEL), lambda i: (i, 0)),
            pl.BlockSpec((1, D_MODEL), lambda i: (0, 0)),
            mod, mod,
            pl.BlockSpec((D_MODEL, 128), lambda i: (0, 0)),
            pl.BlockSpec((1, 128), lambda i: (0, 0)),
        ],
        out_specs=[pl.BlockSpec((tm, D_MODEL), lambda i: (i, 0)), pl.BlockSpec((tm, 128), lambda i: (i, 0))],
        out_shape=[jax.ShapeDtypeStruct((t, D_MODEL), BF16), jax.ShapeDtypeStruct((t, 128), F32)],
        compiler_params=_params(("parallel",)),
        name="moe_router",
    )(x2, npre, scale, shift, rw, rb)


def _moe_ffn_kernel(be_ref, used_ref, xs_ref, wg_ref, wu_ref, wd_ref, o_ref, acc_ref, *, nf):
    i = pl.program_id(0)
    f = pl.program_id(1)
    last = nf - 1
    live = i < used_ref[0]

    def partial():
        xs = xs_ref[...]
        act = _silu(_mm(xs, wg_ref[0, 0])) * _mm(xs, wu_ref[0, 0])
        return _mm(act.astype(BF16), wd_ref[0, 0])

    @pl.when(live & (f == 0))
    def _():
        acc_ref[...] = partial()

    if nf > 2:
        @pl.when(live & (f > 0) & (f < last))
        def _():
            acc_ref[...] += partial()

    @pl.when(live & (f == last))
    def _():
        o_ref[...] = (acc_ref[...] + partial()).astype(o_ref.dtype)

    @pl.when(jnp.logical_not(live) & (f == last))
    def _():
        o_ref[...] = jnp.zeros_like(o_ref)


def _moe_ffn(block_e, n_used, xs, wg, wu, wd, layer):
    rows = xs.shape[0]
    tf = D_EXPERT // 2
    nf = D_EXPERT // tf

    def tile(i, f, used):
        return jnp.where(i < used[0], f, nf - 1)

    grid_spec = pltpu.PrefetchScalarGridSpec(
        num_scalar_prefetch=2,
        grid=(rows // MOE_BLOCK, nf),
        in_specs=[
            pl.BlockSpec((MOE_BLOCK, D_MODEL), lambda i, f, be, used: (i, 0)),
            pl.BlockSpec((1, 1, D_MODEL, tf), lambda i, f, be, used: (layer, be[i], 0, tile(i, f, used))),
            pl.BlockSpec((1, 1, D_MODEL, tf), lambda i, f, be, used: (layer, be[i], 0, tile(i, f, used))),
            pl.BlockSpec((1, 1, tf, D_MODEL), lambda i, f, be, used: (layer, be[i], tile(i, f, used), 0)),
        ],
        out_specs=pl.BlockSpec((MOE_BLOCK, D_MODEL), lambda i, f, be, used: (i, 0)),
        scratch_shapes=[pltpu.VMEM((MOE_BLOCK, D_MODEL), F32)],
    )
    assert nf >= 2
    return pl.pallas_call(
        functools.partial(_moe_ffn_kernel, nf=nf),
        grid_spec=grid_spec,
        out_shape=jax.ShapeDtypeStruct((rows, D_MODEL), BF16),
        compiler_params=_params(("parallel", "arbitrary"), MOE_VMEM_LIMIT),
        name="moe_swiglu",
    )(block_e, n_used, xs, wg, wu, wd)


def _combine_kernel(y0_ref, y1_ref, p_ref, x_ref, g_ref, npost_ref, o_ref):
    p = p_ref[...]
    y = y0_ref[...].astype(F32) * p[:, 0:1] + y1_ref[...].astype(F32) * p[:, 1:2]
    o_ref[...] = x_ref[...] + g_ref[0] * (_rms(y, NORM_EPS) * npost_ref[...])


def _moe_combine(y0, y1, top_p, x2, gate, npost, seq):
    t = x2.shape[0]
    tm = 512
    per_b = seq // tm
    row = pl.BlockSpec((tm, D_MODEL), lambda i: (i, 0))
    return pl.pallas_call(
        _combine_kernel,
        grid=(t // tm,),
        in_specs=[row, row, pl.BlockSpec((tm, 128), lambda i: (i, 0)), row,
                  pl.BlockSpec((1, 1, D_MODEL), lambda i: (i // per_b, 0, 0)),
                  pl.BlockSpec((1, D_MODEL), lambda i: (0, 0))],
        out_specs=row,
        out_shape=jax.ShapeDtypeStruct((t, D_MODEL), F32),
        compiler_params=_params(("parallel",)),
        name="moe_combine",
    )(y0, y1, top_p, x2, gate, npost)


def _routed_ffn(x2, npre, scale, shift, rw, rb, wg, wu, wd, layer, gate, npost, seq):
    t = x2.shape[0]
    rw_pad = jnp.zeros((D_MODEL, 128), F32).at[:, :N_EXPERTS].set(rw)
    rb_pad = jnp.zeros((1, 128), F32).at[0, :N_EXPERTS].set(rb)
    h, logits = _router(x2, npre, scale, shift, rw_pad, rb_pad, seq)
    logits = logits[:, :N_EXPERTS]
    top_logit, top_idx = lax.top_k(logits, TOP_K)
    top_p = jax.nn.softmax(top_logit, axis=-1)
    n_assign = t * TOP_K
    flat_e = top_idx.reshape(-1).astype(jnp.int32)
    order = jnp.argsort(flat_e).astype(jnp.int32)
    rank = jnp.argsort(order).astype(jnp.int32)
    counts = jnp.sum((flat_e[:, None] == jnp.arange(N_EXPERTS, dtype=jnp.int32)[None, :]).astype(jnp.int32), axis=0)
    padded = (counts + MOE_BLOCK - 1) // MOE_BLOCK * MOE_BLOCK
    pad_end = jnp.cumsum(padded)
    pad_start = pad_end - padded
    start = jnp.cumsum(counts) - counts
    n_blocks = -(-n_assign // MOE_BLOCK) + N_EXPERTS
    rows = n_blocks * MOE_BLOCK
    block_e = jnp.minimum(jnp.searchsorted(pad_end, jnp.arange(n_blocks) * MOE_BLOCK, side='right'),
                          N_EXPERTS - 1).astype(jnp.int32)
    e_row = jnp.repeat(block_e, MOE_BLOCK)
    off = jnp.arange(rows, dtype=jnp.int32) - pad_start[e_row]
    src = order[jnp.clip(start[e_row] + off, 0, n_assign - 1)]
    row_tok = jnp.where(off < counts[e_row], src // TOP_K, t)
    h_pad = jnp.concatenate([h, jnp.zeros((1, D_MODEL), h.dtype)], axis=0)
    xs = h_pad[row_tok]
    n_used = (pad_end[-1:] // MOE_BLOCK).astype(jnp.int32)
    yb = _moe_ffn(block_e, n_used, xs, wg, wu, wd, layer)
    pos = (pad_start[flat_e] + rank - start[flat_e]).reshape(t, TOP_K)
    p_pad = jnp.zeros((t, 128), F32).at[:, :TOP_K].set(top_p)
    return _moe_combine(yb[pos[:, 0]], yb[pos[:, 1]], p_pad, x2, gate, npost, seq)


def _in_weights(w_in):
    a0, r0, s0, f0 = 0, ATTN_IN, ATTN_IN + RWKV_IN, ATTN_IN + RWKV_IN + SSD_IN
    zeros = jnp.zeros(w_in.shape[:-1] + (SEC_W - SSD_IN,), w_in.dtype)
    return jnp.concatenate([
        w_in[..., r0:s0],
        w_in[..., s0:f0], zeros,
        w_in[..., f0:f0 + GW],
        w_in[..., a0:r0],
    ], axis=-1).astype(BF16)


def kernel(x, c, ada_w, ada_b, norm_pre, norm_post, w_in, w_out, attn_sink, rwkv_mu, rwkv_w0, rwkv_w_up, rwkv_a0, rwkv_a_up, rwkv_g_up, rwkv_k_k, rwkv_k_a, rwkv_r_k, rwkv_ln_w, rwkv_ln_b, ssd_conv_w, ssd_conv_b, ssd_dt_bias, ssd_a_log, ssd_d, ssd_norm_w, s5_lam_re, s5_lam_im, s5_log_step, s5_b_re, s5_b_im, s5_c_re, s5_c_im, s5_d, s5_glu_w, s5_glu_b, ffn_w_gate, ffn_w_up, ffn_w_down, moe_router_w, moe_router_b, moe_w_gate, moe_w_up, moe_w_down):
    bsz, seq, d = x.shape
    depth = ada_w.shape[0]
    mod = _ada_modulation(c, ada_w, ada_b)
    mod = mod.reshape(depth, 2, bsz, 3, 1, d)
    x2 = x.reshape(bsz * seq, d)
    w_in16 = _in_weights(w_in)
    w_out16 = w_out.astype(BF16)
    ffn16 = (ffn_w_gate.astype(BF16), ffn_w_up.astype(BF16), ffn_w_down.astype(BF16))
    moe16 = (moe_w_gate.astype(BF16), moe_w_up.astype(BF16), moe_w_down.astype(BF16))
    for i in range(depth):
        shift, scale, gate = mod[i, 0, :, 0], mod[i, 0, :, 1], mod[i, 0, :, 2]
        p = _in_projection(x2, norm_pre[i, 0][None], scale, shift, w_in16, i, seq)
        y_attn = _attention(p, attn_sink[i][None], bsz, seq)
        y_rwkv = _rwkv(p, _rwkv_params(rwkv_mu[i], rwkv_w0[i], rwkv_w_up[i], rwkv_a0[i], rwkv_a_up[i],
                                       rwkv_g_up[i], rwkv_k_k[i], rwkv_k_a[i], rwkv_r_k[i].reshape(-1),
                                       rwkv_ln_w[i], rwkv_ln_b[i]), bsz, seq)
        y_ssd = _ssd(p, _ssd_params(ssd_conv_w[i], ssd_conv_b[i], ssd_dt_bias[i], ssd_a_log[i], ssd_d[i],
                                    ssd_norm_w[i]), bsz, seq)
        y_s5 = _s5(p, _s5_params(s5_lam_re[i], s5_lam_im[i], s5_log_step[i], s5_b_re[i], s5_b_im[i],
                                 s5_c_re[i], s5_c_im[i], s5_d[i], s5_glu_w[i], s5_glu_b[i]), bsz, seq)
        x2 = _out_projection((y_attn, y_rwkv, y_ssd, y_s5), w_out16, i, x2, gate, norm_post[i, 0][None], seq)
        shift, scale, gate = mod[i, 1, :, 0], mod[i, 1, :, 1], mod[i, 1, :, 2]
        j = i // 2
        if i % 2 == 0:
            x2 = _dense_ffn(x2, norm_pre[i, 1][None], scale, shift, *ffn16, j, gate, norm_post[i, 1][None], seq)
        else:
            x2 = _routed_ffn(x2, norm_pre[i, 1][None], scale, shift, moe_router_w[j], moe_router_b[j],
                             *moe16, j, gate, norm_post[i, 1][None], seq)
    return x2.reshape(bsz, seq, d)
```
